```python
import math
import jax
import jax.numpy as jnp
from jax import lax
import numpy as np

D_MODEL = 1024
BATCH = 8
SEQ = 4096
DEPTH = 2

HEAD_DIM = 64
N_HEADS_FOX = D_MODEL // (2 * HEAD_DIM)
N_HEADS_DIL = D_MODEL // (2 * HEAD_DIM)
FOX_WIDTH = N_HEADS_FOX * HEAD_DIM
DIL_WIDTH = N_HEADS_DIL * HEAD_DIM
ATTN_MIX_WIDTH = FOX_WIDTH + DIL_WIDTH
ATTN_IN_WIDTH = 3 * FOX_WIDTH + N_HEADS_FOX + 3 * DIL_WIDTH
QUERY_BLOCK = 128
DIL_PATTERNS = ((128, 1), (512, 4), (2048, 16))
FOX_FORGET_BIAS_INIT = 2.0

GDN_HEAD_DIM = 128
N_HEADS_GDN = D_MODEL // GDN_HEAD_DIM
GDN_WIDTH = N_HEADS_GDN * GDN_HEAD_DIM
GDN_IN_WIDTH = 3 * GDN_WIDTH + 2 * N_HEADS_GDN + GDN_WIDTH
GDN_CONV = 4
GDN_CHUNK = 64

D_FF = D_MODEL * 7 // 2
N_EXPERTS = 8
TOP_K = 2

N_EVEN = (DEPTH + 1) // 2
N_ODD = DEPTH // 2
DEEPNORM_ALPHA = (2.0 * DEPTH) ** 0.25
DEEPNORM_BETA = (8.0 * DEPTH) ** -0.25
LN_EPS = 1e-5
RMS_EPS = 1e-6
NEG_INF = -1e30

kernel_name = 'fox_dilated_gdn_moe_deepnorm_hybrid'


def layer_norm(x, g, b):
    x32 = x.astype(jnp.float32)
    mu = jnp.mean(x32, -1, keepdims=True)
    var = jnp.mean(jnp.square(x32 - mu), -1, keepdims=True)
    y = (x32 - mu) * lax.rsqrt(var + LN_EPS) * g.astype(jnp.float32) + b.astype(jnp.float32)
    return y.astype(x.dtype)


def alibi_slopes(n):
    return jnp.asarray(2.0 ** (-8.0 * (np.arange(n) + 1) / n), dtype=jnp.float32)


def split_heads(t, n, dh):
    B, S, _ = t.shape
    return t.reshape(B, S, n, dh).transpose(0, 2, 1, 3)


def merge_heads(t):
    B, n, S, dh = t.shape
    return t.transpose(0, 2, 1, 3).reshape(B, S, n * dh)


def forgetting_attention(q, k, v, log_f):
    B, H, S, Dh = q.shape
    nb = S // QUERY_BLOCK
    cum = jnp.cumsum(log_f, axis=-1)
    qb = q.reshape(B, H, nb, QUERY_BLOCK, Dh).transpose(2, 0, 1, 3, 4)
    cb = cum.reshape(B, H, nb, QUERY_BLOCK).transpose(2, 0, 1, 3)
    key_pos = jnp.arange(S)
    scale = Dh ** -0.5

    def one_block(args):
        q_blk, c_blk, blk = args
        s = jnp.einsum('bhqd,bhkd->bhqk', q_blk, k, preferred_element_type=jnp.float32) * scale
        s = s + (c_blk[..., :, None] - cum[..., None, :])
        q_pos = blk * QUERY_BLOCK + jnp.arange(QUERY_BLOCK)
        s = jnp.where(key_pos[None, :] <= q_pos[:, None], s, NEG_INF)
        p = jax.nn.softmax(s, axis=-1)
        return jnp.einsum('bhqk,bhkd->bhqd', p.astype(v.dtype), v)

    out = lax.map(one_block, (qb, cb, jnp.arange(nb)))
    return out.transpose(1, 2, 0, 3, 4).reshape(B, H, S, Dh)


def dilated_window_branch(q, k, v, slopes, window, dilation):
    B, H, S, Dh = q.shape
    L = S // dilation
    nb = -(-L // QUERY_BLOCK)
    Lp = nb * QUERY_BLOCK
    span = window // dilation

    def to_sub(t):
        t = jnp.moveaxis(t.reshape(B, H, L, dilation, Dh), 3, 2)
        t = jnp.pad(t, ((0, 0), (0, 0), (0, 0), (0, Lp - L), (0, 0)))
        return t.reshape(B, H, dilation, nb, QUERY_BLOCK, Dh)

    def with_prev(t):
        prev = jnp.pad(t, ((0, 0), (0, 0), (0, 0), (1, 0), (0, 0), (0, 0)))[:, :, :, :nb]
        return jnp.concatenate([prev, t], axis=4)

    def from_sub(t):
        rest = t.shape[5:]
        t = t.reshape((B, H, dilation, Lp) + rest)[:, :, :, :L]
        return jnp.moveaxis(t, 2, 3).reshape((B, H, S) + rest)

    qs = to_sub(q)
    kw = with_prev(to_sub(k))
    vw = with_prev(to_sub(v))
    qi = jnp.arange(QUERY_BLOCK)[:, None]
    ki = jnp.arange(2 * QUERY_BLOCK)[None, :]
    delta = QUERY_BLOCK + qi - ki
    key_sub = (jnp.arange(nb)[:, None, None] - 1) * QUERY_BLOCK + ki[None]
    valid = (delta >= 0) & (delta <= span) & (key_sub >= 0)
    s = jnp.einsum('bhrnqd,bhrnkd->bhrnqk', qs, kw, preferred_element_type=jnp.float32) * (Dh ** -0.5)
    s = s - slopes[None, :, None, None, None, None] * (delta * dilation).astype(jnp.float32)
    s = jnp.where(valid, s, NEG_INF)
    m = jnp.max(s, -1)
    p = jnp.exp(s - m[..., None])
    l = jnp.sum(p, -1)
    o = jnp.einsum('bhrnqk,bhrnkd->bhrnqd', p, vw.astype(jnp.float32)) / l[..., None]
    return from_sub(o), from_sub(m), from_sub(l)


def dilated_mixture_attention(q, k, v):
    slopes = alibi_slopes(q.shape[1])
    branches = [dilated_window_branch(q, k, v, slopes, w, d) for (w, d) in DIL_PATTERNS]
    o = jnp.stack([br[0] for br in branches])
    m = jnp.stack([br[1] for br in branches])
    l = jnp.stack([br[2] for br in branches])
    wts = l * jnp.exp(m - jnp.max(m, 0))
    return jnp.sum(wts[..., None] * o, 0) / jnp.sum(wts, 0)[..., None]


def fox_dilated_mixer(x, w_in, forget_bias, w_out):
    h = x @ w_in
    sizes = (FOX_WIDTH, FOX_WIDTH, FOX_WIDTH, N_HEADS_FOX, DIL_WIDTH, DIL_WIDTH, DIL_WIDTH)
    qa, ka, va, fa, qb, kb, vb = jnp.split(h, np.cumsum(sizes)[:-1].tolist(), axis=-1)
    log_f = jax.nn.log_sigmoid(fa.astype(jnp.float32) + forget_bias.astype(jnp.float32)).transpose(0, 2, 1)
    oa = forgetting_attention(split_heads(qa, N_HEADS_FOX, HEAD_DIM), split_heads(ka, N_HEADS_FOX, HEAD_DIM),
                              split_heads(va, N_HEADS_FOX, HEAD_DIM), log_f)
    ob = dilated_mixture_attention(split_heads(qb, N_HEADS_DIL, HEAD_DIM), split_heads(kb, N_HEADS_DIL, HEAD_DIM),
                                   split_heads(vb, N_HEADS_DIL, HEAD_DIM))
    o = jnp.concatenate([oa.astype(x.dtype), ob.astype(x.dtype)], axis=1)
    return merge_heads(o) @ w_out


def causal_depthwise_conv(t, w):
    C = t.shape[-1]
    return lax.conv_general_dilated(t.astype(jnp.float32), w.astype(jnp.float32)[:, None, :],
                                    window_strides=(1,), padding=((GDN_CONV - 1, 0),),
                                    dimension_numbers=('NWC', 'WIO', 'NWC'), feature_group_count=C)


def l2_normalize(t):
    return t * lax.rsqrt(jnp.sum(jnp.square(t), -1, keepdims=True) + RMS_EPS)


def chunk_gated_delta_rule(q, k, v, g, beta):
    B, H, S, Dk = q.shape
    Dv = v.shape[-1]
    C = GDN_CHUNK
    N = S // C
    f32 = jnp.float32
    q, k, v, g, beta = (t.astype(f32).reshape((B, H, N, C) + t.shape[3:]) for t in (q, k, v, g, beta))
    gam = jnp.cumsum(g, -1)
    incl = jnp.tril(jnp.ones((C, C), bool))
    strict = jnp.tril(jnp.ones((C, C), bool), -1)
    diff = gam[..., :, None] - gam[..., None, :]
    decay = jnp.where(incl, jnp.exp(jnp.where(incl, diff, 0.0)), 0.0)
    a_strict = jnp.where(strict, beta[..., :, None] * jnp.einsum('bhncd,bhnjd->bhncj', k, k) * decay, 0.0)
    rhs = jnp.concatenate([v * beta[..., None], k * (beta * jnp.exp(gam))[..., None]], -1)
    sol = lax.linalg.triangular_solve(a_strict + jnp.eye(C, dtype=f32), rhs, left_side=True,
                                      lower=True, unit_diagonal=True)
    u, w = sol[..., :Dv], sol[..., Dv:]
    qk = jnp.einsum('bhncd,bhnjd->bhncj', q, k) * decay
    q_dec = q * jnp.exp(gam)[..., None]
    k_dec = k * jnp.exp(gam[..., -1:] - gam)[..., None]
    g_tot = jnp.exp(gam[..., -1])

    def step(state, inp):
        u_c, w_c, qk_c, qd_c, kd_c, gt_c = inp
        v_new = u_c - jnp.einsum('bhck,bhkv->bhcv', w_c, state)
        o_c = jnp.einsum('bhck,bhkv->bhcv', qd_c, state) + jnp.einsum('bhcj,bhjv->bhcv', qk_c, v_new)
        state = state * gt_c[..., None, None] + jnp.einsum('bhck,bhcv->bhkv', kd_c, v_new)
        return state, o_c

    xs = tuple(jnp.moveaxis(t, 2, 0) for t in (u, w, qk, q_dec, k_dec, g_tot))
    _, o = lax.scan(step, jnp.zeros((B, H, Dk, Dv), f32), xs)
    return jnp.moveaxis(o, 0, 2).reshape(B, H, S, Dv)


def gated_deltanet_mixer(x, w_in, conv_w, a_log, dt_bias, norm_g, w_out):
    f32 = jnp.float32
    h = x @ w_in
    sizes = (3 * GDN_WIDTH, N_HEADS_GDN, N_HEADS_GDN, GDN_WIDTH)
    qkv, b_logit, a_logit, gate = jnp.split(h, np.cumsum(sizes)[:-1].tolist(), axis=-1)
    qkv = jax.nn.silu(causal_depthwise_conv(qkv, conv_w))
    q, k, v = jnp.split(qkv, 3, axis=-1)
    q = l2_normalize(split_heads(q, N_HEADS_GDN, GDN_HEAD_DIM)) * (GDN_HEAD_DIM ** -0.5)
    k = l2_normalize(split_heads(k, N_HEADS_GDN, GDN_HEAD_DIM))
    v = split_heads(v, N_HEADS_GDN, GDN_HEAD_DIM)
    beta = jax.nn.sigmoid(b_logit.astype(f32)).transpose(0, 2, 1)
    g = (-jnp.exp(a_log.astype(f32)) * jax.nn.softplus(a_logit.astype(f32) + dt_bias.astype(f32))).transpose(0, 2, 1)
    o = chunk_gated_delta_rule(q, k, v, g, beta)
    o = o * lax.rsqrt(jnp.mean(jnp.square(o), -1, keepdims=True) + RMS_EPS) * norm_g.astype(f32)
    o = o * jax.nn.silu(split_heads(gate.astype(f32), N_HEADS_GDN, GDN_HEAD_DIM))
    return merge_heads(o.astype(x.dtype)) @ w_out


def swiglu(x, w_gate, w_up, w_down):
    return (jax.nn.silu(x @ w_gate) * (x @ w_up)) @ w_down


def moe_swiglu(x, router, w_gate, w_up, w_down):
    logits = (x @ router).astype(jnp.float32)
    top_v, top_i = lax.top_k(logits, TOP_K)
    top_w = jax.nn.softmax(top_v, axis=-1)
    gate = jnp.sum(jax.nn.one_hot(top_i, N_EXPERTS, dtype=jnp.float32) * top_w[..., None], axis=-2)
    y = jnp.zeros(x.shape, jnp.float32)
    for e in range(N_EXPERTS):
        y = y + gate[..., e:e + 1] * swiglu(x, w_gate[e], w_up[e], w_down[e]).astype(jnp.float32)
    return y.astype(x.dtype)


def setup_inputs(seed: int = 0) -> dict:
    key = jax.random.key(seed)
    ks = jax.random.split(key, 32)
    f32 = jnp.float32

    def nrm(k, shape, fan_in, scale=1.0):
        return jax.random.normal(k, shape, f32) * (scale * fan_in ** -0.5)

    def gain(k, shape):
        return 1.0 + 0.02 * jax.random.normal(k, shape, f32)

    def bias(k, shape):
        return 0.02 * jax.random.normal(k, shape, f32)

    E, O, D = N_EVEN, N_ODD, D_MODEL
    dt = jnp.exp(jax.random.uniform(ks[14], (O, N_HEADS_GDN), f32, math.log(1e-3), math.log(1e-1)))
    return {
        'x': jax.random.normal(ks[0], (BATCH, SEQ, D), f32),
        'attn_w_in': nrm(ks[1], (E, D, ATTN_IN_WIDTH), D),
        'fox_forget_bias': FOX_FORGET_BIAS_INIT + 0.1 * jax.random.normal(ks[2], (E, N_HEADS_FOX), f32),
        'attn_w_out': nrm(ks[3], (E, ATTN_MIX_WIDTH, D), ATTN_MIX_WIDTH, DEEPNORM_BETA),
        'ln_attn_g': gain(ks[4], (E, D)),
        'ln_attn_b': bias(ks[5], (E, D)),
        'ffn_w_gate': nrm(ks[6], (E, D, D_FF), D),
        'ffn_w_up': nrm(ks[7], (E, D, D_FF), D),
        'ffn_w_down': nrm(ks[8], (E, D_FF, D), D_FF, DEEPNORM_BETA),
        'ln_ffn_g': gain(ks[9], (E, D)),
        'ln_ffn_b': bias(ks[10], (E, D)),
        'gdn_w_in': nrm(ks[11], (O, D, GDN_IN_WIDTH), D),
        'gdn_conv_w': nrm(ks[12], (O, GDN_CONV, 3 * GDN_WIDTH), GDN_CONV),
        'gdn_a_log': jnp.log(jax.random.uniform(ks[13], (O, N_HEADS_GDN), f32, 1.0, 16.0)),
        'gdn_dt_bias': dt + jnp.log(-jnp.expm1(-dt)),
        'gdn_norm_g': gain(ks[15], (O, GDN_HEAD_DIM)),
        'gdn_w_out': nrm(ks[16], (O, GDN_WIDTH, D), GDN_WIDTH, DEEPNORM_BETA),
        'ln_gdn_g': gain(ks[17], (O, D)),
        'ln_gdn_b': bias(ks[18], (O, D)),
        'moe_router': nrm(ks[19], (O, D, N_EXPERTS), D),
        'moe_w_gate': nrm(ks[20], (O, N_EXPERTS, D, D_FF), D),
        'moe_w_up': nrm(ks[21], (O, N_EXPERTS, D, D_FF), D),
        'moe_w_down': nrm(ks[22], (O, N_EXPERTS, D_FF, D), D_FF, DEEPNORM_BETA),
        'ln_moe_g': gain(ks[23], (O, D)),
        'ln_moe_b': bias(ks[24], (O, D)),
    }


def reference(x, attn_w_in, fox_forget_bias, attn_w_out, ln_attn_g, ln_attn_b,
              ffn_w_gate, ffn_w_up, ffn_w_down, ln_ffn_g, ln_ffn_b,
              gdn_w_in, gdn_conv_w, gdn_a_log, gdn_dt_bias, gdn_norm_g, gdn_w_out, ln_gdn_g, ln_gdn_b,
              moe_router, moe_w_gate, moe_w_up, moe_w_down, ln_moe_g, ln_moe_b):
    for layer in range(DEPTH):
        i = layer // 2
        if layer % 2 == 0:
            mix = fox_dilated_mixer(x, attn_w_in[i], fox_forget_bias[i], attn_w_out[i])
            x = layer_norm(DEEPNORM_ALPHA * x + mix, ln_attn_g[i], ln_attn_b[i])
            ffn = swiglu(x, ffn_w_gate[i], ffn_w_up[i], ffn_w_down[i])
            x = layer_norm(DEEPNORM_ALPHA * x + ffn, ln_ffn_g[i], ln_ffn_b[i])
        else:
            mix = gated_deltanet_mixer(x, gdn_w_in[i], gdn_conv_w[i], gdn_a_log[i], gdn_dt_bias[i],
                                       gdn_norm_g[i], gdn_w_out[i])
            x = layer_norm(DEEPNORM_ALPHA * x + mix, ln_gdn_g[i], ln_gdn_b[i])
            ffn = moe_swiglu(x, moe_router[i], moe_w_gate[i], moe_w_up[i], moe_w_down[i])
            x = layer_norm(DEEPNORM_ALPHA * x + ffn, ln_moe_g[i], ln_moe_b[i])
    return x
```

```python
import functools

import numpy as np
import jax
import jax.numpy as jnp
from jax import lax
from jax.experimental import pallas as pl
from jax.experimental.pallas import tpu as pltpu

F32 = jnp.float32
BF16 = jnp.bfloat16
HIGHEST = lax.Precision.HIGHEST

LANES = 128
SUBLANES = 8
VMEM_LIMIT = 52 * 1024 * 1024

HEAD_DIM = 64
N_HEADS_ATT = 8
ATT_WIDTH = N_HEADS_ATT * HEAD_DIM
QUERY_BLOCK = 128
DIL_PATTERNS = ((128, 1), (512, 4), (2048, 16))
GDN_HEAD_DIM = 128
N_HEADS_GDN = 8
GDN_WIDTH = N_HEADS_GDN * GDN_HEAD_DIM
GDN_CONV = 4
GDN_CHUNK = 64
N_EXPERTS = 8
DEPTH = 2
DEEPNORM_ALPHA = (2.0 * DEPTH) ** 0.25
LN_EPS = 1e-5
RMS_EPS = 1e-6
NEG_INF = -1e30

PROJ_TM = 1024
PROJ_TN = 512
FOX_TQ = 512
LN_TM = 512
FFN_TM = 512
FFN_TF = 512
GDN_BLK = 256
DISPATCH_CHUNK = 512


def _cparams(*sem):
    return pltpu.CompilerParams(dimension_semantics=sem, vmem_limit_bytes=VMEM_LIMIT)


def _iota(shape, dim):
    return lax.broadcasted_iota(jnp.int32, shape, dim)


def _silu(x):
    return x * jax.nn.sigmoid(x)


def _row_to_col(row):
    n = row.shape[1]
    eye = _iota((LANES, LANES), 0) == _iota((LANES, LANES), 1)
    cols = []
    for c in range(n // LANES):
        seg = row[:, c * LANES:(c + 1) * LANES]
        cols.append(jnp.sum(jnp.where(eye, seg, 0.0), axis=1, keepdims=True))
    return cols[0] if len(cols) == 1 else jnp.concatenate(cols, axis=0)


def _layer_norm_rows(z, g, b):
    mu = jnp.mean(z, axis=-1, keepdims=True)
    zc = z - mu
    var = jnp.mean(zc * zc, axis=-1, keepdims=True)
    return zc * lax.rsqrt(var + LN_EPS) * g + b


def _proj_kernel(x_ref, w_ref, o_ref):
    o_ref[...] = jnp.dot(x_ref[...], w_ref[...], preferred_element_type=F32).astype(o_ref.dtype)


def _proj(x, w, out_dtype):
    M, K = x.shape
    C = w.shape[1]
    tm = min(PROJ_TM, M)
    tn = min(PROJ_TN, C)
    assert M % tm == 0 and C % tn == 0
    return pl.pallas_call(
        _proj_kernel,
        out_shape=jax.ShapeDtypeStruct((M, C), out_dtype),
        grid=(M // tm, C // tn),
        in_specs=[pl.BlockSpec((tm, K), lambda i, j: (i, 0)),
                  pl.BlockSpec((K, tn), lambda i, j: (0, j))],
        out_specs=pl.BlockSpec((tm, tn), lambda i, j: (i, j)),
        compiler_params=_cparams("parallel", "parallel"),
        name="proj",
    )(x, w)


def _forget_cumsum_kernel(f_ref, b_ref, o_ref):
    S = f_ref.shape[2]
    z = f_ref[0] + b_ref[...]
    lf = jnp.minimum(z, 0.0) - jnp.log1p(jnp.exp(-jnp.abs(z)))
    upper = (_iota((LANES, LANES), 0) <= _iota((LANES, LANES), 1)).astype(F32)
    carry = jnp.zeros((z.shape[0], 1), F32)
    for c in range(S // LANES):
        seg = jnp.dot(lf[:, c * LANES:(c + 1) * LANES], upper, precision=HIGHEST,
                      preferred_element_type=F32) + carry
        o_ref[0, :, c * LANES:(c + 1) * LANES] = seg
        carry = seg[:, LANES - 1:LANES]


def _forget_cumsum(f_rows, bias):
    B, H, S = f_rows.shape
    return pl.pallas_call(
        _forget_cumsum_kernel,
        out_shape=jax.ShapeDtypeStruct((B, H, S), F32),
        grid=(B,),
        in_specs=[pl.BlockSpec((1, H, S), lambda b: (b, 0, 0)),
                  pl.BlockSpec((H, 1), lambda b: (0, 0))],
        out_specs=pl.BlockSpec((1, H, S), lambda b: (b, 0, 0)),
        compiler_params=_cparams("parallel"),
        name="forget_cumsum",
    )(f_rows, bias.reshape(H, 1).astype(F32))


def _fox_kernel(q_ref, k_ref, v_ref, fq_ref, fk_ref, o_ref, *, tq):
    i = pl.program_id(2)
    lane = _iota((tq, LANES), 1)
    qs = q_ref[0] * (HEAD_DIM ** -0.5)
    row_ge_col = _iota((tq, tq), 0) >= _iota((tq, tq), 1)
    contract_last = (((1,), (1,)), ((), ()))
    outs = []
    for hh in range(2):
        own = (lane < HEAD_DIM) == (hh == 0)
        qh = jnp.where(own, qs, jnp.zeros_like(qs))
        fq = _row_to_col(fq_ref[0, 0, 0, hh:hh + 1, :])

        def block(j, carry, masked):
            m, l, acc = carry
            kb = k_ref[0, pl.ds(pl.multiple_of(j * tq, tq), tq), :]
            vb = v_ref[0, pl.ds(pl.multiple_of(j * tq, tq), tq), :]
            fk = fk_ref[0, 0, j, hh:hh + 1, :]
            s = lax.dot_general(qh, kb, contract_last, preferred_element_type=F32)
            s = s + (fq - fk)
            if masked:
                s = jnp.where(row_ge_col, s, NEG_INF)
            m_new = jnp.maximum(m, jnp.max(s, axis=-1, keepdims=True))
            p = jnp.exp(s - m_new)
            alpha = jnp.exp(m - m_new)
            l_new = alpha * l + jnp.sum(p, axis=-1, keepdims=True)
            acc_new = alpha * acc + jnp.dot(p.astype(BF16), vb, preferred_element_type=F32)
            return m_new, l_new, acc_new

        init = (jnp.full((tq, 1), NEG_INF, F32), jnp.zeros((tq, 1), F32),
                jnp.zeros((tq, LANES), F32))
        carry = lax.fori_loop(0, i, functools.partial(block, masked=False), init)
        m, l, acc = block(i, carry, masked=True)
        outs.append(acc / l)
    o_ref[0] = jnp.where(lane < HEAD_DIM, outs[0], outs[1]).astype(o_ref.dtype)


def _fox_attention(qkv, f_cum, B, S):
    tq = min(FOX_TQ, S)
    nk = S // tq
    n_pairs = N_HEADS_ATT // 2
    qkv3 = qkv.reshape(B, S, 3 * ATT_WIDTH)
    f5 = f_cum.reshape(B, n_pairs, 2, nk, tq).transpose(0, 1, 3, 2, 4)
    out = pl.pallas_call(
        functools.partial(_fox_kernel, tq=tq),
        out_shape=jax.ShapeDtypeStruct((B, S, ATT_WIDTH), BF16),
        grid=(B, n_pairs, nk),
        in_specs=[pl.BlockSpec((1, tq, LANES), lambda b, p, i: (b, i, p)),
                  pl.BlockSpec((1, S, LANES), lambda b, p, i: (b, 0, n_pairs + p)),
                  pl.BlockSpec((1, S, LANES), lambda b, p, i: (b, 0, 2 * n_pairs + p)),
                  pl.BlockSpec((1, 1, 1, 2, tq), lambda b, p, i: (b, p, i, 0, 0)),
                  pl.BlockSpec((1, 1, nk, 2, tq), lambda b, p, i: (b, p, 0, 0, 0))],
        out_specs=pl.BlockSpec((1, tq, LANES), lambda b, p, i: (b, i, p)),
        compiler_params=_cparams("parallel", "parallel", "arbitrary"),
        name="fox_attention",
    )(qkv3, qkv3, qkv3, f5, f5)
    return out.reshape(B * S, ATT_WIDTH)


def _dil_kernel(slope_ref, q_ref, k_ref, v_ref, o_ref, m_scr, l_scr, acc_scr, *, S):
    QB = QUERY_BLOCK
    p_idx = pl.program_id(1)
    lane = _iota((QB, LANES), 1)
    head0 = lane < HEAD_DIM
    qi = _iota((QB, 2 * QB), 0)
    kj = _iota((QB, 2 * QB), 1)
    delta = QB + qi - kj
    contract_last = (((1,), (1,)), ((), ()))
    scale = HEAD_DIM ** -0.5
    for branch, (window, dil) in enumerate(DIL_PATTERNS):
        span = window // dil
        assert span <= QB and (S // dil) % QB == 0
        nblk = S // dil // QB
        band = (delta >= 0) & (delta <= span)
        dist = (delta * dil).astype(F32)
        bias = [slope_ref[2 * p_idx + hh] * dist for hh in range(2)]

        def step(idx, carry, dil=dil, nblk=nblk, band=band, bias=bias, first=(branch == 0)):
            r = idx // nblk
            n = idx % nblk
            q0 = r + n * (QB * dil)
            kp0 = r + jnp.maximum(n - 1, 0) * (QB * dil)
            rows_q = pl.ds(q0, QB, stride=dil) if dil > 1 else pl.ds(q0, QB)
            rows_p = pl.ds(kp0, QB, stride=dil) if dil > 1 else pl.ds(kp0, QB)
            q = q_ref[0, rows_q, :] * scale
            kcat = jnp.concatenate([k_ref[0, rows_p, :], k_ref[0, rows_q, :]], axis=0).astype(BF16)
            vcat = jnp.concatenate([v_ref[0, rows_p, :], v_ref[0, rows_q, :]], axis=0).astype(BF16)
            valid = band & (kj >= jnp.where(n > 0, 0, QB))
            if first:
                m_old = jnp.full((QB, LANES), NEG_INF, F32)
                l_old = jnp.zeros((QB, LANES), F32)
                acc_old = jnp.zeros((QB, LANES), F32)
            else:
                m_old = m_scr[rows_q, :]
                l_old = l_scr[rows_q, :]
                acc_old = acc_scr[rows_q, :]
            m_h, a_h, ps_h, pv_h = [], [], [], []
            for hh in range(2):
                own = head0 == (hh == 0)
                qh = jnp.where(own, q, 0.0).astype(BF16)
                s = lax.dot_general(qh, kcat, contract_last, preferred_element_type=F32) - bias[hh]
                s = jnp.where(valid, s, NEG_INF)
                mo = m_old[:, hh * HEAD_DIM:hh * HEAD_DIM + 1]
                mn = jnp.maximum(mo, jnp.max(s, axis=-1, keepdims=True))
                p = jnp.exp(s - mn)
                m_h.append(mn)
                a_h.append(jnp.exp(mo - mn))
                ps_h.append(jnp.sum(p, axis=-1, keepdims=True))
                pv_h.append(jnp.dot(p.astype(BF16), vcat, preferred_element_type=F32))
            alpha = jnp.where(head0, a_h[0], a_h[1])
            m_scr[rows_q, :] = jnp.where(head0, m_h[0], m_h[1])
            l_scr[rows_q, :] = alpha * l_old + jnp.where(head0, ps_h[0], ps_h[1])
            acc_scr[rows_q, :] = alpha * acc_old + jnp.where(head0, pv_h[0], pv_h[1])
            return carry

        lax.fori_loop(0, dil * nblk, step, 0)
    o_ref[0] = (acc_scr[...] / l_scr[...]).astype(o_ref.dtype)


def _dilated_attention(qkv, B, S):
    n_pairs = N_HEADS_ATT // 2
    slopes = jnp.asarray(2.0 ** (-8.0 * (np.arange(N_HEADS_ATT) + 1) / N_HEADS_ATT), dtype=F32)
    qkv3 = qkv.reshape(B, S, 3 * ATT_WIDTH)
    out = pl.pallas_call(
        functools.partial(_dil_kernel, S=S),
        out_shape=jax.ShapeDtypeStruct((B, S, ATT_WIDTH), BF16),
        grid=(B, n_pairs),
        in_specs=[pl.BlockSpec(memory_space=pltpu.SMEM),
                  pl.BlockSpec((1, S, LANES), lambda b, p: (b, 0, p)),
                  pl.BlockSpec((1, S, LANES), lambda b, p: (b, 0, n_pairs + p)),
                  pl.BlockSpec((1, S, LANES), lambda b, p: (b, 0, 2 * n_pairs + p))],
        out_specs=pl.BlockSpec((1, S, LANES), lambda b, p: (b, 0, p)),
        scratch_shapes=[pltpu.VMEM((S, LANES), F32), pltpu.VMEM((S, LANES), F32),
                        pltpu.VMEM((S, LANES), F32)],
        compiler_params=_cparams("parallel", "parallel"),
        name="dilated_attention",
    )(slopes, qkv3, qkv3, qkv3)
    return out.reshape(B * S, ATT_WIDTH)


def _attn_out_kernel(a0_ref, a1_ref, w_ref, x_ref, g_ref, b_ref, y_ref, ybf_ref):
    k0 = a0_ref.shape[1]
    mix = jnp.dot(a0_ref[...], w_ref[:k0, :], preferred_element_type=F32)
    mix = mix + jnp.dot(a1_ref[...], w_ref[k0:, :], preferred_element_type=F32)
    y = _layer_norm_rows(DEEPNORM_ALPHA * x_ref[...] + mix, g_ref[...], b_ref[...])
    y_ref[...] = y
    ybf_ref[...] = y.astype(BF16)


def _attn_out_ln(a0, a1, w, x, g, b):
    M, D = x.shape
    tm = min(LN_TM, M)
    row = lambda i: (i, 0)
    const = lambda i: (0, 0)
    return pl.pallas_call(
        _attn_out_kernel,
        out_shape=(jax.ShapeDtypeStruct((M, D), F32), jax.ShapeDtypeStruct((M, D), BF16)),
        grid=(M // tm,),
        in_specs=[pl.BlockSpec((tm, a0.shape[1]), row), pl.BlockSpec((tm, a1.shape[1]), row),
                  pl.BlockSpec(w.shape, const), pl.BlockSpec((tm, D), row),
                  pl.BlockSpec((1, D), const), pl.BlockSpec((1, D), const)],
        out_specs=(pl.BlockSpec((tm, D), row), pl.BlockSpec((tm, D), row)),
        compiler_params=_cparams("parallel"),
        name="attn_out_ln",
    )(a0, a1, w, x, g.reshape(1, D), b.reshape(1, D))


def _gdn_out_kernel(a_ref, w_ref, x_ref, g_ref, b_ref, r_ref, yt_ref, route_ref):
    tm = x_ref.shape[0]
    mix = jnp.dot(a_ref[...], w_ref[...], preferred_element_type=F32)
    y = _layer_norm_rows(DEEPNORM_ALPHA * x_ref[...] + mix, g_ref[...], b_ref[...])
    for s in range(SUBLANES):
        yt_ref[pl.ds(s, tm, stride=SUBLANES), :] = y[:, s * LANES:(s + 1) * LANES]
    logits = jnp.dot(y, r_ref[...], precision=HIGHEST, preferred_element_type=F32)
    lane = _iota((tm, LANES), 1)
    logits = jnp.where(lane < N_EXPERTS, logits, -jnp.inf)
    m1 = jnp.max(logits, axis=-1, keepdims=True)
    i1 = jnp.min(jnp.where(logits == m1, lane, LANES), axis=-1, keepdims=True)
    rest = jnp.where(lane == i1, -jnp.inf, logits)
    m2 = jnp.max(rest, axis=-1, keepdims=True)
    i2 = jnp.min(jnp.where(rest == m2, lane, LANES), axis=-1, keepdims=True)
    e2 = jnp.exp(m2 - m1)
    w1 = 1.0 / (1.0 + e2)
    w2 = e2 / (1.0 + e2)
    route = jnp.where(lane == 0, i1.astype(F32),
                      jnp.where(lane == 1, i2.astype(F32),
                                jnp.where(lane == 2, w1, jnp.where(lane == 3, w2, 0.0))))
    route_ref[...] = route


def _gdn_out_ln_route(a, w, x, g, b, router):
    M, D = x.shape
    tm = min(LN_TM, M)
    row = lambda i: (i, 0)
    const = lambda i: (0, 0)
    r_pad = jnp.zeros((D, LANES), F32).at[:, :N_EXPERTS].set(router.astype(F32))
    return pl.pallas_call(
        _gdn_out_kernel,
        out_shape=(jax.ShapeDtypeStruct((M * SUBLANES, LANES), F32),
                   jax.ShapeDtypeStruct((M, LANES), F32)),
        grid=(M // tm,),
        in_specs=[pl.BlockSpec((tm, a.shape[1]), row), pl.BlockSpec(w.shape, const),
                  pl.BlockSpec((tm, D), row), pl.BlockSpec((1, D), const),
                  pl.BlockSpec((1, D), const), pl.BlockSpec((D, LANES), const)],
        out_specs=(pl.BlockSpec((tm * SUBLANES, LANES), row), pl.BlockSpec((tm, LANES), row)),
        compiler_params=_cparams("parallel"),
        name="gdn_out_ln_route",
    )(a, w, x, g.reshape(1, D), b.reshape(1, D), r_pad)


def _ffn_dense_kernel(xbf_ref, wg_ref, wu_ref, wd_ref, x_ref, g_ref, b_ref, y_ref, ybf_ref, acc_scr):
    j = pl.program_id(1)

    @pl.when(j == 0)
    def _():
        acc_scr[...] = jnp.zeros_like(acc_scr)

    x = xbf_ref[...]
    hg = jnp.dot(x, wg_ref[...], preferred_element_type=F32)
    hu = jnp.dot(x, wu_ref[...], preferred_element_type=F32)
    h = (_silu(hg) * hu).astype(BF16)
    acc_scr[...] += jnp.dot(h, wd_ref[...], preferred_element_type=F32)

    @pl.when(j == pl.num_programs(1) - 1)
    def _():
        y = _layer_norm_rows(DEEPNORM_ALPHA * x_ref[...] + acc_scr[...], g_ref[...], b_ref[...])
        y_ref[...] = y
        ybf_ref[...] = y.astype(BF16)


def _ffn_dense_ln(xbf, x, wg, wu, wd, g, b):
    M, D = x.shape
    FF = wg.shape[1]
    tm = min(FFN_TM, M)
    tf = FFN_TF
    assert FF % tf == 0
    row = lambda i, j: (i, 0)
    const = lambda i, j: (0, 0)
    return pl.pallas_call(
        _ffn_dense_kernel,
        out_shape=(jax.ShapeDtypeStruct((M, D), F32), jax.ShapeDtypeStruct((M, D), BF16)),
        grid=(M // tm, FF // tf),
        in_specs=[pl.BlockSpec((tm, D), row),
                  pl.BlockSpec((D, tf), lambda i, j: (0, j)),
                  pl.BlockSpec((D, tf), lambda i, j: (0, j)),
                  pl.BlockSpec((tf, D), lambda i, j: (j, 0)),
                  pl.BlockSpec((tm, D), row),
                  pl.BlockSpec((1, D), const), pl.BlockSpec((1, D), const)],
        out_specs=(pl.BlockSpec((tm, D), row), pl.BlockSpec((tm, D), row)),
        scratch_shapes=[pltpu.VMEM((tm, D), F32)],
        compiler_params=_cparams("parallel", "arbitrary"),
        name="ffn_dense_ln",
    )(xbf, wg, wu, wd, x, g.reshape(1, D), b.reshape(1, D))


def _ffn_grouped_kernel(te_ref, na_ref, xt_ref, wg_ref, wu_ref, wd_ref, yt_ref, xbf_scr, acc_scr):
    i = pl.program_id(0)
    j = pl.program_id(1)
    tm = acc_scr.shape[0]
    active = i < na_ref[0]

    @pl.when(active & (j == 0))
    def _():
        acc_scr[...] = jnp.zeros_like(acc_scr)
        for s in range(SUBLANES):
            xbf_scr[:, s * LANES:(s + 1) * LANES] = xt_ref[pl.ds(s, tm, stride=SUBLANES), :].astype(BF16)

    @pl.when(active)
    def _():
        x = xbf_scr[...]
        hg = jnp.dot(x, wg_ref[0], preferred_element_type=F32)
        hu = jnp.dot(x, wu_ref[0], preferred_element_type=F32)
        h = (_silu(hg) * hu).astype(BF16)
        acc_scr[...] += jnp.dot(h, wd_ref[0], preferred_element_type=F32)

    @pl.when(active & (j == pl.num_programs(1) - 1))
    def _():
        for s in range(SUBLANES):
            yt_ref[pl.ds(s, tm, stride=SUBLANES), :] = acc_scr[:, s * LANES:(s + 1) * LANES]

    @pl.when(jnp.logical_not(active) & (j == 0))
    def _():
        yt_ref[...] = jnp.zeros_like(yt_ref)


def _ffn_grouped(xt, tile_expert, n_active, wg, wu, wd, tm):
    R = xt.shape[0] // SUBLANES
    E, D, FF = wg.shape
    tf = FFN_TF
    nf = FF // tf
    n_tiles = R // tm

    def row_map(i, j, te, na):
        return (jnp.minimum(i, na[0] - 1), 0)

    def ff_idx(i, j, na):
        return jnp.where(i < na[0], j, nf - 1)

    return pl.pallas_call(
        _ffn_grouped_kernel,
        out_shape=jax.ShapeDtypeStruct((R * SUBLANES, LANES), F32),
        grid_spec=pltpu.PrefetchScalarGridSpec(
            num_scalar_prefetch=2,
            grid=(n_tiles, nf),
            in_specs=[pl.BlockSpec((tm * SUBLANES, LANES), row_map),
                      pl.BlockSpec((1, D, tf), lambda i, j, te, na: (te[i], 0, ff_idx(i, j, na))),
                      pl.BlockSpec((1, D, tf), lambda i, j, te, na: (te[i], 0, ff_idx(i, j, na))),
                      pl.BlockSpec((1, tf, D), lambda i, j, te, na: (te[i], ff_idx(i, j, na), 0))],
            out_specs=pl.BlockSpec((tm * SUBLANES, LANES), lambda i, j, te, na: (i, 0)),
            scratch_shapes=[pltpu.VMEM((tm, D), BF16), pltpu.VMEM((tm, D), F32)]),
        compiler_params=_cparams("arbitrary", "arbitrary"),
        name="ffn_grouped",
    )(tile_expert, n_active, xt, wg, wu, wd)


def _dispatch_kernel(dest_ref, x_hbm, xs_in_hbm, xs_hbm, sem):
    del xs_in_hbm
    i = pl.program_id(0)
    ch = dest_ref.shape[2] // 2

    def body(t, carry):
        n = i * ch + t
        for k in range(2):
            pltpu.make_async_copy(x_hbm.at[n], xs_hbm.at[dest_ref[0, 0, 2 * t + k]], sem).start()
        return carry

    lax.fori_loop(0, ch, body, 0)
    for _ in range(2):
        pltpu.make_async_copy(x_hbm.at[pl.ds(0, ch)], xs_hbm.at[pl.ds(0, ch)], sem).wait()


def _dispatch_rows(xt3, dest, xs_init):
    N = xt3.shape[0]
    ch = min(DISPATCH_CHUNK, N)
    dest3 = dest.reshape(N // ch, 1, 2 * ch)
    return pl.pallas_call(
        _dispatch_kernel,
        out_shape=jax.ShapeDtypeStruct(xs_init.shape, F32),
        grid=(N // ch,),
        in_specs=[pl.BlockSpec((1, 1, 2 * ch), lambda i: (i, 0, 0), memory_space=pltpu.SMEM),
                  pl.BlockSpec(memory_space=pl.ANY),
                  pl.BlockSpec(memory_space=pl.ANY)],
        out_specs=pl.BlockSpec(memory_space=pl.ANY),
        scratch_shapes=[pltpu.SemaphoreType.DMA(())],
        input_output_aliases={2: 0},
        compiler_params=_cparams("arbitrary"),
        name="moe_dispatch",
    )(dest3, xt3, xs_init)


def _combine_kernel(dest_ref, ys_hbm, ya_hbm, yb_hbm, sem):
    i = pl.program_id(0)
    ch = dest_ref.shape[2] // 2

    def body(t, carry):
        n = i * ch + t
        pltpu.make_async_copy(ys_hbm.at[dest_ref[0, 0, 2 * t]], ya_hbm.at[n], sem).start()
        pltpu.make_async_copy(ys_hbm.at[dest_ref[0, 0, 2 * t + 1]], yb_hbm.at[n], sem).start()
        return carry

    lax.fori_loop(0, ch, body, 0)
    for dst in (ya_hbm, yb_hbm):
        pltpu.make_async_copy(ys_hbm.at[pl.ds(0, ch)], dst.at[pl.ds(0, ch)], sem).wait()


def _combine_rows(ys3, dest, N):
    ch = min(DISPATCH_CHUNK, N)
    dest3 = dest.reshape(N // ch, 1, 2 * ch)
    shape = jax.ShapeDtypeStruct((N, SUBLANES, LANES), F32)
    return pl.pallas_call(
        _combine_kernel,
        out_shape=(shape, shape),
        grid=(N // ch,),
        in_specs=[pl.BlockSpec((1, 1, 2 * ch), lambda i: (i, 0, 0), memory_space=pltpu.SMEM),
                  pl.BlockSpec(memory_space=pl.ANY)],
        out_specs=(pl.BlockSpec(memory_space=pl.ANY), pl.BlockSpec(memory_space=pl.ANY)),
        scratch_shapes=[pltpu.SemaphoreType.DMA(())],
        compiler_params=_cparams("arbitrary"),
        name="moe_combine",
    )(dest3, ys3)


def _moe_ln_kernel(xt_ref, ya_ref, yb_ref, route_ref, g_ref, b_ref, o_ref):
    tm = o_ref.shape[0]
    w1 = route_ref[:, 2:3]
    w2 = route_ref[:, 3:4]
    parts = []
    for s in range(SUBLANES):
        rows = pl.ds(s, tm, stride=SUBLANES)
        parts.append(DEEPNORM_ALPHA * xt_ref[rows, :] + (w1 * ya_ref[rows, :] + w2 * yb_ref[rows, :]))
    d_model = SUBLANES * LANES
    mu = sum(jnp.sum(z, axis=-1, keepdims=True) for z in parts) / d_model
    var = sum(jnp.sum((z - mu) * (z - mu), axis=-1, keepdims=True) for z in parts) / d_model
    rstd = lax.rsqrt(var + LN_EPS)
    for s in range(SUBLANES):
        cols = slice(s * LANES, (s + 1) * LANES)
        o_ref[:, cols] = (parts[s] - mu) * rstd * g_ref[:, cols] + b_ref[:, cols]


def _moe_combine_ln(xt, ya, yb, route, g, b):
    M = route.shape[0]
    D = SUBLANES * LANES
    tm = min(LN_TM, M)
    row = lambda i: (i, 0)
    const = lambda i: (0, 0)
    tile_spec = pl.BlockSpec((tm * SUBLANES, LANES), row)
    return pl.pallas_call(
        _moe_ln_kernel,
        out_shape=jax.ShapeDtypeStruct((M, D), F32),
        grid=(M // tm,),
        in_specs=[tile_spec, tile_spec, tile_spec, pl.BlockSpec((tm, LANES), row),
                  pl.BlockSpec((1, D), const), pl.BlockSpec((1, D), const)],
        out_specs=pl.BlockSpec((tm, D), row),
        compiler_params=_cparams("parallel"),
        name="moe_combine_ln",
    )(xt, ya, yb, route, g.reshape(1, D), b.reshape(1, D))


def _gdn_kernel(alog_ref, dt_ref, pq_ref, pk_ref, pv_ref, hq_ref, hk_ref, hv_ref,
                cq_ref, ck_ref, cv_ref, br_ref, ar_ref, ng_ref, gate_ref, o_ref,
                state_scr, stage_scr, *, blk):
    C = GDN_CHUNK
    nchunk = blk // C
    h = pl.program_id(1)
    sb = pl.program_id(2)

    @pl.when(sb == 0)
    def _():
        state_scr[...] = jnp.zeros_like(state_scr)

    have_prev = (sb > 0).astype(F32)

    def conv_silu(cur_ref, halo_ref, w_ref):
        stage_scr[0:SUBLANES, :] = halo_ref[0] * have_prev
        stage_scr[SUBLANES:SUBLANES + blk, :] = cur_ref[0]
        out = None
        for j in range(GDN_CONV):
            off = SUBLANES - (GDN_CONV - 1) + j
            term = w_ref[j:j + 1, :] * stage_scr[off:off + blk, :]
            out = term if out is None else out + term
        return _silu(out)

    def l2n(t):
        return t * lax.rsqrt(jnp.sum(t * t, axis=-1, keepdims=True) + RMS_EPS)

    q = l2n(conv_silu(pq_ref, hq_ref, cq_ref)) * (GDN_HEAD_DIM ** -0.5)
    k = l2n(conv_silu(pk_ref, hk_ref, ck_ref))
    v = conv_silu(pv_ref, hv_ref, cv_ref)

    beta_row = jax.nn.sigmoid(br_ref[0, 0])
    za = ar_ref[0, 0] + dt_ref[h]
    softplus = jnp.maximum(za, 0.0) + jnp.log1p(jnp.exp(-jnp.abs(za)))
    g_row = -jnp.exp(jnp.full((1, blk), alog_ref[h], F32)) * softplus
    ri = _iota((blk, blk), 0)
    ci = _iota((blk, blk), 1)
    same = (ri // C) == (ci // C)
    g8 = jnp.broadcast_to(g_row, (SUBLANES, blk))
    gam_row = jnp.dot(g8, (same & (ri <= ci)).astype(F32), precision=HIGHEST,
                      preferred_element_type=F32)[0:1, :]
    gl_row = jnp.dot(g8, same.astype(F32), precision=HIGHEST,
                     preferred_element_type=F32)[0:1, :]
    beta = _row_to_col(beta_row)
    gam = _row_to_col(gam_row)
    gl = _row_to_col(gl_row)
    eg = jnp.exp(gam)
    ekd = jnp.exp(gl - gam)

    incl = same & (ri >= ci)
    strict = same & (ri > ci)
    decay = jnp.where(incl, jnp.exp(jnp.where(incl, gam - gam_row, 0.0)), 0.0)
    contract_last = (((1,), (1,)), ((), ()))
    kb = k.astype(BF16)
    kk = lax.dot_general(kb, kb, contract_last, preferred_element_type=F32)
    a_neg = jnp.where(strict, -(beta * kk * decay), 0.0)
    x_acc = a_neg
    pw = a_neg
    for _ in range(int(np.log2(C)) - 1):
        pwb = pw.astype(BF16)
        pw = jnp.dot(pwb, pwb, preferred_element_type=F32)
        x_acc = x_acc + pw + jnp.dot(pw.astype(BF16), x_acc.astype(BF16), preferred_element_type=F32)
    rhs = jnp.concatenate([v * beta, k * (beta * eg)], axis=1)
    sol = rhs + jnp.dot(x_acc.astype(BF16), rhs.astype(BF16), preferred_element_type=F32)
    u = sol[:, :GDN_HEAD_DIM]
    w = sol[:, GDN_HEAD_DIM:]
    qk = lax.dot_general(q.astype(BF16), kb, contract_last, preferred_element_type=F32) * decay
    q_dec = (q * eg).astype(BF16)
    k_dec = (k * ekd).astype(BF16)
    w_b = w.astype(BF16)

    state = state_scr[...]
    v_new, o_inter = [], []
    for c in range(nchunk):
        rows = slice(c * C, (c + 1) * C)
        sbf = state.astype(BF16)
        vn = u[rows] - jnp.dot(w_b[rows], sbf, preferred_element_type=F32)
        o_inter.append(jnp.dot(q_dec[rows], sbf, preferred_element_type=F32))
        g_tot = jnp.exp(gl[c * C:c * C + 1, :])
        state = state * g_tot + lax.dot_general(k_dec[rows], vn.astype(BF16), (((0,), (0,)), ((), ())),
                                                preferred_element_type=F32)
        v_new.append(vn)
    state_scr[...] = state
    v_new = jnp.concatenate(v_new, axis=0)
    o = jnp.concatenate(o_inter, axis=0) + jnp.dot(qk.astype(BF16), v_new.astype(BF16),
                                                   preferred_element_type=F32)
    o = o * lax.rsqrt(jnp.mean(o * o, axis=-1, keepdims=True) + RMS_EPS) * ng_ref[...]
    o_ref[0] = (o * _silu(gate_ref[0])).astype(o_ref.dtype)


def _gated_deltanet(pre, ab_rows, gate, conv_w, a_log, dt_bias, norm_g, B, S):
    blk = min(GDN_BLK, S)
    H = N_HEADS_GDN
    pre3 = pre.reshape(B, S, 3 * GDN_WIDTH)
    gate3 = gate.reshape(B, S, GDN_WIDTH)
    halo_blocks = blk // SUBLANES

    def cur(col0):
        return pl.BlockSpec((1, blk, LANES), lambda b, h, s: (b, s, col0 + h))

    def halo(col0):
        return pl.BlockSpec((1, SUBLANES, LANES),
                            lambda b, h, s: (b, jnp.maximum(s * halo_blocks - 1, 0), col0 + h))

    def cw(col0):
        return pl.BlockSpec((GDN_CONV, LANES), lambda b, h, s: (0, col0 + h))

    smem = pl.BlockSpec(memory_space=pltpu.SMEM)
    out = pl.pallas_call(
        functools.partial(_gdn_kernel, blk=blk),
        out_shape=jax.ShapeDtypeStruct((B, S, GDN_WIDTH), BF16),
        grid=(B, H, S // blk),
        in_specs=[smem, smem,
                  cur(0), cur(H), cur(2 * H), halo(0), halo(H), halo(2 * H),
                  cw(0), cw(H), cw(2 * H),
                  pl.BlockSpec((1, 1, 1, blk), lambda b, h, s: (b, h, 0, s)),
                  pl.BlockSpec((1, 1, 1, blk), lambda b, h, s: (b, H + h, 0, s)),
                  pl.BlockSpec((1, LANES), lambda b, h, s: (0, 0)),
                  pl.BlockSpec((1, blk, LANES), lambda b, h, s: (b, s, h))],
        out_specs=pl.BlockSpec((1, blk, LANES), lambda b, h, s: (b, s, h)),
        scratch_shapes=[pltpu.VMEM((GDN_HEAD_DIM, GDN_HEAD_DIM), F32),
                        pltpu.VMEM((blk + SUBLANES, LANES), F32)],
        compiler_params=_cparams("parallel", "parallel", "arbitrary"),
        name="gated_deltanet",
    )(a_log.astype(F32), dt_bias.astype(F32), pre3, pre3, pre3, pre3, pre3, pre3,
      conv_w, conv_w, conv_w, ab_rows, ab_rows, norm_g.reshape(1, LANES).astype(F32), gate3)
    return out.reshape(B * S, GDN_WIDTH)


def _pad_cols(w, width):
    return jnp.zeros((w.shape[0], width), w.dtype).at[:, :w.shape[1]].set(w)


def _attention_layer(x, xbf, B, S, w_in, forget_bias, w_out, ln_g, ln_b,
                     w_gate, w_up, w_down, ln2_g, ln2_b):
    W = ATT_WIDTH
    H = N_HEADS_ATT
    w_fox = w_in[:, :3 * W].astype(BF16)
    w_f = _pad_cols(w_in[:, 3 * W:3 * W + H], LANES).astype(BF16)
    w_dil = w_in[:, 3 * W + H:].astype(BF16)
    qkv_fox = _proj(xbf, w_fox, BF16)
    f_logit = _proj(xbf, w_f, F32)
    qkv_dil = _proj(xbf, w_dil, F32)
    f_rows = f_logit[:, :H].reshape(B, S, H).transpose(0, 2, 1)
    f_cum = _forget_cumsum(f_rows, forget_bias)
    o_fox = _fox_attention(qkv_fox, f_cum, B, S)
    o_dil = _dilated_attention(qkv_dil, B, S)
    x1, x1bf = _attn_out_ln(o_fox, o_dil, w_out.astype(BF16), x, ln_g, ln_b)
    return _ffn_dense_ln(x1bf, x1, w_gate.astype(BF16), w_up.astype(BF16), w_down.astype(BF16),
                         ln2_g, ln2_b)


def _slot_indices(route, tm):
    N = route.shape[0]
    experts = route[:, 0:2].astype(jnp.int32).reshape(2 * N)
    onehot = (experts[:, None] == jnp.arange(N_EXPERTS, dtype=jnp.int32)[None, :]).astype(jnp.int32)
    csum = jnp.cumsum(onehot, axis=0)
    counts = csum[-1]
    padded = ((counts + tm - 1) // tm) * tm
    ends = jnp.cumsum(padded)
    starts = ends - padded
    dest = jnp.sum(onehot * (csum - 1 + starts[None, :]), axis=1).astype(jnp.int32)
    n_tiles = (2 * N) // tm + N_EXPERTS
    tile_start = jnp.arange(n_tiles, dtype=jnp.int32) * tm
    tile_expert = jnp.minimum(jnp.sum((tile_start[:, None] >= ends[None, :]).astype(jnp.int32), axis=1),
                              N_EXPERTS - 1).astype(jnp.int32)
    n_active = (ends[-1] // tm).astype(jnp.int32).reshape(1)
    return dest, tile_expert, n_active, n_tiles


def _deltanet_layer(x, xbf, B, S, w_in, conv_w, a_log, dt_bias, norm_g, w_out, ln_g, ln_b,
                    router, w_gate, w_up, w_down, ln2_g, ln2_b):
    N = B * S
    W = GDN_WIDTH
    H = N_HEADS_GDN
    w_qkv = w_in[:, :3 * W].astype(BF16)
    w_ab = _pad_cols(w_in[:, 3 * W:3 * W + 2 * H], LANES).astype(BF16)
    w_gt = w_in[:, 3 * W + 2 * H:].astype(BF16)
    pre = _proj(xbf, w_qkv, F32)
    ab = _proj(xbf, w_ab, F32)
    gate = _proj(xbf, w_gt, F32)
    ab_rows = ab[:, :2 * H].reshape(B, S, 2 * H).transpose(0, 2, 1).reshape(B, 2 * H, 1, S)
    o = _gated_deltanet(pre, ab_rows, gate, conv_w.astype(F32), a_log, dt_bias, norm_g, B, S)
    xt, route = _gdn_out_ln_route(o, w_out.astype(BF16), x, ln_g, ln_b, router)

    tm = min(FFN_TM, N)
    dest, tile_expert, n_active, n_tiles = _slot_indices(route, tm)
    xs_init = jnp.zeros((n_tiles * tm, SUBLANES, LANES), F32)
    xs = _dispatch_rows(xt.reshape(N, SUBLANES, LANES), dest, xs_init)
    ys = _ffn_grouped(xs.reshape(n_tiles * tm * SUBLANES, LANES), tile_expert, n_active,
                      w_gate.astype(BF16), w_up.astype(BF16), w_down.astype(BF16), tm)
    ya, yb = _combine_rows(ys.reshape(n_tiles * tm, SUBLANES, LANES), dest, N)
    return _moe_combine_ln(xt, ya.reshape(N * SUBLANES, LANES), yb.reshape(N * SUBLANES, LANES),
                           route, ln2_g, ln2_b)


def kernel(x, attn_w_in, fox_forget_bias, attn_w_out, ln_attn_g, ln_attn_b, ffn_w_gate, ffn_w_up,
           ffn_w_down, ln_ffn_g, ln_ffn_b, gdn_w_in, gdn_conv_w, gdn_a_log, gdn_dt_bias, gdn_norm_g,
           gdn_w_out, ln_gdn_g, ln_gdn_b, moe_router, moe_w_gate, moe_w_up, moe_w_down, ln_moe_g,
           ln_moe_b):
    B, S, D = x.shape
    x2 = x.reshape(B * S, D)
    x2bf = x2.astype(BF16)
    x2, x2bf = _attention_layer(x2, x2bf, B, S, attn_w_in[0], fox_forget_bias[0], attn_w_out[0],
                                ln_attn_g[0], ln_attn_b[0], ffn_w_gate[0], ffn_w_up[0], ffn_w_down[0],
                                ln_ffn_g[0], ln_ffn_b[0])
    y = _deltanet_layer(x2, x2bf, B, S, gdn_w_in[0], gdn_conv_w[0], gdn_a_log[0], gdn_dt_bias[0],
                        gdn_norm_g[0], gdn_w_out[0], ln_gdn_g[0], ln_gdn_b[0], moe_router[0],
                        moe_w_gate[0], moe_w_up[0], moe_w_down[0], ln_moe_g[0], ln_moe_b[0])
    return y.reshape(B, S, D)
```

```python
import functools

import numpy as np
import jax
import jax.numpy as jnp
from jax import lax
from jax.experimental import pallas as pl
from jax.experimental.pallas import tpu as pltpu

F32 = jnp.float32
BF16 = jnp.bfloat16
HIGHEST = lax.Precision.HIGHEST

LANES = 128
SUBLANES = 8
VMEM_LIMIT = 52 * 1024 * 1024

HEAD_DIM = 64
N_HEADS_ATT = 8
ATT_WIDTH = N_HEADS_ATT * HEAD_DIM
QUERY_BLOCK = 128
DIL_PATTERNS = ((128, 1), (512, 4), (2048, 16))
GDN_HEAD_DIM = 128
N_HEADS_GDN = 8
GDN_WIDTH = N_HEADS_GDN * GDN_HEAD_DIM
GDN_CONV = 4
GDN_CHUNK = 64
N_EXPERTS = 8
DEPTH = 2
DEEPNORM_ALPHA = (2.0 * DEPTH) ** 0.25
LN_EPS = 1e-5
RMS_EPS = 1e-6
NEG_INF = -1e30

PROJ_TM = 1024
PROJ_TN = 512
FOX_TQ = 512
LN_TM = 512
FFN_TM = 512
FFN_TF = 512
GDN_BLK = 256
DISPATCH_CHUNK = 512


def _cparams(*sem):
    return pltpu.CompilerParams(dimension_semantics=sem, vmem_limit_bytes=VMEM_LIMIT)


def _iota(shape, dim):
    return lax.broadcasted_iota(jnp.int32, shape, dim)


def _silu(x):
    return x * jax.nn.sigmoid(x)


def _row_to_col(row):
    n = row.shape[1]
    eye = _iota((LANES, LANES), 0) == _iota((LANES, LANES), 1)
    cols = []
    for c in range(n // LANES):
        seg = row[:, c * LANES:(c + 1) * LANES]
        cols.append(jnp.sum(jnp.where(eye, seg, 0.0), axis=1, keepdims=True))
    return cols[0] if len(cols) == 1 else jnp.concatenate(cols, axis=0)


def _layer_norm_rows(z, g, b):
    mu = jnp.mean(z, axis=-1, keepdims=True)
    zc = z - mu
    var = jnp.mean(zc * zc, axis=-1, keepdims=True)
    return zc * lax.rsqrt(var + LN_EPS) * g + b


def _proj_kernel(x_ref, w_ref, o_ref):
    o_ref[...] = jnp.dot(x_ref[...], w_ref[...], preferred_element_type=F32).astype(o_ref.dtype)


def _proj(x, w, out_dtype):
    M, K = x.shape
    C = w.shape[1]
    tm = min(PROJ_TM, M)
    tn = min(PROJ_TN, C)
    assert M % tm == 0 and C % tn == 0
    return pl.pallas_call(
        _proj_kernel,
        out_shape=jax.ShapeDtypeStruct((M, C), out_dtype),
        grid=(M // tm, C // tn),
        in_specs=[pl.BlockSpec((tm, K), lambda i, j: (i, 0)),
                  pl.BlockSpec((K, tn), lambda i, j: (0, j))],
        out_specs=pl.BlockSpec((tm, tn), lambda i, j: (i, j)),
        compiler_params=_cparams("parallel", "parallel"),
        name="proj",
    )(x, w)


def _forget_cumsum_kernel(f_ref, b_ref, o_ref):
    S = f_ref.shape[2]
    z = f_ref[0] + b_ref[...]
    lf = jnp.minimum(z, 0.0) - jnp.log1p(jnp.exp(-jnp.abs(z)))
    upper = (_iota((LANES, LANES), 0) <= _iota((LANES, LANES), 1)).astype(F32)
    carry = jnp.zeros((z.shape[0], 1), F32)
    for c in range(S // LANES):
        seg = jnp.dot(lf[:, c * LANES:(c + 1) * LANES], upper, precision=HIGHEST,
                      preferred_element_type=F32) + carry
        o_ref[0, :, c * LANES:(c + 1) * LANES] = seg
        carry = seg[:, LANES - 1:LANES]


def _forget_cumsum(f_rows, bias):
    B, H, S = f_rows.shape
    return pl.pallas_call(
        _forget_cumsum_kernel,
        out_shape=jax.ShapeDtypeStruct((B, H, S), F32),
        grid=(B,),
        in_specs=[pl.BlockSpec((1, H, S), lambda b: (b, 0, 0)),
                  pl.BlockSpec((H, 1), lambda b: (0, 0))],
        out_specs=pl.BlockSpec((1, H, S), lambda b: (b, 0, 0)),
        compiler_params=_cparams("parallel"),
        name="forget_cumsum",
    )(f_rows, bias.reshape(H, 1).astype(F32))


def _fox_kernel(q_ref, k_ref, v_ref, fq_ref, fk_ref, o_ref, *, tq):
    i = pl.program_id(2)
    lane = _iota((tq, LANES), 1)
    qs = q_ref[0] * (HEAD_DIM ** -0.5)
    row_ge_col = _iota((tq, tq), 0) >= _iota((tq, tq), 1)
    contract_last = (((1,), (1,)), ((), ()))
    outs = []
    for hh in range(2):
        own = (lane < HEAD_DIM) == (hh == 0)
        qh = jnp.where(own, qs, jnp.zeros_like(qs))
        fq = _row_to_col(fq_ref[0, 0, 0, hh:hh + 1, :])

        def block(j, carry, masked):
            m, l, acc = carry
            kb = k_ref[0, pl.ds(pl.multiple_of(j * tq, tq), tq), :]
            vb = v_ref[0, pl.ds(pl.multiple_of(j * tq, tq), tq), :]
            fk = fk_ref[0, 0, j, hh:hh + 1, :]
            s = lax.dot_general(qh, kb, contract_last, preferred_element_type=F32)
            s = s + (fq - fk)
            if masked:
                s = jnp.where(row_ge_col, s, NEG_INF)
            m_new = jnp.maximum(m, jnp.max(s, axis=-1, keepdims=True))
            p = jnp.exp(s - m_new)
            alpha = jnp.exp(m - m_new)
            l_new = alpha * l + jnp.sum(p, axis=-1, keepdims=True)
            acc_new = alpha * acc + jnp.dot(p.astype(BF16), vb, preferred_element_type=F32)
            return m_new, l_new, acc_new

        init = (jnp.full((tq, 1), NEG_INF, F32), jnp.zeros((tq, 1), F32),
                jnp.zeros((tq, LANES), F32))
        carry = lax.fori_loop(0, i, functools.partial(block, masked=False), init)
        m, l, acc = block(i, carry, masked=True)
        outs.append(acc / l)
    o_ref[0] = jnp.where(lane < HEAD_DIM, outs[0], outs[1]).astype(o_ref.dtype)


def _fox_attention(qkv, f_cum, B, S):
    tq = min(FOX_TQ, S)
    nk = S // tq
    n_pairs = N_HEADS_ATT // 2
    qkv3 = qkv.reshape(B, S, 3 * ATT_WIDTH)
    f5 = f_cum.reshape(B, n_pairs, 2, nk, tq).transpose(0, 1, 3, 2, 4)
    out = pl.pallas_call(
        functools.partial(_fox_kernel, tq=tq),
        out_shape=jax.ShapeDtypeStruct((B, S, ATT_WIDTH), BF16),
        grid=(B, n_pairs, nk),
        in_specs=[pl.BlockSpec((1, tq, LANES), lambda b, p, i: (b, i, p)),
                  pl.BlockSpec((1, S, LANES), lambda b, p, i: (b, 0, n_pairs + p)),
                  pl.BlockSpec((1, S, LANES), lambda b, p, i: (b, 0, 2 * n_pairs + p)),
                  pl.BlockSpec((1, 1, 1, 2, tq), lambda b, p, i: (b, p, i, 0, 0)),
                  pl.BlockSpec((1, 1, nk, 2, tq), lambda b, p, i: (b, p, 0, 0, 0))],
        out_specs=pl.BlockSpec((1, tq, LANES), lambda b, p, i: (b, i, p)),
        compiler_params=_cparams("parallel", "parallel", "arbitrary"),
        name="fox_attention",
    )(qkv3, qkv3, qkv3, f5, f5)
    return out.reshape(B * S, ATT_WIDTH)


def _dil_kernel(slope_ref, q_ref, k_ref, v_ref, o_ref, m_scr, l_scr, acc_scr, *, S):
    QB = QUERY_BLOCK
    p_idx = pl.program_id(1)
    lane = _iota((QB, LANES), 1)
    head0 = lane < HEAD_DIM
    qi = _iota((QB, 2 * QB), 0)
    kj = _iota((QB, 2 * QB), 1)
    delta = QB + qi - kj
    contract_last = (((1,), (1,)), ((), ()))
    scale = HEAD_DIM ** -0.5
    for branch, (window, dil) in enumerate(DIL_PATTERNS):
        span = window // dil
        assert span <= QB and (S // dil) % QB == 0
        nblk = S // dil // QB
        band = (delta >= 0) & (delta <= span)
        dist = (delta * dil).astype(F32)
        bias = [slope_ref[2 * p_idx + hh] * dist for hh in range(2)]

        def step(idx, carry, dil=dil, nblk=nblk, band=band, bias=bias, first=(branch == 0)):
            r = idx // nblk
            n = idx % nblk
            q0 = r + n * (QB * dil)
            kp0 = r + jnp.maximum(n - 1, 0) * (QB * dil)
            rows_q = pl.ds(q0, QB, stride=dil) if dil > 1 else pl.ds(q0, QB)
            rows_p = pl.ds(kp0, QB, stride=dil) if dil > 1 else pl.ds(kp0, QB)
            q = q_ref[0, rows_q, :] * scale
            kcat = jnp.concatenate([k_ref[0, rows_p, :], k_ref[0, rows_q, :]], axis=0).astype(BF16)
            vcat = jnp.concatenate([v_ref[0, rows_p, :], v_ref[0, rows_q, :]], axis=0).astype(BF16)
            valid = band & (kj >= jnp.where(n > 0, 0, QB))
            if first:
                m_old = jnp.full((QB, LANES), NEG_INF, F32)
                l_old = jnp.zeros((QB, LANES), F32)
                acc_old = jnp.zeros((QB, LANES), F32)
            else:
                m_old = m_scr[rows_q, :]
                l_old = l_scr[rows_q, :]
                acc_old = acc_scr[rows_q, :]
            m_h, a_h, ps_h, pv_h = [], [], [], []
            for hh in range(2):
                own = head0 == (hh == 0)
                qh = jnp.where(own, q, 0.0).astype(BF16)
                s = lax.dot_general(qh, kcat, contract_last, preferred_element_type=F32) - bias[hh]
                s = jnp.where(valid, s, NEG_INF)
                mo = m_old[:, hh * HEAD_DIM:hh * HEAD_DIM + 1]
                mn = jnp.maximum(mo, jnp.max(s, axis=-1, keepdims=True))
                p = jnp.exp(s - mn)
                m_h.append(mn)
                a_h.append(jnp.exp(mo - mn))
                ps_h.append(jnp.sum(p, axis=-1, keepdims=True))
                pv_h.append(jnp.dot(p.astype(BF16), vcat, preferred_element_type=F32))
            alpha = jnp.where(head0, a_h[0], a_h[1])
            m_scr[rows_q, :] = jnp.where(head0, m_h[0], m_h[1])
            l_scr[rows_q, :] = alpha * l_old + jnp.where(head0, ps_h[0], ps_h[1])
            acc_scr[rows_q, :] = alpha * acc_old + jnp.where(head0, pv_h[0], pv_h[1])
            return carry

        lax.fori_loop(0, dil * nblk, step, 0)
    o_ref[0] = (acc_scr[...] / l_scr[...]).astype(o_ref.dtype)


def _dilated_attention(qkv, B, S):
    n_pairs = N_HEADS_ATT // 2
    slopes = jnp.asarray(2.0 ** (-8.0 * (np.arange(N_HEADS_ATT) + 1) / N_HEADS_ATT), dtype=F32)
    qkv3 = qkv.reshape(B, S, 3 * ATT_WIDTH)
    out = pl.pallas_call(
        functools.partial(_dil_kernel, S=S),
        out_shape=jax.ShapeDtypeStruct((B, S, ATT_WIDTH), BF16),
        grid=(B, n_pairs),
        in_specs=[pl.BlockSpec(memory_space=pltpu.SMEM),
                  pl.BlockSpec((1, S, LANES), lambda b, p: (b, 0, p)),
                  pl.BlockSpec((1, S, LANES), lambda b, p: (b, 0, n_pairs + p)),
                  pl.BlockSpec((1, S, LANES), lambda b, p: (b, 0, 2 * n_pairs + p))],
        out_specs=pl.BlockSpec((1, S, LANES), lambda b, p: (b, 0, p)),
        scratch_shapes=[pltpu.VMEM((S, LANES), F32), pltpu.VMEM((S, LANES), F32),
                        pltpu.VMEM((S, LANES), F32)],
        compiler_params=_cparams("parallel", "parallel"),
        name="dilated_attention",
    )(slopes, qkv3, qkv3, qkv3)
    return out.reshape(B * S, ATT_WIDTH)


def _attn_out_kernel(a0_ref, a1_ref, w_ref, x_ref, g_ref, b_ref, y_ref, ybf_ref):
    k0 = a0_ref.shape[1]
    mix = jnp.dot(a0_ref[...], w_ref[:k0, :], preferred_element_type=F32)
    mix = mix + jnp.dot(a1_ref[...], w_ref[k0:, :], preferred_element_type=F32)
    y = _layer_norm_rows(DEEPNORM_ALPHA * x_ref[...] + mix, g_ref[...], b_ref[...])
    y_ref[...] = y
    ybf_ref[...] = y.astype(BF16)


def _attn_out_ln(a0, a1, w, x, g, b):
    M, D = x.shape
    tm = min(LN_TM, M)
    row = lambda i: (i, 0)
    const = lambda i: (0, 0)
    return pl.pallas_call(
        _attn_out_kernel,
        out_shape=(jax.ShapeDtypeStruct((M, D), F32), jax.ShapeDtypeStruct((M, D), BF16)),
        grid=(M // tm,),
        in_specs=[pl.BlockSpec((tm, a0.shape[1]), row), pl.BlockSpec((tm, a1.shape[1]), row),
                  pl.BlockSpec(w.shape, const), pl.BlockSpec((tm, D), row),
                  pl.BlockSpec((1, D), const), pl.BlockSpec((1, D), const)],
        out_specs=(pl.BlockSpec((tm, D), row), pl.BlockSpec((tm, D), row)),
        compiler_params=_cparams("parallel"),
        name="attn_out_ln",
    )(a0, a1, w, x, g.reshape(1, D), b.reshape(1, D))


def _gdn_out_kernel(a_ref, w_ref, x_ref, g_ref, b_ref, r_ref, yt_ref, route_ref):
    tm = x_ref.shape[0]
    mix = jnp.dot(a_ref[...], w_ref[...], preferred_element_type=F32)
    y = _layer_norm_rows(DEEPNORM_ALPHA * x_ref[...] + mix, g_ref[...], b_ref[...])
    for s in range(SUBLANES):
        yt_ref[pl.ds(s, tm, stride=SUBLANES), :] = y[:, s * LANES:(s + 1) * LANES]
    logits = jnp.dot(y, r_ref[...], precision=HIGHEST, preferred_element_type=F32)
    lane = _iota((tm, LANES), 1)
    logits = jnp.where(lane < N_EXPERTS, logits, -jnp.inf)
    m1 = jnp.max(logits, axis=-1, keepdims=True)
    i1 = jnp.min(jnp.where(logits == m1, lane, LANES), axis=-1, keepdims=True)
    rest = jnp.where(lane == i1, -jnp.inf, logits)
    m2 = jnp.max(rest, axis=-1, keepdims=True)
    i2 = jnp.min(jnp.where(rest == m2, lane, LANES), axis=-1, keepdims=True)
    e2 = jnp.exp(m2 - m1)
    w1 = 1.0 / (1.0 + e2)
    w2 = e2 / (1.0 + e2)
    route = jnp.where(lane == 0, i1.astype(F32),
                      jnp.where(lane == 1, i2.astype(F32),
                                jnp.where(lane == 2, w1, jnp.where(lane == 3, w2, 0.0))))
    route_ref[...] = route


def _gdn_out_ln_route(a, w, x, g, b, router):
    M, D = x.shape
    tm = min(LN_TM, M)
    row = lambda i: (i, 0)
    const = lambda i: (0, 0)
    r_pad = jnp.zeros((D, LANES), F32).at[:, :N_EXPERTS].set(router.astype(F32))
    return pl.pallas_call(
        _gdn_out_kernel,
        out_shape=(jax.ShapeDtypeStruct((M * SUBLANES, LANES), F32),
                   jax.ShapeDtypeStruct((M, LANES), F32)),
        grid=(M // tm,),
        in_specs=[pl.BlockSpec((tm, a.shape[1]), row), pl.BlockSpec(w.shape, const),
                  pl.BlockSpec((tm, D), row), pl.BlockSpec((1, D), const),
                  pl.BlockSpec((1, D), const), pl.BlockSpec((D, LANES), const)],
        out_specs=(pl.BlockSpec((tm * SUBLANES, LANES), row), pl.BlockSpec((tm, LANES), row)),
        compiler_params=_cparams("parallel"),
        name="gdn_out_ln_route",
    )(a, w, x, g.reshape(1, D), b.reshape(1, D), r_pad)


def _ffn_dense_kernel(xbf_ref, wg_ref, wu_ref, wd_ref, x_ref, g_ref, b_ref, y_ref, ybf_ref, acc_scr):
    j = pl.program_id(1)

    @pl.when(j == 0)
    def _():
        acc_scr[...] = jnp.zeros_like(acc_scr)

    x = xbf_ref[...]
    hg = jnp.dot(x, wg_ref[...], preferred_element_type=F32)
    hu = jnp.dot(x, wu_ref[...], preferred_element_type=F32)
    h = (_silu(hg) * hu).astype(BF16)
    acc_scr[...] += jnp.dot(h, wd_ref[...], preferred_element_type=F32)

    @pl.when(j == pl.num_programs(1) - 1)
    def _():
        y = _layer_norm_rows(DEEPNORM_ALPHA * x_ref[...] + acc_scr[...], g_ref[...], b_ref[...])
        y_ref[...] = y
        ybf_ref[...] = y.astype(BF16)


def _ffn_dense_ln(xbf, x, wg, wu, wd, g, b):
    M, D = x.shape
    FF = wg.shape[1]
    tm = min(FFN_TM, M)
    tf = FFN_TF
    assert FF % tf == 0
    row = lambda i, j: (i, 0)
    const = lambda i, j: (0, 0)
    return pl.pallas_call(
        _ffn_dense_kernel,
        out_shape=(jax.ShapeDtypeStruct((M, D), F32), jax.ShapeDtypeStruct((M, D), BF16)),
        grid=(M // tm, FF // tf),
        in_specs=[pl.BlockSpec((tm, D), row),
                  pl.BlockSpec((D, tf), lambda i, j: (0, j)),
                  pl.BlockSpec((D, tf), lambda i, j: (0, j)),
                  pl.BlockSpec((tf, D), lambda i, j: (j, 0)),
                  pl.BlockSpec((tm, D), row),
                  pl.BlockSpec((1, D), const), pl.BlockSpec((1, D), const)],
        out_specs=(pl.BlockSpec((tm, D), row), pl.BlockSpec((tm, D), row)),
        scratch_shapes=[pltpu.VMEM((tm, D), F32)],
        compiler_params=_cparams("parallel", "arbitrary"),
        name="ffn_dense_ln",
    )(xbf, wg, wu, wd, x, g.reshape(1, D), b.reshape(1, D))


def _ffn_grouped_kernel(te_ref, na_ref, xt_ref, wg_ref, wu_ref, wd_ref, yt_ref, xbf_scr, acc_scr):
    i = pl.program_id(0)
    j = pl.program_id(1)
    tm = acc_scr.shape[0]
    active = i < na_ref[0]

    @pl.when(active & (j == 0))
    def _():
        acc_scr[...] = jnp.zeros_like(acc_scr)
        for s in range(SUBLANES):
            xbf_scr[:, s * LANES:(s + 1) * LANES] = xt_ref[pl.ds(s, tm, stride=SUBLANES), :].astype(BF16)

    @pl.when(active)
    def _():
        x = xbf_scr[...]
        hg = jnp.dot(x, wg_ref[0], preferred_element_type=F32)
        hu = jnp.dot(x, wu_ref[0], preferred_element_type=F32)
        h = (_silu(hg) * hu).astype(BF16)
        acc_scr[...] += jnp.dot(h, wd_ref[0], preferred_element_type=F32)

    @pl.when(active & (j == pl.num_programs(1) - 1))
    def _():
        for s in range(SUBLANES):
            yt_ref[pl.ds(s, tm, stride=SUBLANES), :] = acc_scr[:, s * LANES:(s + 1) * LANES]

    @pl.when(jnp.logical_not(active) & (j == 0))
    def _():
        yt_ref[...] = jnp.zeros_like(yt_ref)


def _ffn_grouped(xt, tile_expert, n_active, wg, wu, wd, tm):
    R = xt.shape[0] // SUBLANES
    E, D, FF = wg.shape
    tf = FFN_TF
    nf = FF // tf
    n_tiles = R // tm

    def row_map(i, j, te, na):
        return (jnp.minimum(i, na[0] - 1), 0)

    def ff_idx(i, j, na):
        return jnp.where(i < na[0], j, nf - 1)

    return pl.pallas_call(
        _ffn_grouped_kernel,
        out_shape=jax.ShapeDtypeStruct((R * SUBLANES, LANES), F32),
        grid_spec=pltpu.PrefetchScalarGridSpec(
            num_scalar_prefetch=2,
            grid=(n_tiles, nf),
            in_specs=[pl.BlockSpec((tm * SUBLANES, LANES), row_map),
                      pl.BlockSpec((1, D, tf), lambda i, j, te, na: (te[i], 0, ff_idx(i, j, na))),
                      pl.BlockSpec((1, D, tf), lambda i, j, te, na: (te[i], 0, ff_idx(i, j, na))),
                      pl.BlockSpec((1, tf, D), lambda i, j, te, na: (te[i], ff_idx(i, j, na), 0))],
            out_specs=pl.BlockSpec((tm * SUBLANES, LANES), lambda i, j, te, na: (i, 0)),
            scratch_shapes=[pltpu.VMEM((tm, D), BF16), pltpu.VMEM((tm, D), F32)]),
        compiler_params=_cparams("arbitrary", "arbitrary"),
        name="ffn_grouped",
    )(tile_expert, n_active, xt, wg, wu, wd)


def _row_tile(ref, r):
    return ref.at[pl.ds(pl.multiple_of(r, SUBLANES), SUBLANES), :]


def _dispatch_kernel(dest_ref, xt_ref, xs_in_hbm, xs_hbm, sem):
    del xs_in_hbm
    ch = dest_ref.shape[2] // 2

    def body(t, carry):
        src = _row_tile(xt_ref, t * SUBLANES)
        for k in range(2):
            pltpu.make_async_copy(src, _row_tile(xs_hbm, dest_ref[0, 0, 2 * t + k]), sem).start()
        return carry

    lax.fori_loop(0, ch, body, 0)
    for _ in range(2):
        pltpu.make_async_copy(xt_ref, xs_hbm.at[pl.ds(0, ch * SUBLANES), :], sem).wait()


def _dispatch_rows(xt, dest8, xs_init):
    N = xt.shape[0] // SUBLANES
    ch = min(DISPATCH_CHUNK, N)
    dest3 = dest8.reshape(N // ch, 1, 2 * ch)
    return pl.pallas_call(
        _dispatch_kernel,
        out_shape=jax.ShapeDtypeStruct(xs_init.shape, F32),
        grid=(N // ch,),
        in_specs=[pl.BlockSpec((1, 1, 2 * ch), lambda i: (i, 0, 0), memory_space=pltpu.SMEM),
                  pl.BlockSpec((ch * SUBLANES, LANES), lambda i: (i, 0)),
                  pl.BlockSpec(memory_space=pl.ANY)],
        out_specs=pl.BlockSpec(memory_space=pl.ANY),
        scratch_shapes=[pltpu.SemaphoreType.DMA(())],
        input_output_aliases={2: 0},
        compiler_params=_cparams("arbitrary"),
        name="moe_dispatch",
    )(dest3, xt, xs_init)


def _moe_ln_kernel(dest_ref, xt_ref, ys_hbm, route_ref, g_ref, b_ref, o_ref, ya_scr, yb_scr, sem):
    tm = o_ref.shape[0]

    def body(t, carry):
        pltpu.make_async_copy(_row_tile(ys_hbm, dest_ref[0, 0, 2 * t]),
                              _row_tile(ya_scr, t * SUBLANES), sem).start()
        pltpu.make_async_copy(_row_tile(ys_hbm, dest_ref[0, 0, 2 * t + 1]),
                              _row_tile(yb_scr, t * SUBLANES), sem).start()
        return carry

    lax.fori_loop(0, tm, body, 0)
    for scr in (ya_scr, yb_scr):
        pltpu.make_async_copy(ys_hbm.at[pl.ds(0, tm * SUBLANES), :], scr, sem).wait()

    w1 = route_ref[:, 2:3]
    w2 = route_ref[:, 3:4]
    parts = []
    for s in range(SUBLANES):
        rows = pl.ds(s, tm, stride=SUBLANES)
        parts.append(DEEPNORM_ALPHA * xt_ref[rows, :] + (w1 * ya_scr[rows, :] + w2 * yb_scr[rows, :]))
    d_model = SUBLANES * LANES
    mu = sum(jnp.sum(z, axis=-1, keepdims=True) for z in parts) / d_model
    var = sum(jnp.sum((z - mu) * (z - mu), axis=-1, keepdims=True) for z in parts) / d_model
    rstd = lax.rsqrt(var + LN_EPS)
    for s in range(SUBLANES):
        cols = slice(s * LANES, (s + 1) * LANES)
        o_ref[:, cols] = (parts[s] - mu) * rstd * g_ref[:, cols] + b_ref[:, cols]


def _moe_combine_ln(xt, ys, dest8, route, g, b):
    M = route.shape[0]
    D = SUBLANES * LANES
    tm = min(LN_TM, M)
    row = lambda i: (i, 0)
    const = lambda i: (0, 0)
    dest3 = dest8.reshape(M // tm, 1, 2 * tm)
    return pl.pallas_call(
        _moe_ln_kernel,
        out_shape=jax.ShapeDtypeStruct((M, D), F32),
        grid=(M // tm,),
        in_specs=[pl.BlockSpec((1, 1, 2 * tm), lambda i: (i, 0, 0), memory_space=pltpu.SMEM),
                  pl.BlockSpec((tm * SUBLANES, LANES), row),
                  pl.BlockSpec(memory_space=pl.ANY),
                  pl.BlockSpec((tm, LANES), row),
                  pl.BlockSpec((1, D), const), pl.BlockSpec((1, D), const)],
        out_specs=pl.BlockSpec((tm, D), row),
        scratch_shapes=[pltpu.VMEM((tm * SUBLANES, LANES), F32), pltpu.VMEM((tm * SUBLANES, LANES), F32),
                        pltpu.SemaphoreType.DMA(())],
        compiler_params=_cparams("arbitrary"),
        name="moe_combine_ln",
    )(dest3, xt, ys, route, g.reshape(1, D), b.reshape(1, D))


def _gdn_kernel(alog_ref, dt_ref, pq_ref, pk_ref, pv_ref, hq_ref, hk_ref, hv_ref,
                cq_ref, ck_ref, cv_ref, br_ref, ar_ref, ng_ref, gate_ref, o_ref,
                state_scr, stage_scr, *, blk):
    C = GDN_CHUNK
    nchunk = blk // C
    h = pl.program_id(1)
    sb = pl.program_id(2)

    @pl.when(sb == 0)
    def _():
        state_scr[...] = jnp.zeros_like(state_scr)

    have_prev = (sb > 0).astype(F32)

    def conv_silu(cur_ref, halo_ref, w_ref):
        stage_scr[0:SUBLANES, :] = halo_ref[0] * have_prev
        stage_scr[SUBLANES:SUBLANES + blk, :] = cur_ref[0]
        out = None
        for j in range(GDN_CONV):
            off = SUBLANES - (GDN_CONV - 1) + j
            term = w_ref[j:j + 1, :] * stage_scr[off:off + blk, :]
            out = term if out is None else out + term
        return _silu(out)

    def l2n(t):
        return t * lax.rsqrt(jnp.sum(t * t, axis=-1, keepdims=True) + RMS_EPS)

    q = l2n(conv_silu(pq_ref, hq_ref, cq_ref)) * (GDN_HEAD_DIM ** -0.5)
    k = l2n(conv_silu(pk_ref, hk_ref, ck_ref))
    v = conv_silu(pv_ref, hv_ref, cv_ref)

    beta_row = jax.nn.sigmoid(br_ref[0, 0])
    za = ar_ref[0, 0] + dt_ref[h]
    softplus = jnp.maximum(za, 0.0) + jnp.log1p(jnp.exp(-jnp.abs(za)))
    g_row = -jnp.exp(jnp.full((1, blk), alog_ref[h], F32)) * softplus
    ri = _iota((blk, blk), 0)
    ci = _iota((blk, blk), 1)
    same = (ri // C) == (ci // C)
    g8 = jnp.broadcast_to(g_row, (SUBLANES, blk))
    gam_row = jnp.dot(g8, (same & (ri <= ci)).astype(F32), precision=HIGHEST,
                      preferred_element_type=F32)[0:1, :]
    gl_row = jnp.dot(g8, same.astype(F32), precision=HIGHEST,
                     preferred_element_type=F32)[0:1, :]
    beta = _row_to_col(beta_row)
    gam = _row_to_col(gam_row)
    gl = _row_to_col(gl_row)
    eg = jnp.exp(gam)
    ekd = jnp.exp(gl - gam)

    incl = same & (ri >= ci)
    strict = same & (ri > ci)
    decay = jnp.where(incl, jnp.exp(jnp.where(incl, gam - gam_row, 0.0)), 0.0)
    contract_last = (((1,), (1,)), ((), ()))
    kb = k.astype(BF16)
    kk = lax.dot_general(kb, kb, contract_last, preferred_element_type=F32)
    a_neg = jnp.where(strict, -(beta * kk * decay), 0.0)
    x_acc = a_neg
    pw = a_neg
    for _ in range(int(np.log2(C)) - 1):
        pwb = pw.astype(BF16)
        pw = jnp.dot(pwb, pwb, preferred_element_type=F32)
        x_acc = x_acc + pw + jnp.dot(pw.astype(BF16), x_acc.astype(BF16), preferred_element_type=F32)
    rhs = jnp.concatenate([v * beta, k * (beta * eg)], axis=1)
    sol = rhs + jnp.dot(x_acc.astype(BF16), rhs.astype(BF16), preferred_element_type=F32)
    u = sol[:, :GDN_HEAD_DIM]
    w = sol[:, GDN_HEAD_DIM:]
    qk = lax.dot_general(q.astype(BF16), kb, contract_last, preferred_element_type=F32) * decay
    q_dec = (q * eg).astype(BF16)
    k_dec = (k * ekd).astype(BF16)
    w_b = w.astype(BF16)

    state = state_scr[...]
    v_new, o_inter = [], []
    for c in range(nchunk):
        rows = slice(c * C, (c + 1) * C)
        sbf = state.astype(BF16)
        vn = u[rows] - jnp.dot(w_b[rows], sbf, preferred_element_type=F32)
        o_inter.append(jnp.dot(q_dec[rows], sbf, preferred_element_type=F32))
        g_tot = jnp.exp(gl[c * C:c * C + 1, :])
        state = state * g_tot + lax.dot_general(k_dec[rows], vn.astype(BF16), (((0,), (0,)), ((), ())),
                                                preferred_element_type=F32)
        v_new.append(vn)
    state_scr[...] = state
    v_new = jnp.concatenate(v_new, axis=0)
    o = jnp.concatenate(o_inter, axis=0) + jnp.dot(qk.astype(BF16), v_new.astype(BF16),
                                                   preferred_element_type=F32)
    o = o * lax.rsqrt(jnp.mean(o * o, axis=-1, keepdims=True) + RMS_EPS) * ng_ref[...]
    o_ref[0] = (o * _silu(gate_ref[0])).astype(o_ref.dtype)


def _gated_deltanet(pre, ab_rows, gate, conv_w, a_log, dt_bias, norm_g, B, S):
    blk = min(GDN_BLK, S)
    H = N_HEADS_GDN
    pre3 = pre.reshape(B, S, 3 * GDN_WIDTH)
    gate3 = gate.reshape(B, S, GDN_WIDTH)
    halo_blocks = blk // SUBLANES

    def cur(col0):
        return pl.BlockSpec((1, blk, LANES), lambda b, h, s: (b, s, col0 + h))

    def halo(col0):
        return pl.BlockSpec((1, SUBLANES, LANES),
                            lambda b, h, s: (b, jnp.maximum(s * halo_blocks - 1, 0), col0 + h))

    def cw(col0):
        return pl.BlockSpec((GDN_CONV, LANES), lambda b, h, s: (0, col0 + h))

    smem = pl.BlockSpec(memory_space=pltpu.SMEM)
    out = pl.pallas_call(
        functools.partial(_gdn_kernel, blk=blk),
        out_shape=jax.ShapeDtypeStruct((B, S, GDN_WIDTH), BF16),
        grid=(B, H, S // blk),
        in_specs=[smem, smem,
                  cur(0), cur(H), cur(2 * H), halo(0), halo(H), halo(2 * H),
                  cw(0), cw(H), cw(2 * H),
                  pl.BlockSpec((1, 1, 1, blk), lambda b, h, s: (b, h, 0, s)),
                  pl.BlockSpec((1, 1, 1, blk), lambda b, h, s: (b, H + h, 0, s)),
                  pl.BlockSpec((1, LANES), lambda b, h, s: (0, 0)),
                  pl.BlockSpec((1, blk, LANES), lambda b, h, s: (b, s, h))],
        out_specs=pl.BlockSpec((1, blk, LANES), lambda b, h, s: (b, s, h)),
        scratch_shapes=[pltpu.VMEM((GDN_HEAD_DIM, GDN_HEAD_DIM), F32),
                        pltpu.VMEM((blk + SUBLANES, LANES), F32)],
        compiler_params=_cparams("parallel", "parallel", "arbitrary"),
        name="gated_deltanet",
    )(a_log.astype(F32), dt_bias.astype(F32), pre3, pre3, pre3, pre3, pre3, pre3,
      conv_w, conv_w, conv_w, ab_rows, ab_rows, norm_g.reshape(1, LANES).astype(F32), gate3)
    return out.reshape(B * S, GDN_WIDTH)


def _pad_cols(w, width):
    return jnp.zeros((w.shape[0], width), w.dtype).at[:, :w.shape[1]].set(w)


def _attention_layer(x, xbf, B, S, w_in, forget_bias, w_out, ln_g, ln_b,
                     w_gate, w_up, w_down, ln2_g, ln2_b):
    W = ATT_WIDTH
    H = N_HEADS_ATT
    w_fox = w_in[:, :3 * W].astype(BF16)
    w_f = _pad_cols(w_in[:, 3 * W:3 * W + H], LANES).astype(BF16)
    w_dil = w_in[:, 3 * W + H:].astype(BF16)
    qkv_fox = _proj(xbf, w_fox, BF16)
    f_logit = _proj(xbf, w_f, F32)
    qkv_dil = _proj(xbf, w_dil, F32)
    f_rows = f_logit[:, :H].reshape(B, S, H).transpose(0, 2, 1)
    f_cum = _forget_cumsum(f_rows, forget_bias)
    o_fox = _fox_attention(qkv_fox, f_cum, B, S)
    o_dil = _dilated_attention(qkv_dil, B, S)
    x1, x1bf = _attn_out_ln(o_fox, o_dil, w_out.astype(BF16), x, ln_g, ln_b)
    return _ffn_dense_ln(x1bf, x1, w_gate.astype(BF16), w_up.astype(BF16), w_down.astype(BF16),
                         ln2_g, ln2_b)


def _slot_indices(route, tm):
    N = route.shape[0]
    experts = route[:, 0:2].astype(jnp.int32).reshape(2 * N)
    onehot = (experts[:, None] == jnp.arange(N_EXPERTS, dtype=jnp.int32)[None, :]).astype(jnp.int32)
    csum = jnp.cumsum(onehot, axis=0)
    counts = csum[-1]
    padded = ((counts + tm - 1) // tm) * tm
    ends = jnp.cumsum(padded)
    starts = ends - padded
    dest = jnp.sum(onehot * (csum - 1 + starts[None, :]), axis=1).astype(jnp.int32)
    n_tiles = (2 * N) // tm + N_EXPERTS
    tile_start = jnp.arange(n_tiles, dtype=jnp.int32) * tm
    tile_expert = jnp.minimum(jnp.sum((tile_start[:, None] >= ends[None, :]).astype(jnp.int32), axis=1),
                              N_EXPERTS - 1).astype(jnp.int32)
    n_active = (ends[-1] // tm).astype(jnp.int32).reshape(1)
    return dest, tile_expert, n_active, n_tiles


def _deltanet_layer(x, xbf, B, S, w_in, conv_w, a_log, dt_bias, norm_g, w_out, ln_g, ln_b,
                    router, w_gate, w_up, w_down, ln2_g, ln2_b):
    N = B * S
    W = GDN_WIDTH
    H = N_HEADS_GDN
    w_qkv = w_in[:, :3 * W].astype(BF16)
    w_ab = _pad_cols(w_in[:, 3 * W:3 * W + 2 * H], LANES).astype(BF16)
    w_gt = w_in[:, 3 * W + 2 * H:].astype(BF16)
    pre = _proj(xbf, w_qkv, F32)
    ab = _proj(xbf, w_ab, F32)
    gate = _proj(xbf, w_gt, F32)
    ab_rows = ab[:, :2 * H].reshape(B, S, 2 * H).transpose(0, 2, 1).reshape(B, 2 * H, 1, S)
    o = _gated_deltanet(pre, ab_rows, gate, conv_w.astype(F32), a_log, dt_bias, norm_g, B, S)
    xt, route = _gdn_out_ln_route(o, w_out.astype(BF16), x, ln_g, ln_b, router)

    tm = min(FFN_TM, N)
    dest, tile_expert, n_active, n_tiles = _slot_indices(route, tm)
    dest8 = dest * SUBLANES
    xs_init = jnp.zeros((n_tiles * tm * SUBLANES, LANES), F32)
    xs = _dispatch_rows(xt, dest8, xs_init)
    ys = _ffn_grouped(xs, tile_expert, n_active,
                      w_gate.astype(BF16), w_up.astype(BF16), w_down.astype(BF16), tm)
    return _moe_combine_ln(xt, ys, dest8, route, ln2_g, ln2_b)


def kernel(x, attn_w_in, fox_forget_bias, attn_w_out, ln_attn_g, ln_attn_b, ffn_w_gate, ffn_w_up,
           ffn_w_down, ln_ffn_g, ln_ffn_b, gdn_w_in, gdn_conv_w, gdn_a_log, gdn_dt_bias, gdn_norm_g,
           gdn_w_out, ln_gdn_g, ln_gdn_b, moe_router, moe_w_gate, moe_w_up, moe_w_down, ln_moe_g,
           ln_moe_b):
    B, S, D = x.shape
    x2 = x.reshape(B * S, D)
    x2bf = x2.astype(BF16)
    x2, x2bf = _attention_layer(x2, x2bf, B, S, attn_w_in[0], fox_forget_bias[0], attn_w_out[0],
                                ln_attn_g[0], ln_attn_b[0], ffn_w_gate[0], ffn_w_up[0], ffn_w_down[0],
                                ln_ffn_g[0], ln_ffn_b[0])
    y = _deltanet_layer(x2, x2bf, B, S, gdn_w_in[0], gdn_conv_w[0], gdn_a_log[0], gdn_dt_bias[0],
                        gdn_norm_g[0], gdn_w_out[0], ln_gdn_g[0], ln_gdn_b[0], moe_router[0],
                        moe_w_gate[0], moe_w_up[0], moe_w_down[0], ln_moe_g[0], ln_moe_b[0])
    return y.reshape(B, S, D)
```

```python
import functools

import numpy as np
import jax
import jax.numpy as jnp
from jax import lax
from jax.experimental import pallas as pl
from jax.experimental.pallas import tpu as pltpu

F32 = jnp.float32
BF16 = jnp.bfloat16
HIGHEST = lax.Precision.HIGHEST

LANES = 128
SUBLANES = 8
VMEM_LIMIT = 52 * 1024 * 1024

HEAD_DIM = 64
N_HEADS_ATT = 8
ATT_WIDTH = N_HEADS_ATT * HEAD_DIM
QUERY_BLOCK = 128
DIL_PATTERNS = ((128, 1), (512, 4), (2048, 16))
GDN_HEAD_DIM = 128
N_HEADS_GDN = 8
GDN_WIDTH = N_HEADS_GDN * GDN_HEAD_DIM
GDN_CONV = 4
GDN_CHUNK = 64
N_EXPERTS = 8
DEPTH = 2
DEEPNORM_ALPHA = (2.0 * DEPTH) ** 0.25
LN_EPS = 1e-5
RMS_EPS = 1e-6
NEG_INF = -1e30
LOG2E = 1.4426950408889634

PROJ_TM = 1024
PROJ_TN = 512
FOX_TQ = 1024
DIL_GROUP = 4
LN_TM = 512
FFN_TM = 1024
FFN_TF = 896
GDN_BLK = 128
GDN_HEADS_PER_STEP = 8
DISPATCH_CHUNK = 512


def _cparams(*sem):
    return pltpu.CompilerParams(dimension_semantics=sem, vmem_limit_bytes=VMEM_LIMIT)


def _iota(shape, dim):
    return lax.broadcasted_iota(jnp.int32, shape, dim)


def _silu(x):
    return x * jax.nn.sigmoid(x)


def _row_to_col(row):
    n = row.shape[1]
    eye = _iota((LANES, LANES), 0) == _iota((LANES, LANES), 1)
    cols = []
    for c in range(n // LANES):
        seg = row[:, c * LANES:(c + 1) * LANES]
        cols.append(jnp.sum(jnp.where(eye, seg, 0.0), axis=1, keepdims=True))
    return cols[0] if len(cols) == 1 else jnp.concatenate(cols, axis=0)


def _layer_norm_rows(z, g, b):
    mu = jnp.mean(z, axis=-1, keepdims=True)
    zc = z - mu
    var = jnp.mean(zc * zc, axis=-1, keepdims=True)
    return zc * lax.rsqrt(var + LN_EPS) * g + b


def _proj_kernel(x_ref, w_ref, o_ref):
    o_ref[...] = jnp.dot(x_ref[...], w_ref[...], preferred_element_type=F32).astype(o_ref.dtype)


def _proj(x, w, out_dtype):
    M, K = x.shape
    C = w.shape[1]
    tm = min(PROJ_TM, M)
    tn = min(PROJ_TN, C)
    assert M % tm == 0 and C % tn == 0
    return pl.pallas_call(
        _proj_kernel,
        out_shape=jax.ShapeDtypeStruct((M, C), out_dtype),
        grid=(M // tm, C // tn),
        in_specs=[pl.BlockSpec((tm, K), lambda i, j: (i, 0)),
                  pl.BlockSpec((K, tn), lambda i, j: (0, j))],
        out_specs=pl.BlockSpec((tm, tn), lambda i, j: (i, j)),
        compiler_params=_cparams("parallel", "parallel"),
        name="proj",
    )(x, w)


def _forget_cumsum_kernel(f_ref, b_ref, o_ref):
    S = f_ref.shape[2]
    z = f_ref[0] + b_ref[...]
    lf = (jnp.minimum(z, 0.0) - jnp.log1p(jnp.exp(-jnp.abs(z)))) * LOG2E
    upper = (_iota((LANES, LANES), 0) <= _iota((LANES, LANES), 1)).astype(F32)
    carry = jnp.zeros((z.shape[0], 1), F32)
    for c in range(S // LANES):
        seg = jnp.dot(lf[:, c * LANES:(c + 1) * LANES], upper, precision=HIGHEST,
                      preferred_element_type=F32) + carry
        o_ref[0, :, c * LANES:(c + 1) * LANES] = seg
        carry = seg[:, LANES - 1:LANES]


def _forget_cumsum(f_rows, bias):
    B, H, S = f_rows.shape
    return pl.pallas_call(
        _forget_cumsum_kernel,
        out_shape=jax.ShapeDtypeStruct((B, H, S), F32),
        grid=(B,),
        in_specs=[pl.BlockSpec((1, H, S), lambda b: (b, 0, 0)),
                  pl.BlockSpec((H, 1), lambda b: (0, 0))],
        out_specs=pl.BlockSpec((1, H, S), lambda b: (b, 0, 0)),
        compiler_params=_cparams("parallel"),
        name="forget_cumsum",
    )(f_rows, bias.reshape(H, 1).astype(F32))


def _fox_kernel(q_ref, k_ref, v_ref, fq_ref, fk_ref, o_ref, *, tq):
    i = pl.program_id(2)
    lane = _iota((tq, LANES), 1)
    q = q_ref[0]
    row_ge_col = _iota((tq, tq), 0) >= _iota((tq, tq), 1)
    contract_last = (((1,), (1,)), ((), ()))
    outs = []
    for hh in range(2):
        qh = jnp.where((lane < HEAD_DIM) == (hh == 0), q, jnp.zeros_like(q))
        fq = jnp.broadcast_to(_row_to_col(fq_ref[0, 0, 0, hh:hh + 1, :]), (tq, LANES))

        def block(j, carry, masked, hh=hh, qh=qh, fq=fq):
            m, l, acc = carry
            kb = k_ref[0, pl.ds(pl.multiple_of(j * tq, tq), tq), :]
            vb = v_ref[0, pl.ds(pl.multiple_of(j * tq, tq), tq), :]
            s = lax.dot_general(qh, kb, contract_last, preferred_element_type=F32) - fk_ref[0, 0, j, hh:hh + 1, :]
            if masked:
                s = jnp.where(row_ge_col, s, NEG_INF)
            m_new = jnp.maximum(m, jnp.max(s, axis=-1, keepdims=True) + fq)
            p = jnp.exp2(s - jnp.concatenate([m_new - fq] * (tq // LANES), axis=1))
            alpha = jnp.exp2(m - m_new)
            l_new = alpha * l + jnp.sum(p, axis=-1, keepdims=True)
            acc_new = alpha * acc + jnp.dot(p.astype(BF16), vb, preferred_element_type=F32)
            return m_new, l_new, acc_new

        init = (jnp.full((tq, LANES), NEG_INF, F32), jnp.zeros((tq, LANES), F32), jnp.zeros((tq, LANES), F32))
        carry = lax.fori_loop(0, i, functools.partial(block, masked=False), init)
        m, l, acc = block(i, carry, masked=True)
        outs.append(acc / l)
    o_ref[0] = jnp.where(lane < HEAD_DIM, outs[0], outs[1]).astype(o_ref.dtype)


def _fox_attention(qkv, f_cum, B, S):
    tq = min(FOX_TQ, S)
    nk = S // tq
    n_pairs = N_HEADS_ATT // 2
    qkv3 = qkv.reshape(B, S, 3 * ATT_WIDTH)
    f5 = f_cum.reshape(B, n_pairs, 2, nk, tq).transpose(0, 1, 3, 2, 4)
    out = pl.pallas_call(
        functools.partial(_fox_kernel, tq=tq),
        out_shape=jax.ShapeDtypeStruct((B, S, ATT_WIDTH), BF16),
        grid=(B, n_pairs, nk),
        in_specs=[pl.BlockSpec((1, tq, LANES), lambda b, p, i: (b, i, p)),
                  pl.BlockSpec((1, S, LANES), lambda b, p, i: (b, 0, n_pairs + p)),
                  pl.BlockSpec((1, S, LANES), lambda b, p, i: (b, 0, 2 * n_pairs + p)),
                  pl.BlockSpec((1, 1, 1, 2, tq), lambda b, p, i: (b, p, i, 0, 0)),
                  pl.BlockSpec((1, 1, nk, 2, tq), lambda b, p, i: (b, p, 0, 0, 0))],
        out_specs=pl.BlockSpec((1, tq, LANES), lambda b, p, i: (b, i, p)),
        compiler_params=_cparams("parallel", "parallel", "arbitrary"),
        name="fox_attention",
    )(qkv3, qkv3, qkv3, f5, f5)
    return out.reshape(B * S, ATT_WIDTH)


def _dil_kernel(slope_ref, q_ref, k_ref, v_ref, o_ref, m0_scr, m1_scr, l_scr, acc_scr, *, S):
    QB = QUERY_BLOCK
    G = DIL_GROUP
    p_idx = pl.program_id(1)
    lane = _iota((QB, LANES), 1)
    head0 = lane < HEAD_DIM
    contract_last = (((1,), (1,)), ((), ()))
    qscale = HEAD_DIM ** -0.5 * LOG2E
    slopes = [slope_ref[2 * p_idx + hh] * LOG2E for hh in range(2)]
    delta_cur = _iota((QB, QB), 0) - _iota((QB, QB), 1)
    delta_two = QB + _iota((QB, 2 * QB), 0) - _iota((QB, 2 * QB), 1)

    def rows(start, dil):
        return pl.ds(start, QB, stride=dil) if dil > 1 else pl.ds(start, QB)

    def update(q0s, kp0s, dil, biases, first):
        loaded = []
        for g, q0 in enumerate(q0s):
            rq = rows(q0, dil)
            q = q_ref[0, rq, :] * qscale
            kc = k_ref[0, rq, :]
            vc = v_ref[0, rq, :]
            if kp0s is not None:
                rp = rows(kp0s[g], dil)
                kc = jnp.concatenate([k_ref[0, rp, :], kc], axis=0)
                vc = jnp.concatenate([v_ref[0, rp, :], vc], axis=0)
            old = None if first else (m0_scr[rq, :], m1_scr[rq, :], l_scr[rq, :], acc_scr[rq, :])
            loaded.append((rq, q, kc.astype(BF16), vc.astype(BF16), old))
        units = [(g, hh) for g in range(len(loaded)) for hh in range(2)]
        reps = loaded[0][2].shape[0] // LANES
        s_u = [lax.dot_general(jnp.where(head0 == (hh == 0), loaded[g][1], 0.0).astype(BF16), loaded[g][2],
                               contract_last, preferred_element_type=F32) - biases[hh] for g, hh in units]
        m_u = [jnp.broadcast_to(jnp.max(s, axis=-1, keepdims=True), (QB, LANES)) for s in s_u]
        if not first:
            m_u = [jnp.maximum(loaded[g][4][hh], m) for (g, hh), m in zip(units, m_u)]
            a_u = [jnp.exp2(loaded[g][4][hh] - m) for (g, hh), m in zip(units, m_u)]
        p_u = [jnp.exp2(s - jnp.concatenate([m] * reps, axis=1)) for s, m in zip(s_u, m_u)]
        ps_u = [jnp.sum(p, axis=-1, keepdims=True) for p in p_u]
        pv_u = [jnp.dot(p.astype(BF16), loaded[g][3], preferred_element_type=F32)
                for (g, hh), p in zip(units, p_u)]
        for g in range(len(loaded)):
            rq, old = loaded[g][0], loaded[g][4]
            l_new = jnp.where(head0, ps_u[2 * g], ps_u[2 * g + 1])
            acc_new = jnp.where(head0, pv_u[2 * g], pv_u[2 * g + 1])
            if not first:
                alpha = jnp.where(head0, a_u[2 * g], a_u[2 * g + 1])
                l_new = alpha * old[2] + l_new
                acc_new = alpha * old[3] + acc_new
            m0_scr[rq, :] = m_u[2 * g]
            m1_scr[rq, :] = m_u[2 * g + 1]
            l_scr[rq, :] = l_new
            acc_scr[rq, :] = acc_new

    for branch, (window, dil) in enumerate(sorted(DIL_PATTERNS, key=lambda wd: -wd[1])):
        span = window // dil
        assert span <= QB and (S // dil) % QB == 0
        nblk = S // dil // QB
        first = branch == 0

        def masked_bias(delta, hh, dil=dil, span=span):
            return jnp.where((delta >= 0) & (delta <= span), slopes[hh] * (delta * dil).astype(F32), -NEG_INF)

        bias_cur = [masked_bias(delta_cur, hh) for hh in range(2)]
        bias_two = [masked_bias(delta_two, hh) for hh in range(2)]

        ga = min(G, dil)

        def head_step(t, carry, dil=dil, ga=ga, bias_cur=bias_cur, first=first):
            update([t * ga + g for g in range(ga)], None, dil, bias_cur, first)
            return carry

        lax.fori_loop(0, dil // ga, head_step, 0)

        def starts(idx, dil=dil):
            q0 = (idx % dil) + (1 + idx // dil) * (QB * dil)
            return q0, q0 - QB * dil

        def tail_step(t, carry, base=0, count=G, dil=dil, bias_two=bias_two, first=first):
            pairs = [starts(base + t * count + g) for g in range(count)]
            update([a for a, _ in pairs], [b for _, b in pairs], dil, bias_two, first)
            return carry

        n_tail = dil * (nblk - 1)
        lax.fori_loop(0, n_tail // G, tail_step, 0)
        if n_tail % G:
            tail_step(0, 0, base=(n_tail // G) * G, count=n_tail % G)
    o_ref[0] = (acc_scr[...] / l_scr[...]).astype(o_ref.dtype)


def _dilated_attention(qkv, B, S):
    n_pairs = N_HEADS_ATT // 2
    slopes = jnp.asarray(2.0 ** (-8.0 * (np.arange(N_HEADS_ATT) + 1) / N_HEADS_ATT), dtype=F32)
    qkv3 = qkv.reshape(B, S, 3 * ATT_WIDTH)
    out = pl.pallas_call(
        functools.partial(_dil_kernel, S=S),
        out_shape=jax.ShapeDtypeStruct((B, S, ATT_WIDTH), BF16),
        grid=(B, n_pairs),
        in_specs=[pl.BlockSpec(memory_space=pltpu.SMEM),
                  pl.BlockSpec((1, S, LANES), lambda b, p: (b, 0, p)),
                  pl.BlockSpec((1, S, LANES), lambda b, p: (b, 0, n_pairs + p)),
                  pl.BlockSpec((1, S, LANES), lambda b, p: (b, 0, 2 * n_pairs + p))],
        out_specs=pl.BlockSpec((1, S, LANES), lambda b, p: (b, 0, p)),
        scratch_shapes=[pltpu.VMEM((S, LANES), F32) for _ in range(4)],
        compiler_params=_cparams("parallel", "parallel"),
        name="dilated_attention",
    )(slopes, qkv3, qkv3, qkv3)
    return out.reshape(B * S, ATT_WIDTH)


def _attn_out_kernel(a0_ref, a1_ref, w_ref, x_ref, g_ref, b_ref, y_ref, ybf_ref):
    k0 = a0_ref.shape[1]
    mix = jnp.dot(a0_ref[...], w_ref[:k0, :], preferred_element_type=F32)
    mix = mix + jnp.dot(a1_ref[...], w_ref[k0:, :], preferred_element_type=F32)
    y = _layer_norm_rows(DEEPNORM_ALPHA * x_ref[...] + mix, g_ref[...], b_ref[...])
    y_ref[...] = y
    ybf_ref[...] = y.astype(BF16)


def _attn_out_ln(a0, a1, w, x, g, b):
    M, D = x.shape
    tm = min(LN_TM, M)
    row = lambda i: (i, 0)
    const = lambda i: (0, 0)
    return pl.pallas_call(
        _attn_out_kernel,
        out_shape=(jax.ShapeDtypeStruct((M, D), F32), jax.ShapeDtypeStruct((M, D), BF16)),
        grid=(M // tm,),
        in_specs=[pl.BlockSpec((tm, a0.shape[1]), row), pl.BlockSpec((tm, a1.shape[1]), row),
                  pl.BlockSpec(w.shape, const), pl.BlockSpec((tm, D), row),
                  pl.BlockSpec((1, D), const), pl.BlockSpec((1, D), const)],
        out_specs=(pl.BlockSpec((tm, D), row), pl.BlockSpec((tm, D), row)),
        compiler_params=_cparams("parallel"),
        name="attn_out_ln",
    )(a0, a1, w, x, g.reshape(1, D), b.reshape(1, D))


def _gdn_out_kernel(a_ref, w_ref, x_ref, g_ref, b_ref, r_ref, yt_ref, route_ref):
    tm = x_ref.shape[0]
    mix = jnp.dot(a_ref[...], w_ref[...], preferred_element_type=F32)
    y = _layer_norm_rows(DEEPNORM_ALPHA * x_ref[...] + mix, g_ref[...], b_ref[...])
    for s in range(SUBLANES):
        yt_ref[pl.ds(s, tm, stride=SUBLANES), :] = y[:, s * LANES:(s + 1) * LANES]
    logits = jnp.dot(y, r_ref[...], precision=HIGHEST, preferred_element_type=F32)
    lane = _iota((tm, LANES), 1)
    logits = jnp.where(lane < N_EXPERTS, logits, -jnp.inf)
    m1 = jnp.max(logits, axis=-1, keepdims=True)
    i1 = jnp.min(jnp.where(logits == m1, lane, LANES), axis=-1, keepdims=True)
    rest = jnp.where(lane == i1, -jnp.inf, logits)
    m2 = jnp.max(rest, axis=-1, keepdims=True)
    i2 = jnp.min(jnp.where(rest == m2, lane, LANES), axis=-1, keepdims=True)
    e2 = jnp.exp(m2 - m1)
    w1 = 1.0 / (1.0 + e2)
    w2 = e2 / (1.0 + e2)
    route = jnp.where(lane == 0, i1.astype(F32),
                      jnp.where(lane == 1, i2.astype(F32),
                                jnp.where(lane == 2, w1, jnp.where(lane == 3, w2, 0.0))))
    route_ref[...] = route


def _gdn_out_ln_route(a, w, x, g, b, router):
    M, D = x.shape
    tm = min(LN_TM, M)
    row = lambda i: (i, 0)
    const = lambda i: (0, 0)
    r_pad = jnp.zeros((D, LANES), F32).at[:, :N_EXPERTS].set(router.astype(F32))
    return pl.pallas_call(
        _gdn_out_kernel,
        out_shape=(jax.ShapeDtypeStruct((M * SUBLANES, LANES), F32),
                   jax.ShapeDtypeStruct((M, LANES), F32)),
        grid=(M // tm,),
        in_specs=[pl.BlockSpec((tm, a.shape[1]), row), pl.BlockSpec(w.shape, const),
                  pl.BlockSpec((tm, D), row), pl.BlockSpec((1, D), const),
                  pl.BlockSpec((1, D), const), pl.BlockSpec((D, LANES), const)],
        out_specs=(pl.BlockSpec((tm * SUBLANES, LANES), row), pl.BlockSpec((tm, LANES), row)),
        compiler_params=_cparams("parallel"),
        name="gdn_out_ln_route",
    )(a, w, x, g.reshape(1, D), b.reshape(1, D), r_pad)


def _ffn_dense_kernel(xbf_ref, wg_ref, wu_ref, wd_ref, x_ref, g_ref, b_ref, y_ref, ybf_ref, acc_scr):
    j = pl.program_id(1)

    @pl.when(j == 0)
    def _():
        acc_scr[...] = jnp.zeros_like(acc_scr)

    x = xbf_ref[...]
    hg = jnp.dot(x, wg_ref[...], preferred_element_type=F32)
    hu = jnp.dot(x, wu_ref[...], preferred_element_type=F32)
    h = (_silu(hg) * hu).astype(BF16)
    acc_scr[...] += jnp.dot(h, wd_ref[...], preferred_element_type=F32)

    @pl.when(j == pl.num_programs(1) - 1)
    def _():
        y = _layer_norm_rows(DEEPNORM_ALPHA * x_ref[...] + acc_scr[...], g_ref[...], b_ref[...])
        y_ref[...] = y
        ybf_ref[...] = y.astype(BF16)


def _ffn_dense_ln(xbf, x, wg, wu, wd, g, b):
    M, D = x.shape
    FF = wg.shape[1]
    tm = min(FFN_TM, M)
    tf = FFN_TF
    assert FF % tf == 0
    row = lambda i, j: (i, 0)
    const = lambda i, j: (0, 0)
    return pl.pallas_call(
        _ffn_dense_kernel,
        out_shape=(jax.ShapeDtypeStruct((M, D), F32), jax.ShapeDtypeStruct((M, D), BF16)),
        grid=(M // tm, FF // tf),
        in_specs=[pl.BlockSpec((tm, D), row),
                  pl.BlockSpec((D, tf), lambda i, j: (0, j)),
                  pl.BlockSpec((D, tf), lambda i, j: (0, j)),
                  pl.BlockSpec((tf, D), lambda i, j: (j, 0)),
                  pl.BlockSpec((tm, D), row),
                  pl.BlockSpec((1, D), const), pl.BlockSpec((1, D), const)],
        out_specs=(pl.BlockSpec((tm, D), row), pl.BlockSpec((tm, D), row)),
        scratch_shapes=[pltpu.VMEM((tm, D), F32)],
        compiler_params=_cparams("parallel", "arbitrary"),
        name="ffn_dense_ln",
    )(xbf, wg, wu, wd, x, g.reshape(1, D), b.reshape(1, D))


def _ffn_grouped_kernel(te_ref, na_ref, xt_ref, wg_ref, wu_ref, wd_ref, yt_ref, xbf_scr, acc_scr):
    i = pl.program_id(0)
    j = pl.program_id(1)
    tm = acc_scr.shape[0]
    active = i < na_ref[0]

    @pl.when(active & (j == 0))
    def _():
        acc_scr[...] = jnp.zeros_like(acc_scr)
        for s in range(SUBLANES):
            xbf_scr[:, s * LANES:(s + 1) * LANES] = xt_ref[pl.ds(s, tm, stride=SUBLANES), :].astype(BF16)

    @pl.when(active)
    def _():
        x = xbf_scr[...]
        hg = jnp.dot(x, wg_ref[0], preferred_element_type=F32)
        hu = jnp.dot(x, wu_ref[0], preferred_element_type=F32)
        h = (_silu(hg) * hu).astype(BF16)
        acc_scr[...] += jnp.dot(h, wd_ref[0], preferred_element_type=F32)

    @pl.when(active & (j == pl.num_programs(1) - 1))
    def _():
        for s in range(SUBLANES):
            yt_ref[pl.ds(s, tm, stride=SUBLANES), :] = acc_scr[:, s * LANES:(s + 1) * LANES]

    @pl.when(jnp.logical_not(active) & (j == 0))
    def _():
        yt_ref[...] = jnp.zeros_like(yt_ref)


def _ffn_grouped(xt, tile_expert, n_active, wg, wu, wd, tm):
    R = xt.shape[0] // SUBLANES
    E, D, FF = wg.shape
    tf = FFN_TF
    nf = FF // tf
    n_tiles = R // tm

    def row_map(i, j, te, na):
        return (jnp.minimum(i, na[0] - 1), 0)

    def ff_idx(i, j, na):
        return jnp.where(i < na[0], j, nf - 1)

    return pl.pallas_call(
        _ffn_grouped_kernel,
        out_shape=jax.ShapeDtypeStruct((R * SUBLANES, LANES), F32),
        grid_spec=pltpu.PrefetchScalarGridSpec(
            num_scalar_prefetch=2,
            grid=(n_tiles, nf),
            in_specs=[pl.BlockSpec((tm * SUBLANES, LANES), row_map),
                      pl.BlockSpec((1, D, tf), lambda i, j, te, na: (te[i], 0, ff_idx(i, j, na))),
                      pl.BlockSpec((1, D, tf), lambda i, j, te, na: (te[i], 0, ff_idx(i, j, na))),
                      pl.BlockSpec((1, tf, D), lambda i, j, te, na: (te[i], ff_idx(i, j, na), 0))],
            out_specs=pl.BlockSpec((tm * SUBLANES, LANES), lambda i, j, te, na: (i, 0)),
            scratch_shapes=[pltpu.VMEM((tm, D), BF16), pltpu.VMEM((tm, D), F32)]),
        compiler_params=_cparams("arbitrary", "arbitrary"),
        name="ffn_grouped",
    )(tile_expert, n_active, xt, wg, wu, wd)


def _row_tile(ref, r):
    return ref.at[pl.ds(pl.multiple_of(r, SUBLANES), SUBLANES), :]


def _dispatch_kernel(dest_ref, xt_ref, xs_in_hbm, xs_hbm, sem):
    del xs_in_hbm
    ch = dest_ref.shape[2] // 2

    def body(t, carry):
        src = _row_tile(xt_ref, t * SUBLANES)
        for k in range(2):
            pltpu.make_async_copy(src, _row_tile(xs_hbm, dest_ref[0, 0, 2 * t + k]), sem).start()
        return carry

    lax.fori_loop(0, ch, body, 0)
    for _ in range(2):
        pltpu.make_async_copy(xt_ref, xs_hbm.at[pl.ds(0, ch * SUBLANES), :], sem).wait()


def _dispatch_rows(xt, dest8, xs_init):
    N = xt.shape[0] // SUBLANES
    ch = min(DISPATCH_CHUNK, N)
    dest3 = dest8.reshape(N // ch, 1, 2 * ch)
    return pl.pallas_call(
        _dispatch_kernel,
        out_shape=jax.ShapeDtypeStruct(xs_init.shape, F32),
        grid=(N // ch,),
        in_specs=[pl.BlockSpec((1, 1, 2 * ch), lambda i: (i, 0, 0), memory_space=pltpu.SMEM),
                  pl.BlockSpec((ch * SUBLANES, LANES), lambda i: (i, 0)),
                  pl.BlockSpec(memory_space=pl.ANY)],
        out_specs=pl.BlockSpec(memory_space=pl.ANY),
        scratch_shapes=[pltpu.SemaphoreType.DMA(())],
        input_output_aliases={2: 0},
        compiler_params=_cparams("arbitrary"),
        name="moe_dispatch",
    )(dest3, xt, xs_init)


def _moe_ln_kernel(dest_ref, xt_ref, ys_hbm, route_ref, g_ref, b_ref, o_ref, ya_scr, yb_scr, sem):
    tm = o_ref.shape[0]

    def body(t, carry):
        pltpu.make_async_copy(_row_tile(ys_hbm, dest_ref[0, 0, 2 * t]),
                              _row_tile(ya_scr, t * SUBLANES), sem).start()
        pltpu.make_async_copy(_row_tile(ys_hbm, dest_ref[0, 0, 2 * t + 1]),
                              _row_tile(yb_scr, t * SUBLANES), sem).start()
        return carry

    lax.fori_loop(0, tm, body, 0)
    for scr in (ya_scr, yb_scr):
        pltpu.make_async_copy(ys_hbm.at[pl.ds(0, tm * SUBLANES), :], scr, sem).wait()

    w1 = route_ref[:, 2:3]
    w2 = route_ref[:, 3:4]
    parts = []
    for s in range(SUBLANES):
        rows = pl.ds(s, tm, stride=SUBLANES)
        parts.append(DEEPNORM_ALPHA * xt_ref[rows, :] + (w1 * ya_scr[rows, :] + w2 * yb_scr[rows, :]))
    d_model = SUBLANES * LANES
    mu = sum(jnp.sum(z, axis=-1, keepdims=True) for z in parts) / d_model
    var = sum(jnp.sum((z - mu) * (z - mu), axis=-1, keepdims=True) for z in parts) / d_model
    rstd = lax.rsqrt(var + LN_EPS)
    for s in range(SUBLANES):
        cols = slice(s * LANES, (s + 1) * LANES)
        o_ref[:, cols] = (parts[s] - mu) * rstd * g_ref[:, cols] + b_ref[:, cols]


def _moe_combine_ln(xt, ys, dest8, route, g, b):
    M = route.shape[0]
    D = SUBLANES * LANES
    tm = min(LN_TM, M)
    row = lambda i: (i, 0)
    const = lambda i: (0, 0)
    dest3 = dest8.reshape(M // tm, 1, 2 * tm)
    return pl.pallas_call(
        _moe_ln_kernel,
        out_shape=jax.ShapeDtypeStruct((M, D), F32),
        grid=(M // tm,),
        in_specs=[pl.BlockSpec((1, 1, 2 * tm), lambda i: (i, 0, 0), memory_space=pltpu.SMEM),
                  pl.BlockSpec((tm * SUBLANES, LANES), row),
                  pl.BlockSpec(memory_space=pl.ANY),
                  pl.BlockSpec((tm, LANES), row),
                  pl.BlockSpec((1, D), const), pl.BlockSpec((1, D), const)],
        out_specs=pl.BlockSpec((tm, D), row),
        scratch_shapes=[pltpu.VMEM((tm * SUBLANES, LANES), F32), pltpu.VMEM((tm * SUBLANES, LANES), F32),
                        pltpu.SemaphoreType.DMA(())],
        compiler_params=_cparams("arbitrary"),
        name="moe_combine_ln",
    )(dest3, xt, ys, route, g.reshape(1, D), b.reshape(1, D))


def _gdn_kernel(alog_ref, dt_ref, pq_ref, pk_ref, pv_ref, hq_ref, hk_ref, hv_ref,
                cq_ref, ck_ref, cv_ref, br_ref, ar_ref, ng_ref, gate_ref, o_ref,
                state_scr, sq_scr, sk_scr, sv_scr, *, blk, hp):
    C = GDN_CHUNK
    Dh = GDN_HEAD_DIM
    nchunk = blk // C
    h0 = pl.program_id(1) * hp
    sb = pl.program_id(2)
    heads = range(hp)

    @pl.when(sb == 0)
    def _():
        state_scr[...] = jnp.zeros_like(state_scr)

    have_prev = (sb > 0).astype(F32)

    def conv_silu(cur_ref, halo_ref, w_ref, stage_scr):
        stage_scr[0:SUBLANES, :] = halo_ref[0] * have_prev
        stage_scr[SUBLANES:SUBLANES + blk, :] = cur_ref[0]
        out = None
        for j in range(GDN_CONV):
            off = SUBLANES - (GDN_CONV - 1) + j
            term = w_ref[j:j + 1, :] * stage_scr[off:off + blk, :]
            out = term if out is None else out + term
        return _silu(out)

    def split(t):
        return [t[:, hh * Dh:(hh + 1) * Dh] for hh in heads]

    def l2n(t):
        return t * lax.rsqrt(jnp.sum(t * t, axis=-1, keepdims=True) + RMS_EPS)

    q_h = [l2n(t) * (Dh ** -0.5) for t in split(conv_silu(pq_ref, hq_ref, cq_ref, sq_scr))]
    k_h = [l2n(t) for t in split(conv_silu(pk_ref, hk_ref, ck_ref, sk_scr))]
    v_h = split(conv_silu(pv_ref, hv_ref, cv_ref, sv_scr))

    lanes_row = _iota((hp, blk), 0)
    dt_rows = jnp.zeros((hp, blk), F32)
    alog_rows = jnp.zeros((hp, blk), F32)
    for hh in heads:
        dt_rows = jnp.where(lanes_row == hh, dt_ref[h0 + hh], dt_rows)
        alog_rows = jnp.where(lanes_row == hh, alog_ref[h0 + hh], alog_rows)
    beta_rows = jax.nn.sigmoid(br_ref[0, 0, 0])
    za = ar_ref[0, 0, 0] + dt_rows
    g_rows = -jnp.exp(alog_rows) * (jnp.maximum(za, 0.0) + jnp.log1p(jnp.exp(-jnp.abs(za))))
    ri = _iota((blk, blk), 0)
    ci = _iota((blk, blk), 1)
    same = (ri // C) == (ci // C)
    g8 = jnp.concatenate([g_rows, jnp.zeros((SUBLANES - hp, blk), F32)], axis=0) if hp < SUBLANES else g_rows
    gam_rows = jnp.dot(g8, (same & (ri <= ci)).astype(F32), precision=HIGHEST,
                       preferred_element_type=F32)
    gl_rows = jnp.dot(g8, same.astype(F32), precision=HIGHEST,
                      preferred_element_type=F32)
    beta = [_row_to_col(beta_rows[hh:hh + 1, :]) for hh in heads]
    gam = [_row_to_col(gam_rows[hh:hh + 1, :]) for hh in heads]
    gl = [_row_to_col(gl_rows[hh:hh + 1, :]) for hh in heads]
    eg = [jnp.exp(t) for t in gam]
    ekd = [jnp.exp(a - b) for a, b in zip(gl, gam)]

    incl = same & (ri >= ci)
    strict = same & (ri > ci)
    contract_last = (((1,), (1,)), ((), ()))
    decay = [jnp.where(incl, jnp.exp(jnp.where(incl, gam[hh] - gam_rows[hh:hh + 1, :], 0.0)), 0.0)
             for hh in heads]
    kb = [t.astype(BF16) for t in k_h]
    kk = [lax.dot_general(t, t, contract_last, preferred_element_type=F32) for t in kb]
    x_acc = [jnp.where(strict, -(beta[hh] * kk[hh] * decay[hh]), 0.0) for hh in heads]
    pw = x_acc
    for _ in range(int(np.log2(C)) - 1):
        pwb = [t.astype(BF16) for t in pw]
        pw = [jnp.dot(t, t, preferred_element_type=F32) for t in pwb]
        x_acc = [x + p + jnp.dot(p.astype(BF16), x.astype(BF16), preferred_element_type=F32)
                 for x, p in zip(x_acc, pw)]
    rhs = [jnp.concatenate([v_h[hh] * beta[hh], k_h[hh] * (beta[hh] * eg[hh])], axis=1) for hh in heads]
    sol = [r + jnp.dot(x.astype(BF16), r.astype(BF16), preferred_element_type=F32)
           for x, r in zip(x_acc, rhs)]
    u = [t[:, :Dh] for t in sol]
    w_b = [t[:, Dh:].astype(BF16) for t in sol]
    qk = [lax.dot_general(q_h[hh].astype(BF16), kb[hh], contract_last, preferred_element_type=F32) * decay[hh]
          for hh in heads]
    q_dec = [(q_h[hh] * eg[hh]).astype(BF16) for hh in heads]
    k_dec = [(k_h[hh] * ekd[hh]).astype(BF16) for hh in heads]

    state = [state_scr[hh] for hh in heads]
    v_new = [[] for _ in heads]
    o_inter = [[] for _ in heads]
    for c in range(nchunk):
        rows = slice(c * C, (c + 1) * C)
        sbf = [t.astype(BF16) for t in state]
        vn = [u[hh][rows] - jnp.dot(w_b[hh][rows], sbf[hh], preferred_element_type=F32) for hh in heads]
        for hh in heads:
            o_inter[hh].append(jnp.dot(q_dec[hh][rows], sbf[hh], preferred_element_type=F32))
            v_new[hh].append(vn[hh])
        state = [state[hh] * jnp.exp(gl[hh][c * C:c * C + 1, :])
                 + lax.dot_general(k_dec[hh][rows], vn[hh].astype(BF16), (((0,), (0,)), ((), ())),
                                   preferred_element_type=F32) for hh in heads]
    for hh in heads:
        state_scr[hh] = state[hh]
    o = [jnp.concatenate(o_inter[hh], axis=0)
         + jnp.dot(qk[hh].astype(BF16), jnp.concatenate(v_new[hh], axis=0).astype(BF16),
                   preferred_element_type=F32) for hh in heads]
    o = [t * lax.rsqrt(jnp.mean(t * t, axis=-1, keepdims=True) + RMS_EPS) * ng_ref[...] for t in o]
    o = o[0] if hp == 1 else jnp.concatenate(o, axis=1)
    o_ref[0] = (o * _silu(gate_ref[0])).astype(o_ref.dtype)


def _gated_deltanet(pre, ab_rows, gate, conv_w, a_log, dt_bias, norm_g, B, S):
    blk = min(GDN_BLK, S)
    H = N_HEADS_GDN
    hp = GDN_HEADS_PER_STEP
    hg = H // hp
    wide = hp * GDN_HEAD_DIM
    pre3 = pre.reshape(B, S, 3 * GDN_WIDTH)
    gate3 = gate.reshape(B, S, GDN_WIDTH)
    ab5 = ab_rows.reshape(B, 2, hg, hp, S)
    halo_blocks = blk // SUBLANES

    def cur(sec):
        return pl.BlockSpec((1, blk, wide), lambda b, h, s: (b, s, sec * hg + h))

    def halo(sec):
        return pl.BlockSpec((1, SUBLANES, wide),
                            lambda b, h, s: (b, jnp.maximum(s * halo_blocks - 1, 0), sec * hg + h))

    def cw(sec):
        return pl.BlockSpec((GDN_CONV, wide), lambda b, h, s: (0, sec * hg + h))

    smem = pl.BlockSpec(memory_space=pltpu.SMEM)
    out = pl.pallas_call(
        functools.partial(_gdn_kernel, blk=blk, hp=hp),
        out_shape=jax.ShapeDtypeStruct((B, S, GDN_WIDTH), BF16),
        grid=(B, hg, S // blk),
        in_specs=[smem, smem,
                  cur(0), cur(1), cur(2), halo(0), halo(1), halo(2),
                  cw(0), cw(1), cw(2),
                  pl.BlockSpec((1, 1, 1, hp, blk), lambda b, h, s: (b, 0, h, 0, s)),
                  pl.BlockSpec((1, 1, 1, hp, blk), lambda b, h, s: (b, 1, h, 0, s)),
                  pl.BlockSpec((1, LANES), lambda b, h, s: (0, 0)),
                  pl.BlockSpec((1, blk, wide), lambda b, h, s: (b, s, h))],
        out_specs=pl.BlockSpec((1, blk, wide), lambda b, h, s: (b, s, h)),
        scratch_shapes=[pltpu.VMEM((hp, GDN_HEAD_DIM, GDN_HEAD_DIM), F32)]
                       + [pltpu.VMEM((blk + SUBLANES, wide), F32) for _ in range(3)],
        compiler_params=_cparams("parallel", "parallel", "arbitrary"),
        name="gated_deltanet",
    )(a_log.astype(F32), dt_bias.astype(F32), pre3, pre3, pre3, pre3, pre3, pre3,
      conv_w, conv_w, conv_w, ab5, ab5, norm_g.reshape(1, LANES).astype(F32), gate3)
    return out.reshape(B * S, GDN_WIDTH)


def _pad_cols(w, width):
    return jnp.zeros((w.shape[0], width), w.dtype).at[:, :w.shape[1]].set(w)


def _attention_layer(x, xbf, B, S, w_in, forget_bias, w_out, ln_g, ln_b,
                     w_gate, w_up, w_down, ln2_g, ln2_b):
    W = ATT_WIDTH
    H = N_HEADS_ATT
    w_fox = jnp.concatenate([w_in[:, :W] * (HEAD_DIM ** -0.5 * LOG2E), w_in[:, W:3 * W]], axis=1).astype(BF16)
    w_f = _pad_cols(w_in[:, 3 * W:3 * W + H], LANES).astype(BF16)
    w_dil = w_in[:, 3 * W + H:].astype(BF16)
    qkv_fox = _proj(xbf, w_fox, BF16)
    f_logit = _proj(xbf, w_f, F32)
    qkv_dil = _proj(xbf, w_dil, F32)
    f_rows = f_logit[:, :H].reshape(B, S, H).transpose(0, 2, 1)
    f_cum = _forget_cumsum(f_rows, forget_bias)
    o_fox = _fox_attention(qkv_fox, f_cum, B, S)
    o_dil = _dilated_attention(qkv_dil, B, S)
    x1, x1bf = _attn_out_ln(o_fox, o_dil, w_out.astype(BF16), x, ln_g, ln_b)
    return _ffn_dense_ln(x1bf, x1, w_gate.astype(BF16), w_up.astype(BF16), w_down.astype(BF16),
                         ln2_g, ln2_b)


def _slot_indices(route, tm):
    N = route.shape[0]
    experts = route[:, 0:2].astype(jnp.int32).reshape(2 * N)
    onehot = (experts[:, None] == jnp.arange(N_EXPERTS, dtype=jnp.int32)[None, :]).astype(jnp.int32)
    csum = jnp.cumsum(onehot, axis=0)
    counts = csum[-1]
    padded = ((counts + tm - 1) // tm) * tm
    ends = jnp.cumsum(padded)
    starts = ends - padded
    dest = jnp.sum(onehot * (csum - 1 + starts[None, :]), axis=1).astype(jnp.int32)
    n_tiles = (2 * N) // tm + N_EXPERTS
    tile_start = jnp.arange(n_tiles, dtype=jnp.int32) * tm
    tile_expert = jnp.minimum(jnp.sum((tile_start[:, None] >= ends[None, :]).astype(jnp.int32), axis=1),
                              N_EXPERTS - 1).astype(jnp.int32)
    n_active = (ends[-1] // tm).astype(jnp.int32).reshape(1)
    return dest, tile_expert, n_active, n_tiles


def _deltanet_layer(x, xbf, B, S, w_in, conv_w, a_log, dt_bias, norm_g, w_out, ln_g, ln_b,
                    router, w_gate, w_up, w_down, ln2_g, ln2_b):
    N = B * S
    W = GDN_WIDTH
    H = N_HEADS_GDN
    w_qkv = w_in[:, :3 * W].astype(BF16)
    w_ab = _pad_cols(w_in[:, 3 * W:3 * W + 2 * H], LANES).astype(BF16)
    w_gt = w_in[:, 3 * W + 2 * H:].astype(BF16)
    pre = _proj(xbf, w_qkv, F32)
    ab = _proj(xbf, w_ab, F32)
    gate = _proj(xbf, w_gt, F32)
    ab_rows = ab[:, :2 * H].reshape(B, S, 2 * H).transpose(0, 2, 1).reshape(B, 2 * H, 1, S)
    o = _gated_deltanet(pre, ab_rows, gate, conv_w.astype(F32), a_log, dt_bias, norm_g, B, S)
    xt, route = _gdn_out_ln_route(o, w_out.astype(BF16), x, ln_g, ln_b, router)

    tm = min(FFN_TM, N)
    dest, tile_expert, n_active, n_tiles = _slot_indices(route, tm)
    dest8 = dest * SUBLANES
    xs_init = jnp.zeros((n_tiles * tm * SUBLANES, LANES), F32)
    xs = _dispatch_rows(xt, dest8, xs_init)
    ys = _ffn_grouped(xs, tile_expert, n_active,
                      w_gate.astype(BF16), w_up.astype(BF16), w_down.astype(BF16), tm)
    return _moe_combine_ln(xt, ys, dest8, route, ln2_g, ln2_b)


def kernel(x, attn_w_in, fox_forget_bias, attn_w_out, ln_attn_g, ln_attn_b, ffn_w_gate, ffn_w_up,
           ffn_w_down, ln_ffn_g, ln_ffn_b, gdn_w_in, gdn_conv_w, gdn_a_log, gdn_dt_bias, gdn_norm_g,
           gdn_w_out, ln_gdn_g, ln_gdn_b, moe_router, moe_w_gate, moe_w_up, moe_w_down, ln_moe_g,
           ln_moe_b):
    B, S, D = x.shape
    x2 = x.reshape(B * S, D)
    x2bf = x2.astype(BF16)
    x2, x2bf = _attention_layer(x2, x2bf, B, S, attn_w_in[0], fox_forget_bias[0], attn_w_out[0],
                                ln_attn_g[0], ln_attn_b[0], ffn_w_gate[0], ffn_w_up[0], ffn_w_down[0],
                                ln_ffn_g[0], ln_ffn_b[0])
    y = _deltanet_layer(x2, x2bf, B, S, gdn_w_in[0], gdn_conv_w[0], gdn_a_log[0], gdn_dt_bias[0],
                        gdn_norm_g[0], gdn_w_out[0], ln_gdn_g[0], ln_gdn_b[0], moe_router[0],
                        moe_w_gate[0], moe_w_up[0], moe_w_down[0], ln_moe_g[0], ln_moe_b[0])
    return y.reshape(B, S, D)
```

```python
import functools

import numpy as np
import jax
import jax.numpy as jnp
from jax import lax
from jax.experimental import pallas as pl
from jax.experimental.pallas import tpu as pltpu

F32 = jnp.float32
BF16 = jnp.bfloat16
HIGHEST = lax.Precision.HIGHEST

LANES = 128
SUBLANES = 8
VMEM_LIMIT = 52 * 1024 * 1024

HEAD_DIM = 64
N_HEADS_ATT = 8
ATT_WIDTH = N_HEADS_ATT * HEAD_DIM
QUERY_BLOCK = 128
DIL_PATTERNS = ((128, 1), (512, 4), (2048, 16))
GDN_HEAD_DIM = 128
N_HEADS_GDN = 8
GDN_WIDTH = N_HEADS_GDN * GDN_HEAD_DIM
GDN_CONV = 4
GDN_CHUNK = 64
N_EXPERTS = 8
DEPTH = 2
DEEPNORM_ALPHA = (2.0 * DEPTH) ** 0.25
LN_EPS = 1e-5
RMS_EPS = 1e-6
NEG_INF = -1e30
LOG2E = 1.4426950408889634

PROJ_TM = 1024
PROJ_TN = 1536
FOX_TQ = 1024
FOX_GROUP = 512
FOX_STRIP = 32
DIL_GROUP = 4
LN_TM = 512
FFN_TM = 1024
FFN_TF = 896
GDN_BLK = 128
GDN_HEADS_PER_STEP = 8
DISPATCH_CHUNK = 512


def _cparams(*sem):
    return pltpu.CompilerParams(dimension_semantics=sem, vmem_limit_bytes=VMEM_LIMIT)


def _iota(shape, dim):
    return lax.broadcasted_iota(jnp.int32, shape, dim)


def _silu(x):
    return x * jax.nn.sigmoid(x)


def _row_to_col(row):
    n = row.shape[1]
    eye = _iota((LANES, LANES), 0) == _iota((LANES, LANES), 1)
    cols = []
    for c in range(n // LANES):
        seg = row[:, c * LANES:(c + 1) * LANES]
        cols.append(jnp.sum(jnp.where(eye, seg, 0.0), axis=1, keepdims=True))
    return cols[0] if len(cols) == 1 else jnp.concatenate(cols, axis=0)


def _layer_norm_rows(z, g, b):
    mu = jnp.mean(z, axis=-1, keepdims=True)
    zc = z - mu
    var = jnp.mean(zc * zc, axis=-1, keepdims=True)
    return zc * lax.rsqrt(var + LN_EPS) * g + b


def _proj_kernel(x_ref, w_ref, o_ref):
    o_ref[...] = jnp.dot(x_ref[...], w_ref[...], preferred_element_type=F32).astype(o_ref.dtype)


def _proj(x, w, out_dtype):
    M, K = x.shape
    C = w.shape[1]
    tm = min(PROJ_TM, M)
    tn = min(PROJ_TN, C)
    assert M % tm == 0 and C % tn == 0
    return pl.pallas_call(
        _proj_kernel,
        out_shape=jax.ShapeDtypeStruct((M, C), out_dtype),
        grid=(M // tm, C // tn),
        in_specs=[pl.BlockSpec((tm, K), lambda i, j: (i, 0)),
                  pl.BlockSpec((K, tn), lambda i, j: (0, j))],
        out_specs=pl.BlockSpec((tm, tn), lambda i, j: (i, j)),
        compiler_params=_cparams("parallel", "parallel"),
        name="proj",
    )(x, w)


def _forget_cumsum_kernel(f_ref, b_ref, o_ref):
    S = f_ref.shape[2]
    z = f_ref[0] + b_ref[...]
    lf = (jnp.minimum(z, 0.0) - jnp.log1p(jnp.exp(-jnp.abs(z)))) * LOG2E
    upper = (_iota((LANES, LANES), 0) <= _iota((LANES, LANES), 1)).astype(F32)
    carry = jnp.zeros((z.shape[0], 1), F32)
    for c in range(S // LANES):
        seg = jnp.dot(lf[:, c * LANES:(c + 1) * LANES], upper, precision=HIGHEST,
                      preferred_element_type=F32) + carry
        o_ref[0, :, c * LANES:(c + 1) * LANES] = seg
        carry = seg[:, LANES - 1:LANES]


def _forget_cumsum(f_rows, bias):
    B, H, S = f_rows.shape
    return pl.pallas_call(
        _forget_cumsum_kernel,
        out_shape=jax.ShapeDtypeStruct((B, H, S), F32),
        grid=(B,),
        in_specs=[pl.BlockSpec((1, H, S), lambda b: (b, 0, 0)),
                  pl.BlockSpec((H, 1), lambda b: (0, 0))],
        out_specs=pl.BlockSpec((1, H, S), lambda b: (b, 0, 0)),
        compiler_params=_cparams("parallel"),
        name="forget_cumsum",
    )(f_rows, bias.reshape(H, 1).astype(F32))


def _fox_kernel(q_ref, k_ref, v_ref, fq_ref, fk_ref, o_ref,
                s_scr, p_scr, m_scr, l_scr, a_scr, acc_scr, fq_scr, *, tq):
    G = min(FOX_GROUP, tq)
    R = FOX_STRIP
    i = pl.program_id(2)
    contract_last = (((1,), (1,)), ((), ()))
    units = [(hh, g) for hh in range(2) for g in range(tq // G)]
    lane_g = _iota((G, LANES), 1)
    qh = []
    for hh, g in units:
        qg = q_ref[0, g * G:(g + 1) * G, :]
        qh.append(jnp.where((lane_g < HEAD_DIM) == (hh == 0), qg, jnp.zeros_like(qg)))
    for hh in range(2):
        fq_scr[hh] = jnp.broadcast_to(_row_to_col(fq_ref[0, 0, 0, hh:hh + 1, :]), (tq, LANES))
    m_scr[...] = jnp.full(m_scr.shape, NEG_INF, F32)
    l_scr[...] = jnp.zeros(l_scr.shape, F32)
    acc_scr[...] = jnp.zeros(acc_scr.shape, F32)

    def kv_block(j, masked):
        kstart = pl.multiple_of(j * tq, tq)

        def ncols(g):
            return (g + 1) * G if masked else tq

        def scores(u):
            hh, g = units[u]
            n = ncols(g)
            kb = k_ref[0, pl.ds(kstart, n), :]
            s_scr[u, :, :n] = (lax.dot_general(qh[u], kb, contract_last, preferred_element_type=F32)
                               - fk_ref[0, 0, j, hh:hh + 1, :n])

        def strips(u):
            hh, g = units[u]
            n = ncols(g)
            for r in range(G // R):
                rows = slice(r * R, (r + 1) * R)
                grow = slice(g * G + r * R, g * G + (r + 1) * R)
                s = s_scr[u, rows, :n]
                if masked:
                    s = jnp.where(_iota((R, n), 1) <= _iota((R, n), 0) + (g * G + r * R), s, NEG_INF)
                fq = fq_scr[hh, grow, :]
                m_old = m_scr[hh, grow, :]
                m_new = jnp.maximum(m_old, jnp.max(s, axis=-1, keepdims=True) + fq)
                p = jnp.exp2(s - jnp.concatenate([m_new - fq] * (n // LANES), axis=1))
                alpha = jnp.exp2(m_old - m_new)
                a_scr[hh, grow, :] = alpha
                l_scr[hh, grow, :] = alpha * l_scr[hh, grow, :] + jnp.sum(p, axis=-1, keepdims=True)
                m_scr[hh, grow, :] = m_new
                p_scr[u, rows, :n] = p.astype(BF16)

        def values(u):
            hh, g = units[u]
            n = ncols(g)
            grow = slice(g * G, (g + 1) * G)
            vb = v_ref[0, pl.ds(kstart, n), :]
            acc_scr[hh, grow, :] = (a_scr[hh, grow, :] * acc_scr[hh, grow, :]
                                    + jnp.dot(p_scr[u, :, :n], vb, preferred_element_type=F32))

        scores(0)
        for u in range(1, len(units)):
            scores(u)
            strips(u - 1)
            values(u - 1)
        strips(len(units) - 1)
        values(len(units) - 1)

    def body(j, carry):
        kv_block(j, False)
        return carry

    lax.fori_loop(0, i, body, 0)
    kv_block(i, True)
    lane = _iota((tq, LANES), 1)
    o_ref[0] = jnp.where(lane < HEAD_DIM, acc_scr[0] / l_scr[0], acc_scr[1] / l_scr[1]).astype(o_ref.dtype)


def _fox_attention(qkv, f_cum, B, S):
    tq = min(FOX_TQ, S)
    nk = S // tq
    grp = min(FOX_GROUP, tq)
    n_units = 2 * (tq // grp)
    n_pairs = N_HEADS_ATT // 2
    qkv3 = qkv.reshape(B, S, 3 * ATT_WIDTH)
    f5 = f_cum.reshape(B, n_pairs, 2, nk, tq).transpose(0, 1, 3, 2, 4)
    out = pl.pallas_call(
        functools.partial(_fox_kernel, tq=tq),
        out_shape=jax.ShapeDtypeStruct((B, S, ATT_WIDTH), BF16),
        grid=(B, n_pairs, nk),
        in_specs=[pl.BlockSpec((1, tq, LANES), lambda b, p, i: (b, i, p)),
                  pl.BlockSpec((1, S, LANES), lambda b, p, i: (b, 0, n_pairs + p)),
                  pl.BlockSpec((1, S, LANES), lambda b, p, i: (b, 0, 2 * n_pairs + p)),
                  pl.BlockSpec((1, 1, 1, 2, tq), lambda b, p, i: (b, p, i, 0, 0)),
                  pl.BlockSpec((1, 1, nk, 2, tq), lambda b, p, i: (b, p, 0, 0, 0))],
        out_specs=pl.BlockSpec((1, tq, LANES), lambda b, p, i: (b, i, p)),
        scratch_shapes=[pltpu.VMEM((n_units, grp, tq), F32), pltpu.VMEM((n_units, grp, tq), BF16)]
                       + [pltpu.VMEM((2, tq, LANES), F32) for _ in range(5)],
        compiler_params=_cparams("parallel", "parallel", "arbitrary"),
        name="fox_attention",
    )(qkv3, qkv3, qkv3, f5, f5)
    return out.reshape(B * S, ATT_WIDTH)


def _dil_kernel(slope_ref, q_ref, k_ref, v_ref, o_ref, m0_scr, m1_scr, l_scr, acc_scr, *, S):
    QB = QUERY_BLOCK
    G = DIL_GROUP
    p_idx = pl.program_id(1)
    lane = _iota((QB, LANES), 1)
    head0 = lane < HEAD_DIM
    contract_last = (((1,), (1,)), ((), ()))
    qscale = HEAD_DIM ** -0.5 * LOG2E
    slopes = [slope_ref[2 * p_idx + hh] * LOG2E for hh in range(2)]
    delta_cur = _iota((QB, QB), 0) - _iota((QB, QB), 1)
    delta_two = QB + _iota((QB, 2 * QB), 0) - _iota((QB, 2 * QB), 1)

    def rows(start, dil):
        return pl.ds(start, QB, stride=dil) if dil > 1 else pl.ds(start, QB)

    def update(q0s, kp0s, dil, biases, first):
        loaded = []
        for g, q0 in enumerate(q0s):
            rq = rows(q0, dil)
            q = q_ref[0, rq, :] * qscale
            kc = k_ref[0, rq, :]
            vc = v_ref[0, rq, :]
            if kp0s is not None:
                rp = rows(kp0s[g], dil)
                kc = jnp.concatenate([k_ref[0, rp, :], kc], axis=0)
                vc = jnp.concatenate([v_ref[0, rp, :], vc], axis=0)
            old = None if first else (m0_scr[rq, :], m1_scr[rq, :], l_scr[rq, :], acc_scr[rq, :])
            loaded.append((rq, q, kc.astype(BF16), vc.astype(BF16), old))
        units = [(g, hh) for g in range(len(loaded)) for hh in range(2)]
        reps = loaded[0][2].shape[0] // LANES
        s_u = [lax.dot_general(jnp.where(head0 == (hh == 0), loaded[g][1], 0.0).astype(BF16), loaded[g][2],
                               contract_last, preferred_element_type=F32) - biases[hh] for g, hh in units]
        m_u = [jnp.broadcast_to(jnp.max(s, axis=-1, keepdims=True), (QB, LANES)) for s in s_u]
        if not first:
            m_u = [jnp.maximum(loaded[g][4][hh], m) for (g, hh), m in zip(units, m_u)]
            a_u = [jnp.exp2(loaded[g][4][hh] - m) for (g, hh), m in zip(units, m_u)]
        p_u = [jnp.exp2(s - jnp.concatenate([m] * reps, axis=1)) for s, m in zip(s_u, m_u)]
        ps_u = [jnp.sum(p, axis=-1, keepdims=True) for p in p_u]
        pv_u = [jnp.dot(p.astype(BF16), loaded[g][3], preferred_element_type=F32)
                for (g, hh), p in zip(units, p_u)]
        for g in range(len(loaded)):
            rq, old = loaded[g][0], loaded[g][4]
            l_new = jnp.where(head0, ps_u[2 * g], ps_u[2 * g + 1])
            acc_new = jnp.where(head0, pv_u[2 * g], pv_u[2 * g + 1])
            if not first:
                alpha = jnp.where(head0, a_u[2 * g], a_u[2 * g + 1])
                l_new = alpha * old[2] + l_new
                acc_new = alpha * old[3] + acc_new
            m0_scr[rq, :] = m_u[2 * g]
            m1_scr[rq, :] = m_u[2 * g + 1]
            l_scr[rq, :] = l_new
            acc_scr[rq, :] = acc_new

    for branch, (window, dil) in enumerate(sorted(DIL_PATTERNS, key=lambda wd: -wd[1])):
        span = window // dil
        assert span <= QB and (S // dil) % QB == 0
        nblk = S // dil // QB
        first = branch == 0

        def masked_bias(delta, hh, dil=dil, span=span):
            return jnp.where((delta >= 0) & (delta <= span), slopes[hh] * (delta * dil).astype(F32), -NEG_INF)

        bias_cur = [masked_bias(delta_cur, hh) for hh in range(2)]
        bias_two = [masked_bias(delta_two, hh) for hh in range(2)]

        ga = min(G, dil)

        def head_step(t, carry, dil=dil, ga=ga, bias_cur=bias_cur, first=first):
            update([t * ga + g for g in range(ga)], None, dil, bias_cur, first)
            return carry

        lax.fori_loop(0, dil // ga, head_step, 0)

        def starts(idx, dil=dil):
            q0 = (idx % dil) + (1 + idx // dil) * (QB * dil)
            return q0, q0 - QB * dil

        def tail_step(t, carry, base=0, count=G, dil=dil, bias_two=bias_two, first=first):
            pairs = [starts(base + t * count + g) for g in range(count)]
            update([a for a, _ in pairs], [b for _, b in pairs], dil, bias_two, first)
            return carry

        n_tail = dil * (nblk - 1)
        lax.fori_loop(0, n_tail // G, tail_step, 0)
        if n_tail % G:
            tail_step(0, 0, base=(n_tail // G) * G, count=n_tail % G)
    o_ref[0] = (acc_scr[...] / l_scr[...]).astype(o_ref.dtype)


def _dilated_attention(qkv, B, S):
    n_pairs = N_HEADS_ATT // 2
    slopes = jnp.asarray(2.0 ** (-8.0 * (np.arange(N_HEADS_ATT) + 1) / N_HEADS_ATT), dtype=F32)
    qkv3 = qkv.reshape(B, S, 3 * ATT_WIDTH)
    out = pl.pallas_call(
        functools.partial(_dil_kernel, S=S),
        out_shape=jax.ShapeDtypeStruct((B, S, ATT_WIDTH), BF16),
        grid=(B, n_pairs),
        in_specs=[pl.BlockSpec(memory_space=pltpu.SMEM),
                  pl.BlockSpec((1, S, LANES), lambda b, p: (b, 0, p)),
                  pl.BlockSpec((1, S, LANES), lambda b, p: (b, 0, n_pairs + p)),
                  pl.BlockSpec((1, S, LANES), lambda b, p: (b, 0, 2 * n_pairs + p))],
        out_specs=pl.BlockSpec((1, S, LANES), lambda b, p: (b, 0, p)),
        scratch_shapes=[pltpu.VMEM((S, LANES), F32) for _ in range(4)],
        compiler_params=_cparams("parallel", "parallel"),
        name="dilated_attention",
    )(slopes, qkv3, qkv3, qkv3)
    return out.reshape(B * S, ATT_WIDTH)


def _attn_out_kernel(a0_ref, a1_ref, w_ref, x_ref, g_ref, b_ref, y_ref, ybf_ref):
    k0 = a0_ref.shape[1]
    mix = jnp.dot(a0_ref[...], w_ref[:k0, :], preferred_element_type=F32)
    mix = mix + jnp.dot(a1_ref[...], w_ref[k0:, :], preferred_element_type=F32)
    y = _layer_norm_rows(DEEPNORM_ALPHA * x_ref[...] + mix, g_ref[...], b_ref[...])
    y_ref[...] = y
    ybf_ref[...] = y.astype(BF16)


def _attn_out_ln(a0, a1, w, x, g, b):
    M, D = x.shape
    tm = min(LN_TM, M)
    row = lambda i: (i, 0)
    const = lambda i: (0, 0)
    return pl.pallas_call(
        _attn_out_kernel,
        out_shape=(jax.ShapeDtypeStruct((M, D), F32), jax.ShapeDtypeStruct((M, D), BF16)),
        grid=(M // tm,),
        in_specs=[pl.BlockSpec((tm, a0.shape[1]), row), pl.BlockSpec((tm, a1.shape[1]), row),
                  pl.BlockSpec(w.shape, const), pl.BlockSpec((tm, D), row),
                  pl.BlockSpec((1, D), const), pl.BlockSpec((1, D), const)],
        out_specs=(pl.BlockSpec((tm, D), row), pl.BlockSpec((tm, D), row)),
        compiler_params=_cparams("parallel"),
        name="attn_out_ln",
    )(a0, a1, w, x, g.reshape(1, D), b.reshape(1, D))


def _gdn_out_kernel(a_ref, w_ref, x_ref, g_ref, b_ref, r_ref, yt_ref, route_ref):
    tm = x_ref.shape[0]
    mix = jnp.dot(a_ref[...], w_ref[...], preferred_element_type=F32)
    y = _layer_norm_rows(DEEPNORM_ALPHA * x_ref[...] + mix, g_ref[...], b_ref[...])
    for s in range(SUBLANES):
        yt_ref[pl.ds(s, tm, stride=SUBLANES), :] = y[:, s * LANES:(s + 1) * LANES]
    y_hi = y.astype(BF16)
    y_lo = (y - y_hi.astype(F32)).astype(BF16)
    both = jnp.dot(y_hi, r_ref[...], preferred_element_type=F32)
    logits = both[:, :LANES] + (both[:, LANES:] + jnp.dot(y_lo, r_ref[:, :LANES], preferred_element_type=F32))
    lane = _iota((tm, LANES), 1)
    logits = jnp.where(lane < N_EXPERTS, logits, -jnp.inf)
    m1 = jnp.max(logits, axis=-1, keepdims=True)
    i1 = jnp.min(jnp.where(logits == m1, lane, LANES), axis=-1, keepdims=True)
    rest = jnp.where(lane == i1, -jnp.inf, logits)
    m2 = jnp.max(rest, axis=-1, keepdims=True)
    i2 = jnp.min(jnp.where(rest == m2, lane, LANES), axis=-1, keepdims=True)
    e2 = jnp.exp(m2 - m1)
    w1 = 1.0 / (1.0 + e2)
    w2 = e2 / (1.0 + e2)
    route = jnp.where(lane == 0, i1.astype(F32),
                      jnp.where(lane == 1, i2.astype(F32),
                                jnp.where(lane == 2, w1, jnp.where(lane == 3, w2, 0.0))))
    route_ref[...] = route


def _gdn_out_ln_route(a, w, x, g, b, router):
    M, D = x.shape
    tm = min(LN_TM, M)
    row = lambda i: (i, 0)
    const = lambda i: (0, 0)
    r_pad = jnp.zeros((D, LANES), F32).at[:, :N_EXPERTS].set(router.astype(F32))
    r_hi = r_pad.astype(BF16)
    r_lo = (r_pad - r_hi.astype(F32)).astype(BF16)
    r_pad = jnp.concatenate([r_hi, r_lo], axis=1)
    return pl.pallas_call(
        _gdn_out_kernel,
        out_shape=(jax.ShapeDtypeStruct((M * SUBLANES, LANES), F32),
                   jax.ShapeDtypeStruct((M, LANES), F32)),
        grid=(M // tm,),
        in_specs=[pl.BlockSpec((tm, a.shape[1]), row), pl.BlockSpec(w.shape, const),
                  pl.BlockSpec((tm, D), row), pl.BlockSpec((1, D), const),
                  pl.BlockSpec((1, D), const), pl.BlockSpec((D, 2 * LANES), const)],
        out_specs=(pl.BlockSpec((tm * SUBLANES, LANES), row), pl.BlockSpec((tm, LANES), row)),
        compiler_params=_cparams("parallel"),
        name="gdn_out_ln_route",
    )(a, w, x, g.reshape(1, D), b.reshape(1, D), r_pad)


def _ffn_dense_kernel(xbf_ref, wg_ref, wu_ref, wd_ref, x_ref, g_ref, b_ref, y_ref, ybf_ref, acc_scr):
    j = pl.program_id(1)

    @pl.when(j == 0)
    def _():
        acc_scr[...] = jnp.zeros_like(acc_scr)

    x = xbf_ref[...]
    hg = jnp.dot(x, wg_ref[...], preferred_element_type=F32)
    hu = jnp.dot(x, wu_ref[...], preferred_element_type=F32)
    h = (_silu(hg) * hu).astype(BF16)
    acc_scr[...] += jnp.dot(h, wd_ref[...], preferred_element_type=F32)

    @pl.when(j == pl.num_programs(1) - 1)
    def _():
        y = _layer_norm_rows(DEEPNORM_ALPHA * x_ref[...] + acc_scr[...], g_ref[...], b_ref[...])
        y_ref[...] = y
        ybf_ref[...] = y.astype(BF16)


def _ffn_dense_ln(xbf, x, wg, wu, wd, g, b):
    M, D = x.shape
    FF = wg.shape[1]
    tm = min(FFN_TM, M)
    tf = FFN_TF
    assert FF % tf == 0
    row = lambda i, j: (i, 0)
    const = lambda i, j: (0, 0)
    return pl.pallas_call(
        _ffn_dense_kernel,
        out_shape=(jax.ShapeDtypeStruct((M, D), F32), jax.ShapeDtypeStruct((M, D), BF16)),
        grid=(M // tm, FF // tf),
        in_specs=[pl.BlockSpec((tm, D), row),
                  pl.BlockSpec((D, tf), lambda i, j: (0, j)),
                  pl.BlockSpec((D, tf), lambda i, j: (0, j)),
                  pl.BlockSpec((tf, D), lambda i, j: (j, 0)),
                  pl.BlockSpec((tm, D), row),
                  pl.BlockSpec((1, D), const), pl.BlockSpec((1, D), const)],
        out_specs=(pl.BlockSpec((tm, D), row), pl.BlockSpec((tm, D), row)),
        scratch_shapes=[pltpu.VMEM((tm, D), F32)],
        compiler_params=_cparams("parallel", "arbitrary"),
        name="ffn_dense_ln",
    )(xbf, wg, wu, wd, x, g.reshape(1, D), b.reshape(1, D))


def _ffn_grouped_kernel(te_ref, na_ref, xt_ref, wg_ref, wu_ref, wd_ref, yt_ref, xbf_scr, acc_scr):
    i = pl.program_id(0)
    j = pl.program_id(1)
    tm = acc_scr.shape[0]
    active = i < na_ref[0]

    @pl.when(active & (j == 0))
    def _():
        acc_scr[...] = jnp.zeros_like(acc_scr)
        for s in range(SUBLANES):
            xbf_scr[:, s * LANES:(s + 1) * LANES] = xt_ref[pl.ds(s, tm, stride=SUBLANES), :].astype(BF16)

    @pl.when(active)
    def _():
        x = xbf_scr[...]
        hg = jnp.dot(x, wg_ref[0].astype(BF16), preferred_element_type=F32)
        hu = jnp.dot(x, wu_ref[0].astype(BF16), preferred_element_type=F32)
        h = (_silu(hg) * hu).astype(BF16)
        acc_scr[...] += jnp.dot(h, wd_ref[0].astype(BF16), preferred_element_type=F32)

    @pl.when(active & (j == pl.num_programs(1) - 1))
    def _():
        for s in range(SUBLANES):
            yt_ref[pl.ds(s, tm, stride=SUBLANES), :] = acc_scr[:, s * LANES:(s + 1) * LANES]

    @pl.when(jnp.logical_not(active) & (j == 0))
    def _():
        yt_ref[...] = jnp.zeros_like(yt_ref)


def _ffn_grouped(xt, tile_expert, n_active, wg, wu, wd, tm):
    R = xt.shape[0] // SUBLANES
    E, D, FF = wg.shape
    tf = FFN_TF
    nf = FF // tf
    n_tiles = R // tm

    def row_map(i, j, te, na):
        return (jnp.minimum(i, na[0] - 1), 0)

    def ff_idx(i, j, na):
        return jnp.where(i < na[0], j, nf - 1)

    return pl.pallas_call(
        _ffn_grouped_kernel,
        out_shape=jax.ShapeDtypeStruct((R * SUBLANES, LANES), F32),
        grid_spec=pltpu.PrefetchScalarGridSpec(
            num_scalar_prefetch=2,
            grid=(n_tiles, nf),
            in_specs=[pl.BlockSpec((tm * SUBLANES, LANES), row_map),
                      pl.BlockSpec((1, D, tf), lambda i, j, te, na: (te[i], 0, ff_idx(i, j, na))),
                      pl.BlockSpec((1, D, tf), lambda i, j, te, na: (te[i], 0, ff_idx(i, j, na))),
                      pl.BlockSpec((1, tf, D), lambda i, j, te, na: (te[i], ff_idx(i, j, na), 0))],
            out_specs=pl.BlockSpec((tm * SUBLANES, LANES), lambda i, j, te, na: (i, 0)),
            scratch_shapes=[pltpu.VMEM((tm, D), BF16), pltpu.VMEM((tm, D), F32)]),
        compiler_params=_cparams("arbitrary", "arbitrary"),
        name="ffn_grouped",
    )(tile_expert, n_active, xt, wg, wu, wd)


def _row_tile(ref, r):
    return ref.at[pl.ds(pl.multiple_of(r, SUBLANES), SUBLANES), :]


def _dispatch_kernel(dest_ref, xt_ref, xs_in_hbm, xs_hbm, sem):
    del xs_in_hbm
    ch = dest_ref.shape[2] // 2

    def body(t, carry):
        src = _row_tile(xt_ref, t * SUBLANES)
        for k in range(2):
            pltpu.make_async_copy(src, _row_tile(xs_hbm, dest_ref[0, 0, 2 * t + k]), sem).start()
        return carry

    lax.fori_loop(0, ch, body, 0, unroll=8)
    for _ in range(2):
        pltpu.make_async_copy(xt_ref, xs_hbm.at[pl.ds(0, ch * SUBLANES), :], sem).wait()


def _dispatch_rows(xt, dest8, xs_init):
    N = xt.shape[0] // SUBLANES
    ch = min(DISPATCH_CHUNK, N)
    dest3 = dest8.reshape(N // ch, 1, 2 * ch)
    return pl.pallas_call(
        _dispatch_kernel,
        out_shape=jax.ShapeDtypeStruct(xs_init.shape, F32),
        grid=(N // ch,),
        in_specs=[pl.BlockSpec((1, 1, 2 * ch), lambda i: (i, 0, 0), memory_space=pltpu.SMEM),
                  pl.BlockSpec((ch * SUBLANES, LANES), lambda i: (i, 0)),
                  pl.BlockSpec(memory_space=pl.ANY)],
        out_specs=pl.BlockSpec(memory_space=pl.ANY),
        scratch_shapes=[pltpu.SemaphoreType.DMA(())],
        input_output_aliases={2: 0},
        compiler_params=_cparams("arbitrary"),
        name="moe_dispatch",
    )(dest3, xt, xs_init)


def _moe_ln_kernel(dest_ref, xt_ref, ys_hbm, route_ref, g_ref, b_ref, o_ref, ya_scr, yb_scr, sem):
    tm = o_ref.shape[0]

    def body(t, carry):
        pltpu.make_async_copy(_row_tile(ys_hbm, dest_ref[0, 0, 2 * t]),
                              _row_tile(ya_scr, t * SUBLANES), sem).start()
        pltpu.make_async_copy(_row_tile(ys_hbm, dest_ref[0, 0, 2 * t + 1]),
                              _row_tile(yb_scr, t * SUBLANES), sem).start()
        return carry

    lax.fori_loop(0, tm, body, 0, unroll=8)
    for scr in (ya_scr, yb_scr):
        pltpu.make_async_copy(ys_hbm.at[pl.ds(0, tm * SUBLANES), :], scr, sem).wait()

    w1 = jnp.broadcast_to(route_ref[:, 2:3], (tm, LANES))
    w2 = jnp.broadcast_to(route_ref[:, 3:4], (tm, LANES))
    parts = []
    for s in range(SUBLANES):
        rows = pl.ds(s, tm, stride=SUBLANES)
        parts.append(DEEPNORM_ALPHA * xt_ref[rows, :] + (w1 * ya_scr[rows, :] + w2 * yb_scr[rows, :]))
    d_model = SUBLANES * LANES
    mu = jnp.sum(sum(parts), axis=-1, keepdims=True) / d_model
    var = jnp.sum(sum((z - mu) * (z - mu) for z in parts), axis=-1, keepdims=True) / d_model
    rstd = lax.rsqrt(var + LN_EPS)
    for s in range(SUBLANES):
        cols = slice(s * LANES, (s + 1) * LANES)
        o_ref[:, cols] = (parts[s] - mu) * rstd * g_ref[:, cols] + b_ref[:, cols]


def _moe_combine_ln(xt, ys, dest8, route, g, b):
    M = route.shape[0]
    D = SUBLANES * LANES
    tm = min(LN_TM, M)
    row = lambda i: (i, 0)
    const = lambda i: (0, 0)
    dest3 = dest8.reshape(M // tm, 1, 2 * tm)
    return pl.pallas_call(
        _moe_ln_kernel,
        out_shape=jax.ShapeDtypeStruct((M, D), F32),
        grid=(M // tm,),
        in_specs=[pl.BlockSpec((1, 1, 2 * tm), lambda i: (i, 0, 0), memory_space=pltpu.SMEM),
                  pl.BlockSpec((tm * SUBLANES, LANES), row),
                  pl.BlockSpec(memory_space=pl.ANY),
                  pl.BlockSpec((tm, LANES), row),
                  pl.BlockSpec((1, D), const), pl.BlockSpec((1, D), const)],
        out_specs=pl.BlockSpec((tm, D), row),
        scratch_shapes=[pltpu.VMEM((tm * SUBLANES, LANES), F32), pltpu.VMEM((tm * SUBLANES, LANES), F32),
                        pltpu.SemaphoreType.DMA(())],
        compiler_params=_cparams("arbitrary"),
        name="moe_combine_ln",
    )(dest3, xt, ys, route, g.reshape(1, D), b.reshape(1, D))


def _gdn_kernel(alog_ref, dt_ref, pq_ref, pk_ref, pv_ref, hq_ref, hk_ref, hv_ref,
                cq_ref, ck_ref, cv_ref, br_ref, ar_ref, ng_ref, gate_ref, o_ref,
                state_scr, sq_scr, sk_scr, sv_scr, *, blk, hp):
    C = GDN_CHUNK
    Dh = GDN_HEAD_DIM
    nchunk = blk // C
    h0 = pl.program_id(1) * hp
    sb = pl.program_id(2)
    heads = range(hp)

    @pl.when(sb == 0)
    def _():
        state_scr[...] = jnp.zeros_like(state_scr)

    have_prev = (sb > 0).astype(F32)

    def conv_silu(cur_ref, halo_ref, w_ref, stage_scr):
        stage_scr[0:SUBLANES, :] = halo_ref[0] * have_prev
        stage_scr[SUBLANES:SUBLANES + blk, :] = cur_ref[0]
        out = None
        for j in range(GDN_CONV):
            off = SUBLANES - (GDN_CONV - 1) + j
            term = w_ref[j:j + 1, :] * stage_scr[off:off + blk, :]
            out = term if out is None else out + term
        return _silu(out)

    def split(t):
        return [t[:, hh * Dh:(hh + 1) * Dh] for hh in heads]

    def l2n(t):
        return t * lax.rsqrt(jnp.sum(t * t, axis=-1, keepdims=True) + RMS_EPS)

    q_h = [l2n(t) * (Dh ** -0.5) for t in split(conv_silu(pq_ref, hq_ref, cq_ref, sq_scr))]
    k_h = [l2n(t) for t in split(conv_silu(pk_ref, hk_ref, ck_ref, sk_scr))]
    v_h = split(conv_silu(pv_ref, hv_ref, cv_ref, sv_scr))

    lanes_row = _iota((hp, blk), 0)
    dt_rows = jnp.zeros((hp, blk), F32)
    alog_rows = jnp.zeros((hp, blk), F32)
    for hh in heads:
        dt_rows = jnp.where(lanes_row == hh, dt_ref[h0 + hh], dt_rows)
        alog_rows = jnp.where(lanes_row == hh, alog_ref[h0 + hh], alog_rows)
    beta_rows = jax.nn.sigmoid(br_ref[0, 0, 0])
    za = ar_ref[0, 0, 0] + dt_rows
    g_rows = -jnp.exp(alog_rows) * (jnp.maximum(za, 0.0) + jnp.log1p(jnp.exp(-jnp.abs(za))))
    ri = _iota((blk, blk), 0)
    ci = _iota((blk, blk), 1)
    same = (ri // C) == (ci // C)
    g8 = jnp.concatenate([g_rows, jnp.zeros((SUBLANES - hp, blk), F32)], axis=0) if hp < SUBLANES else g_rows
    gam_rows = jnp.dot(g8, (same & (ri <= ci)).astype(F32), precision=HIGHEST,
                       preferred_element_type=F32)
    gl_rows = jnp.dot(g8, same.astype(F32), precision=HIGHEST,
                      preferred_element_type=F32)
    beta = [_row_to_col(beta_rows[hh:hh + 1, :]) for hh in heads]
    gam = [_row_to_col(gam_rows[hh:hh + 1, :]) for hh in heads]
    gl = [_row_to_col(gl_rows[hh:hh + 1, :]) for hh in heads]
    eg = [jnp.exp(t) for t in gam]
    ekd = [jnp.exp(a - b) for a, b in zip(gl, gam)]

    incl = same & (ri >= ci)
    strict = same & (ri > ci)
    contract_last = (((1,), (1,)), ((), ()))
    decay = [jnp.where(incl, jnp.exp(jnp.where(incl, gam[hh] - gam_rows[hh:hh + 1, :], 0.0)), 0.0)
             for hh in heads]
    kb = [t.astype(BF16) for t in k_h]
    kk = [lax.dot_general(t, t, contract_last, preferred_element_type=F32) for t in kb]
    x_acc = [jnp.where(strict, -(beta[hh] * kk[hh] * decay[hh]), 0.0) for hh in heads]
    pw = x_acc
    for _ in range(int(np.log2(C)) - 1):
        pwb = [t.astype(BF16) for t in pw]
        pw = [jnp.dot(t, t, preferred_element_type=F32) for t in pwb]
        x_acc = [x + p + jnp.dot(p.astype(BF16), x.astype(BF16), preferred_element_type=F32)
                 for x, p in zip(x_acc, pw)]
    rhs = [jnp.concatenate([v_h[hh] * beta[hh], k_h[hh] * (beta[hh] * eg[hh])], axis=1) for hh in heads]
    sol = [r + jnp.dot(x.astype(BF16), r.astype(BF16), preferred_element_type=F32)
           for x, r in zip(x_acc, rhs)]
    u = [t[:, :Dh] for t in sol]
    w_b = [t[:, Dh:].astype(BF16) for t in sol]
    qk = [lax.dot_general(q_h[hh].astype(BF16), kb[hh], contract_last, preferred_element_type=F32) * decay[hh]
          for hh in heads]
    q_dec = [(q_h[hh] * eg[hh]).astype(BF16) for hh in heads]
    k_dec = [(k_h[hh] * ekd[hh]).astype(BF16) for hh in heads]

    state = [state_scr[hh] for hh in heads]
    v_new = [[] for _ in heads]
    o_inter = [[] for _ in heads]
    for c in range(nchunk):
        rows = slice(c * C, (c + 1) * C)
        sbf = [t.astype(BF16) for t in state]
        vn = [u[hh][rows] - jnp.dot(w_b[hh][rows], sbf[hh], preferred_element_type=F32) for hh in heads]
        for hh in heads:
            o_inter[hh].append(jnp.dot(q_dec[hh][rows], sbf[hh], preferred_element_type=F32))
            v_new[hh].append(vn[hh])
        state = [state[hh] * jnp.exp(gl[hh][c * C:c * C + 1, :])
                 + lax.dot_general(k_dec[hh][rows], vn[hh].astype(BF16), (((0,), (0,)), ((), ())),
                                   preferred_element_type=F32) for hh in heads]
    for hh in heads:
        state_scr[hh] = state[hh]
    o = [jnp.concatenate(o_inter[hh], axis=0)
         + jnp.dot(qk[hh].astype(BF16), jnp.concatenate(v_new[hh], axis=0).astype(BF16),
                   preferred_element_type=F32) for hh in heads]
    o = [t * lax.rsqrt(jnp.mean(t * t, axis=-1, keepdims=True) + RMS_EPS) * ng_ref[...] for t in o]
    o = o[0] if hp == 1 else jnp.concatenate(o, axis=1)
    o_ref[0] = (o * _silu(gate_ref[0])).astype(o_ref.dtype)


def _gated_deltanet(pre, ab_rows, gate, conv_w, a_log, dt_bias, norm_g, B, S):
    blk = min(GDN_BLK, S)
    H = N_HEADS_GDN
    hp = GDN_HEADS_PER_STEP
    hg = H // hp
    wide = hp * GDN_HEAD_DIM
    pre3 = pre.reshape(B, S, 3 * GDN_WIDTH)
    gate3 = gate.reshape(B, S, GDN_WIDTH)
    ab5 = ab_rows.reshape(B, 2, hg, hp, S)
    halo_blocks = blk // SUBLANES

    def cur(sec):
        return pl.BlockSpec((1, blk, wide), lambda b, h, s: (b, s, sec * hg + h))

    def halo(sec):
        return pl.BlockSpec((1, SUBLANES, wide),
                            lambda b, h, s: (b, jnp.maximum(s * halo_blocks - 1, 0), sec * hg + h))

    def cw(sec):
        return pl.BlockSpec((GDN_CONV, wide), lambda b, h, s: (0, sec * hg + h))

    smem = pl.BlockSpec(memory_space=pltpu.SMEM)
    out = pl.pallas_call(
        functools.partial(_gdn_kernel, blk=blk, hp=hp),
        out_shape=jax.ShapeDtypeStruct((B, S, GDN_WIDTH), BF16),
        grid=(B, hg, S // blk),
        in_specs=[smem, smem,
                  cur(0), cur(1), cur(2), halo(0), halo(1), halo(2),
                  cw(0), cw(1), cw(2),
                  pl.BlockSpec((1, 1, 1, hp, blk), lambda b, h, s: (b, 0, h, 0, s)),
                  pl.BlockSpec((1, 1, 1, hp, blk), lambda b, h, s: (b, 1, h, 0, s)),
                  pl.BlockSpec((1, LANES), lambda b, h, s: (0, 0)),
                  pl.BlockSpec((1, blk, wide), lambda b, h, s: (b, s, h))],
        out_specs=pl.BlockSpec((1, blk, wide), lambda b, h, s: (b, s, h)),
        scratch_shapes=[pltpu.VMEM((hp, GDN_HEAD_DIM, GDN_HEAD_DIM), F32)]
                       + [pltpu.VMEM((blk + SUBLANES, wide), F32) for _ in range(3)],
        compiler_params=_cparams("parallel", "parallel", "arbitrary"),
        name="gated_deltanet",
    )(a_log.astype(F32), dt_bias.astype(F32), pre3, pre3, pre3, pre3, pre3, pre3,
      conv_w, conv_w, conv_w, ab5, ab5, norm_g.reshape(1, LANES).astype(F32), gate3)
    return out.reshape(B * S, GDN_WIDTH)


def _pad_cols(w, width):
    return jnp.zeros((w.shape[0], width), w.dtype).at[:, :w.shape[1]].set(w)


def _attention_layer(x, xbf, B, S, w_in, forget_bias, w_out, ln_g, ln_b,
                     w_gate, w_up, w_down, ln2_g, ln2_b):
    W = ATT_WIDTH
    H = N_HEADS_ATT
    w_fox = jnp.concatenate([w_in[:, :W] * (HEAD_DIM ** -0.5 * LOG2E), w_in[:, W:3 * W]], axis=1).astype(BF16)
    w_f = _pad_cols(w_in[:, 3 * W:3 * W + H], LANES).astype(BF16)
    w_dil = w_in[:, 3 * W + H:].astype(BF16)
    qkv_fox = _proj(xbf, w_fox, BF16)
    f_logit = _proj(xbf, w_f, F32)
    qkv_dil = _proj(xbf, w_dil, F32)
    f_rows = f_logit[:, :H].reshape(B, S, H).transpose(0, 2, 1)
    f_cum = _forget_cumsum(f_rows, forget_bias)
    o_fox = _fox_attention(qkv_fox, f_cum, B, S)
    o_dil = _dilated_attention(qkv_dil, B, S)
    x1, x1bf = _attn_out_ln(o_fox, o_dil, w_out.astype(BF16), x, ln_g, ln_b)
    return _ffn_dense_ln(x1bf, x1, w_gate.astype(BF16), w_up.astype(BF16), w_down.astype(BF16),
                         ln2_g, ln2_b)


def _slot_indices(route, tm):
    N = route.shape[0]
    experts = route[:, 0:2].astype(jnp.int32).reshape(2 * N)
    onehot = (experts[:, None] == jnp.arange(N_EXPERTS, dtype=jnp.int32)[None, :]).astype(jnp.int32)
    csum = jnp.cumsum(onehot, axis=0)
    counts = csum[-1]
    padded = ((counts + tm - 1) // tm) * tm
    ends = jnp.cumsum(padded)
    starts = ends - padded
    dest = jnp.sum(onehot * (csum - 1 + starts[None, :]), axis=1).astype(jnp.int32)
    n_tiles = (2 * N) // tm + N_EXPERTS
    tile_start = jnp.arange(n_tiles, dtype=jnp.int32) * tm
    tile_expert = jnp.minimum(jnp.sum((tile_start[:, None] >= ends[None, :]).astype(jnp.int32), axis=1),
                              N_EXPERTS - 1).astype(jnp.int32)
    n_active = (ends[-1] // tm).astype(jnp.int32).reshape(1)
    return dest, tile_expert, n_active, n_tiles


def _deltanet_layer(x, xbf, B, S, w_in, conv_w, a_log, dt_bias, norm_g, w_out, ln_g, ln_b,
                    router, w_gate, w_up, w_down, ln2_g, ln2_b):
    N = B * S
    W = GDN_WIDTH
    H = N_HEADS_GDN
    w_qkv = w_in[:, :3 * W].astype(BF16)
    w_ab = _pad_cols(w_in[:, 3 * W:3 * W + 2 * H], LANES).astype(BF16)
    w_gt = w_in[:, 3 * W + 2 * H:].astype(BF16)
    pre = _proj(xbf, w_qkv, F32)
    ab = _proj(xbf, w_ab, F32)
    gate = _proj(xbf, w_gt, F32)
    ab_rows = ab[:, :2 * H].reshape(B, S, 2 * H).transpose(0, 2, 1).reshape(B, 2 * H, 1, S)
    o = _gated_deltanet(pre, ab_rows, gate, conv_w.astype(F32), a_log, dt_bias, norm_g, B, S)
    xt, route = _gdn_out_ln_route(o, w_out.astype(BF16), x, ln_g, ln_b, router)

    tm = min(FFN_TM, N)
    dest, tile_expert, n_active, n_tiles = _slot_indices(route, tm)
    dest8 = dest * SUBLANES
    xs_init = jnp.zeros((n_tiles * tm * SUBLANES, LANES), F32)
    xs = _dispatch_rows(xt, dest8, xs_init)
    ys = _ffn_grouped(xs, tile_expert, n_active,
                      w_gate, w_up, w_down, tm)
    return _moe_combine_ln(xt, ys, dest8, route, ln2_g, ln2_b)


def kernel(x, attn_w_in, fox_forget_bias, attn_w_out, ln_attn_g, ln_attn_b, ffn_w_gate, ffn_w_up,
           ffn_w_down, ln_ffn_g, ln_ffn_b, gdn_w_in, gdn_conv_w, gdn_a_log, gdn_dt_bias, gdn_norm_g,
           gdn_w_out, ln_gdn_g, ln_gdn_b, moe_router, moe_w_gate, moe_w_up, moe_w_down, ln_moe_g,
           ln_moe_b):
    B, S, D = x.shape
    x2 = x.reshape(B * S, D)
    x2bf = x2.astype(BF16)
    x2, x2bf = _attention_layer(x2, x2bf, B, S, attn_w_in[0], fox_forget_bias[0], attn_w_out[0],
                                ln_attn_g[0], ln_attn_b[0], ffn_w_gate[0], ffn_w_up[0], ffn_w_down[0],
                                ln_ffn_g[0], ln_ffn_b[0])
    y = _deltanet_layer(x2, x2bf, B, S, gdn_w_in[0], gdn_conv_w[0], gdn_a_log[0], gdn_dt_bias[0],
                        gdn_norm_g[0], gdn_w_out[0], ln_gdn_g[0], ln_gdn_b[0], moe_router[0],
                        moe_w_gate[0], moe_w_up[0], moe_w_down[0], ln_moe_g[0], ln_moe_b[0])
    return y.reshape(B, S, D)
```

```python
import functools

import numpy as np
import jax
import jax.numpy as jnp
from jax import lax
from jax.experimental import pallas as pl
from jax.experimental.pallas import tpu as pltpu

F32 = jnp.float32
BF16 = jnp.bfloat16
HIGHEST = lax.Precision.HIGHEST

LANES = 128
SUBLANES = 8
VMEM_LIMIT = 52 * 1024 * 1024

HEAD_DIM = 64
N_HEADS_ATT = 8
ATT_WIDTH = N_HEADS_ATT * HEAD_DIM
QUERY_BLOCK = 128
DIL_PATTERNS = ((128, 1), (512, 4), (2048, 16))
GDN_HEAD_DIM = 128
N_HEADS_GDN = 8
GDN_WIDTH = N_HEADS_GDN * GDN_HEAD_DIM
GDN_CONV = 4
GDN_CHUNK = 64
N_EXPERTS = 8
DEPTH = 2
DEEPNORM_ALPHA = (2.0 * DEPTH) ** 0.25
LN_EPS = 1e-5
RMS_EPS = 1e-6
NEG_INF = -1e30
LOG2E = 1.4426950408889634

PROJ_TM = 1024
PROJ_TN = 1664
FOX_TQ = 1024
FOX_GROUP = 512
FOX_STRIP = 32
DIL_GROUP = 4
LN_TM = 512
FFN_TM = 1024
FFN_TF = 896
GDN_BLK = 128
GDN_HEADS_PER_STEP = 8
DISPATCH_CHUNK = 512


def _cparams(*sem):
    return pltpu.CompilerParams(dimension_semantics=sem, vmem_limit_bytes=VMEM_LIMIT)


def _iota(shape, dim):
    return lax.broadcasted_iota(jnp.int32, shape, dim)


def _silu(x):
    return x * jax.nn.sigmoid(x)


def _row_to_col(row):
    n = row.shape[1]
    eye = _iota((LANES, LANES), 0) == _iota((LANES, LANES), 1)
    cols = []
    for c in range(n // LANES):
        seg = row[:, c * LANES:(c + 1) * LANES]
        cols.append(jnp.sum(jnp.where(eye, seg, 0.0), axis=1, keepdims=True))
    return cols[0] if len(cols) == 1 else jnp.concatenate(cols, axis=0)


def _layer_norm_rows(z, g, b):
    mu = jnp.mean(z, axis=-1, keepdims=True)
    zc = z - mu
    var = jnp.mean(zc * zc, axis=-1, keepdims=True)
    return zc * lax.rsqrt(var + LN_EPS) * g + b


def _proj_kernel(x_ref, w_ref, o_ref):
    o_ref[...] = jnp.dot(x_ref[...], w_ref[...], preferred_element_type=F32).astype(o_ref.dtype)


def _proj(x, w, out_dtype):
    M, K = x.shape
    C = w.shape[1]
    tm = min(PROJ_TM, M)
    tn = min(PROJ_TN, C)
    assert M % tm == 0 and C % tn == 0
    return pl.pallas_call(
        _proj_kernel,
        out_shape=jax.ShapeDtypeStruct((M, C), out_dtype),
        grid=(M // tm, C // tn),
        in_specs=[pl.BlockSpec((tm, K), lambda i, j: (i, 0)),
                  pl.BlockSpec((K, tn), lambda i, j: (0, j))],
        out_specs=pl.BlockSpec((tm, tn), lambda i, j: (i, j)),
        compiler_params=_cparams("parallel", "parallel"),
        name="proj",
    )(x, w)


def _forget_cumsum_kernel(f_ref, b_ref, o_ref):
    S = f_ref.shape[2]
    z = f_ref[0] + b_ref[...]
    lf = (jnp.minimum(z, 0.0) - jnp.log1p(jnp.exp(-jnp.abs(z)))) * LOG2E
    upper = (_iota((LANES, LANES), 0) <= _iota((LANES, LANES), 1)).astype(F32)
    carry = jnp.zeros((z.shape[0], 1), F32)
    for c in range(S // LANES):
        seg = jnp.dot(lf[:, c * LANES:(c + 1) * LANES], upper, precision=HIGHEST,
                      preferred_element_type=F32) + carry
        o_ref[0, :, c * LANES:(c + 1) * LANES] = seg
        carry = seg[:, LANES - 1:LANES]


def _forget_cumsum(f_rows, bias):
    B, H, S = f_rows.shape
    return pl.pallas_call(
        _forget_cumsum_kernel,
        out_shape=jax.ShapeDtypeStruct((B, H, S), F32),
        grid=(B,),
        in_specs=[pl.BlockSpec((1, H, S), lambda b: (b, 0, 0)),
                  pl.BlockSpec((H, 1), lambda b: (0, 0))],
        out_specs=pl.BlockSpec((1, H, S), lambda b: (b, 0, 0)),
        compiler_params=_cparams("parallel"),
        name="forget_cumsum",
    )(f_rows, bias.reshape(H, 1).astype(F32))


def _fox_kernel(q_ref, k_ref, v_ref, fq_ref, fk_ref, o_ref,
                s_scr, p_scr, m_scr, l_scr, a_scr, acc_scr, fq_scr, *, tq):
    G = min(FOX_GROUP, tq)
    R = FOX_STRIP
    i = pl.program_id(2)
    contract_last = (((1,), (1,)), ((), ()))
    units = [(hh, g) for hh in range(2) for g in range(tq // G)]
    lane_g = _iota((G, LANES), 1)
    qh = []
    for hh, g in units:
        qg = q_ref[0, g * G:(g + 1) * G, :]
        qh.append(jnp.where((lane_g < HEAD_DIM) == (hh == 0), qg, jnp.zeros_like(qg)))
    for hh in range(2):
        fq_scr[hh] = jnp.broadcast_to(_row_to_col(fq_ref[0, 0, 0, hh:hh + 1, :]), (tq, LANES))
    m_scr[...] = jnp.full(m_scr.shape, NEG_INF, F32)
    l_scr[...] = jnp.zeros(l_scr.shape, F32)
    acc_scr[...] = jnp.zeros(acc_scr.shape, F32)

    def kv_block(j, masked):
        kstart = pl.multiple_of(j * tq, tq)

        def ncols(g):
            return (g + 1) * G if masked else tq

        def scores(u):
            hh, g = units[u]
            n = ncols(g)
            kb = k_ref[0, pl.ds(kstart, n), :]
            s_scr[u, :, :n] = (lax.dot_general(qh[u], kb, contract_last, preferred_element_type=F32)
                               - fk_ref[0, 0, j, hh:hh + 1, :n])

        def strips(u):
            hh, g = units[u]
            n = ncols(g)
            for r in range(G // R):
                rows = slice(r * R, (r + 1) * R)
                grow = slice(g * G + r * R, g * G + (r + 1) * R)
                s = s_scr[u, rows, :n]
                if masked:
                    s = jnp.where(_iota((R, n), 1) <= _iota((R, n), 0) + (g * G + r * R), s, NEG_INF)
                fq = fq_scr[hh, grow, :]
                m_old = m_scr[hh, grow, :]
                m_new = jnp.maximum(m_old, jnp.max(s, axis=-1, keepdims=True) + fq)
                p = jnp.exp2(s - jnp.concatenate([m_new - fq] * (n // LANES), axis=1))
                alpha = jnp.exp2(m_old - m_new)
                a_scr[hh, grow, :] = alpha
                l_scr[hh, grow, :] = alpha * l_scr[hh, grow, :] + jnp.sum(p, axis=-1, keepdims=True)
                m_scr[hh, grow, :] = m_new
                p_scr[u, rows, :n] = p.astype(BF16)

        def values(u):
            hh, g = units[u]
            n = ncols(g)
            grow = slice(g * G, (g + 1) * G)
            vb = v_ref[0, pl.ds(kstart, n), :]
            acc_scr[hh, grow, :] = (a_scr[hh, grow, :] * acc_scr[hh, grow, :]
                                    + jnp.dot(p_scr[u, :, :n], vb, preferred_element_type=F32))

        scores(0)
        for u in range(1, len(units)):
            scores(u)
            strips(u - 1)
            values(u - 1)
        strips(len(units) - 1)
        values(len(units) - 1)

    def body(j, carry):
        kv_block(j, False)
        return carry

    lax.fori_loop(0, i, body, 0)
    kv_block(i, True)
    lane = _iota((tq, LANES), 1)
    o_ref[0] = jnp.where(lane < HEAD_DIM, acc_scr[0] / l_scr[0], acc_scr[1] / l_scr[1]).astype(o_ref.dtype)


def _fox_attention(qkv, f_cum, B, S):
    tq = min(FOX_TQ, S)
    nk = S // tq
    grp = min(FOX_GROUP, tq)
    n_units = 2 * (tq // grp)
    n_pairs = N_HEADS_ATT // 2
    qkv3 = qkv.reshape(B, S, 3 * ATT_WIDTH)
    f5 = f_cum.reshape(B, n_pairs, 2, nk, tq).transpose(0, 1, 3, 2, 4)
    out = pl.pallas_call(
        functools.partial(_fox_kernel, tq=tq),
        out_shape=jax.ShapeDtypeStruct((B, S, ATT_WIDTH), BF16),
        grid=(B, n_pairs, nk),
        in_specs=[pl.BlockSpec((1, tq, LANES), lambda b, p, i: (b, i, p)),
                  pl.BlockSpec((1, S, LANES), lambda b, p, i: (b, 0, n_pairs + p)),
                  pl.BlockSpec((1, S, LANES), lambda b, p, i: (b, 0, 2 * n_pairs + p)),
                  pl.BlockSpec((1, 1, 1, 2, tq), lambda b, p, i: (b, p, i, 0, 0)),
                  pl.BlockSpec((1, 1, nk, 2, tq), lambda b, p, i: (b, p, 0, 0, 0))],
        out_specs=pl.BlockSpec((1, tq, LANES), lambda b, p, i: (b, i, p)),
        scratch_shapes=[pltpu.VMEM((n_units, grp, tq), F32), pltpu.VMEM((n_units, grp, tq), BF16)]
                       + [pltpu.VMEM((2, tq, LANES), F32) for _ in range(5)],
        compiler_params=_cparams("parallel", "parallel", "arbitrary"),
        name="fox_attention",
    )(qkv3, qkv3, qkv3, f5, f5)
    return out.reshape(B * S, ATT_WIDTH)


def _dil_kernel(slope_ref, q_ref, k_ref, v_ref, o_ref, m0_scr, m1_scr, l_scr, acc_scr, *, S):
    QB = QUERY_BLOCK
    G = DIL_GROUP
    p_idx = pl.program_id(1)
    lane = _iota((QB, LANES), 1)
    head0 = lane < HEAD_DIM
    contract_last = (((1,), (1,)), ((), ()))
    qscale = HEAD_DIM ** -0.5 * LOG2E
    slopes = [slope_ref[2 * p_idx + hh] * LOG2E for hh in range(2)]
    delta_cur = _iota((QB, QB), 0) - _iota((QB, QB), 1)
    delta_two = QB + _iota((QB, 2 * QB), 0) - _iota((QB, 2 * QB), 1)

    def rows(start, dil):
        return pl.ds(start, QB, stride=dil) if dil > 1 else pl.ds(start, QB)

    def update(q0s, kp0s, dil, biases, first):
        loaded = []
        for g, q0 in enumerate(q0s):
            rq = rows(q0, dil)
            q = q_ref[0, rq, :] * qscale
            kc = k_ref[0, rq, :]
            vc = v_ref[0, rq, :]
            if kp0s is not None:
                rp = rows(kp0s[g], dil)
                kc = jnp.concatenate([k_ref[0, rp, :], kc], axis=0)
                vc = jnp.concatenate([v_ref[0, rp, :], vc], axis=0)
            old = None if first else (m0_scr[rq, :], m1_scr[rq, :], l_scr[rq, :], acc_scr[rq, :])
            loaded.append((rq, q, kc.astype(BF16), vc.astype(BF16), old))
        units = [(g, hh) for g in range(len(loaded)) for hh in range(2)]
        reps = loaded[0][2].shape[0] // LANES
        s_u = [lax.dot_general(jnp.where(head0 == (hh == 0), loaded[g][1], 0.0).astype(BF16), loaded[g][2],
                               contract_last, preferred_element_type=F32) - biases[hh] for g, hh in units]
        m_u = [jnp.broadcast_to(jnp.max(s, axis=-1, keepdims=True), (QB, LANES)) for s in s_u]
        if not first:
            m_u = [jnp.maximum(loaded[g][4][hh], m) for (g, hh), m in zip(units, m_u)]
            a_u = [jnp.exp2(loaded[g][4][hh] - m) for (g, hh), m in zip(units, m_u)]
        p_u = [jnp.exp2(s - jnp.concatenate([m] * reps, axis=1)) for s, m in zip(s_u, m_u)]
        ps_u = [jnp.sum(p, axis=-1, keepdims=True) for p in p_u]
        pv_u = [jnp.dot(p.astype(BF16), loaded[g][3], preferred_element_type=F32)
                for (g, hh), p in zip(units, p_u)]
        for g in range(len(loaded)):
            rq, old = loaded[g][0], loaded[g][4]
            l_new = jnp.where(head0, ps_u[2 * g], ps_u[2 * g + 1])
            acc_new = jnp.where(head0, pv_u[2 * g], pv_u[2 * g + 1])
            if not first:
                alpha = jnp.where(head0, a_u[2 * g], a_u[2 * g + 1])
                l_new = alpha * old[2] + l_new
                acc_new = alpha * old[3] + acc_new
            m0_scr[rq, :] = m_u[2 * g]
            m1_scr[rq, :] = m_u[2 * g + 1]
            l_scr[rq, :] = l_new
            acc_scr[rq, :] = acc_new

    for branch, (window, dil) in enumerate(sorted(DIL_PATTERNS, key=lambda wd: -wd[1])):
        span = window // dil
        assert span <= QB and (S // dil) % QB == 0
        nblk = S // dil // QB
        first = branch == 0

        def masked_bias(delta, hh, dil=dil, span=span):
            return jnp.where((delta >= 0) & (delta <= span), slopes[hh] * (delta * dil).astype(F32), -NEG_INF)

        bias_cur = [masked_bias(delta_cur, hh) for hh in range(2)]
        bias_two = [masked_bias(delta_two, hh) for hh in range(2)]

        ga = min(G, dil)

        def head_step(t, carry, dil=dil, ga=ga, bias_cur=bias_cur, first=first):
            update([t * ga + g for g in range(ga)], None, dil, bias_cur, first)
            return carry

        lax.fori_loop(0, dil // ga, head_step, 0)

        def starts(idx, dil=dil):
            q0 = (idx % dil) + (1 + idx // dil) * (QB * dil)
            return q0, q0 - QB * dil

        def tail_step(t, carry, base=0, count=G, dil=dil, bias_two=bias_two, first=first):
            pairs = [starts(base + t * count + g) for g in range(count)]
            update([a for a, _ in pairs], [b for _, b in pairs], dil, bias_two, first)
            return carry

        n_tail = dil * (nblk - 1)
        lax.fori_loop(0, n_tail // G, tail_step, 0)
        if n_tail % G:
            tail_step(0, 0, base=(n_tail // G) * G, count=n_tail % G)
    o_ref[0] = (acc_scr[...] / l_scr[...]).astype(o_ref.dtype)


def _dilated_attention(qkv, B, S):
    n_pairs = N_HEADS_ATT // 2
    slopes = jnp.asarray(2.0 ** (-8.0 * (np.arange(N_HEADS_ATT) + 1) / N_HEADS_ATT), dtype=F32)
    qkv3 = qkv.reshape(B, S, qkv.shape[1])
    out = pl.pallas_call(
        functools.partial(_dil_kernel, S=S),
        out_shape=jax.ShapeDtypeStruct((B, S, ATT_WIDTH), BF16),
        grid=(B, n_pairs),
        in_specs=[pl.BlockSpec(memory_space=pltpu.SMEM),
                  pl.BlockSpec((1, S, LANES), lambda b, p: (b, 0, p)),
                  pl.BlockSpec((1, S, LANES), lambda b, p: (b, 0, n_pairs + p)),
                  pl.BlockSpec((1, S, LANES), lambda b, p: (b, 0, 2 * n_pairs + p))],
        out_specs=pl.BlockSpec((1, S, LANES), lambda b, p: (b, 0, p)),
        scratch_shapes=[pltpu.VMEM((S, LANES), F32) for _ in range(4)],
        compiler_params=_cparams("parallel", "parallel"),
        name="dilated_attention",
    )(slopes, qkv3, qkv3, qkv3)
    return out.reshape(B * S, ATT_WIDTH)


def _attn_out_kernel(a0_ref, a1_ref, w_ref, x_ref, g_ref, b_ref, y_ref, ybf_ref):
    k0 = a0_ref.shape[1]
    mix = jnp.dot(a0_ref[...], w_ref[:k0, :], preferred_element_type=F32)
    mix = mix + jnp.dot(a1_ref[...], w_ref[k0:, :], preferred_element_type=F32)
    y = _layer_norm_rows(DEEPNORM_ALPHA * x_ref[...] + mix, g_ref[...], b_ref[...])
    y_ref[...] = y
    ybf_ref[...] = y.astype(BF16)


def _attn_out_ln(a0, a1, w, x, g, b):
    M, D = x.shape
    tm = min(LN_TM, M)
    row = lambda i: (i, 0)
    const = lambda i: (0, 0)
    return pl.pallas_call(
        _attn_out_kernel,
        out_shape=(jax.ShapeDtypeStruct((M, D), F32), jax.ShapeDtypeStruct((M, D), BF16)),
        grid=(M // tm,),
        in_specs=[pl.BlockSpec((tm, a0.shape[1]), row), pl.BlockSpec((tm, a1.shape[1]), row),
                  pl.BlockSpec(w.shape, const), pl.BlockSpec((tm, D), row),
                  pl.BlockSpec((1, D), const), pl.BlockSpec((1, D), const)],
        out_specs=(pl.BlockSpec((tm, D), row), pl.BlockSpec((tm, D), row)),
        compiler_params=_cparams("parallel"),
        name="attn_out_ln",
    )(a0, a1, w, x, g.reshape(1, D), b.reshape(1, D))


def _gdn_out_kernel(a_ref, w_ref, x_ref, g_ref, b_ref, r_ref, yt_ref, route_ref):
    tm = x_ref.shape[0]
    mix = jnp.dot(a_ref[...], w_ref[...], preferred_element_type=F32)
    y = _layer_norm_rows(DEEPNORM_ALPHA * x_ref[...] + mix, g_ref[...], b_ref[...])
    for s in range(SUBLANES):
        yt_ref[pl.ds(s, tm, stride=SUBLANES), :] = y[:, s * LANES:(s + 1) * LANES]
    y_hi = y.astype(BF16)
    y_lo = (y - y_hi.astype(F32)).astype(BF16)
    both = jnp.dot(y_hi, r_ref[...], preferred_element_type=F32)
    logits = both[:, :LANES] + (both[:, LANES:] + jnp.dot(y_lo, r_ref[:, :LANES], preferred_element_type=F32))
    lane = _iota((tm, LANES), 1)
    logits = jnp.where(lane < N_EXPERTS, logits, -jnp.inf)
    m1 = jnp.max(logits, axis=-1, keepdims=True)
    i1 = jnp.min(jnp.where(logits == m1, lane, LANES), axis=-1, keepdims=True)
    rest = jnp.where(lane == i1, -jnp.inf, logits)
    m2 = jnp.max(rest, axis=-1, keepdims=True)
    i2 = jnp.min(jnp.where(rest == m2, lane, LANES), axis=-1, keepdims=True)
    e2 = jnp.exp(m2 - m1)
    w1 = 1.0 / (1.0 + e2)
    w2 = e2 / (1.0 + e2)
    route = jnp.where(lane == 0, i1.astype(F32),
                      jnp.where(lane == 1, i2.astype(F32),
                                jnp.where(lane == 2, w1, jnp.where(lane == 3, w2, 0.0))))
    route_ref[...] = route


def _gdn_out_ln_route(a, w, x, g, b, router):
    M, D = x.shape
    tm = min(LN_TM, M)
    row = lambda i: (i, 0)
    const = lambda i: (0, 0)
    r_pad = jnp.zeros((D, LANES), F32).at[:, :N_EXPERTS].set(router.astype(F32))
    r_hi = r_pad.astype(BF16)
    r_lo = (r_pad - r_hi.astype(F32)).astype(BF16)
    r_pad = jnp.concatenate([r_hi, r_lo], axis=1)
    return pl.pallas_call(
        _gdn_out_kernel,
        out_shape=(jax.ShapeDtypeStruct((M * SUBLANES, LANES), F32),
                   jax.ShapeDtypeStruct((M, LANES), F32)),
        grid=(M // tm,),
        in_specs=[pl.BlockSpec((tm, a.shape[1]), row), pl.BlockSpec(w.shape, const),
                  pl.BlockSpec((tm, D), row), pl.BlockSpec((1, D), const),
                  pl.BlockSpec((1, D), const), pl.BlockSpec((D, 2 * LANES), const)],
        out_specs=(pl.BlockSpec((tm * SUBLANES, LANES), row), pl.BlockSpec((tm, LANES), row)),
        compiler_params=_cparams("parallel"),
        name="gdn_out_ln_route",
    )(a, w, x, g.reshape(1, D), b.reshape(1, D), r_pad)


def _ffn_dense_kernel(xbf_ref, wg_ref, wu_ref, wd_ref, x_ref, g_ref, b_ref, y_ref, ybf_ref, acc_scr):
    j = pl.program_id(1)

    @pl.when(j == 0)
    def _():
        acc_scr[...] = jnp.zeros_like(acc_scr)

    x = xbf_ref[...]
    hg = jnp.dot(x, wg_ref[...], preferred_element_type=F32)
    hu = jnp.dot(x, wu_ref[...], preferred_element_type=F32)
    h = (_silu(hg) * hu).astype(BF16)
    acc_scr[...] += jnp.dot(h, wd_ref[...], preferred_element_type=F32)

    @pl.when(j == pl.num_programs(1) - 1)
    def _():
        y = _layer_norm_rows(DEEPNORM_ALPHA * x_ref[...] + acc_scr[...], g_ref[...], b_ref[...])
        y_ref[...] = y
        ybf_ref[...] = y.astype(BF16)


def _ffn_dense_ln(xbf, x, wg, wu, wd, g, b):
    M, D = x.shape
    FF = wg.shape[1]
    tm = min(FFN_TM, M)
    tf = FFN_TF
    assert FF % tf == 0
    row = lambda i, j: (i, 0)
    const = lambda i, j: (0, 0)
    return pl.pallas_call(
        _ffn_dense_kernel,
        out_shape=(jax.ShapeDtypeStruct((M, D), F32), jax.ShapeDtypeStruct((M, D), BF16)),
        grid=(M // tm, FF // tf),
        in_specs=[pl.BlockSpec((tm, D), row),
                  pl.BlockSpec((D, tf), lambda i, j: (0, j)),
                  pl.BlockSpec((D, tf), lambda i, j: (0, j)),
                  pl.BlockSpec((tf, D), lambda i, j: (j, 0)),
                  pl.BlockSpec((tm, D), row),
                  pl.BlockSpec((1, D), const), pl.BlockSpec((1, D), const)],
        out_specs=(pl.BlockSpec((tm, D), row), pl.BlockSpec((tm, D), row)),
        scratch_shapes=[pltpu.VMEM((tm, D), F32)],
        compiler_params=_cparams("parallel", "arbitrary"),
        name="ffn_dense_ln",
    )(xbf, wg, wu, wd, x, g.reshape(1, D), b.reshape(1, D))


def _ffn_grouped_kernel(te_ref, na_ref, xt_ref, wg_ref, wu_ref, wd_ref, yt_ref, xbf_scr, acc_scr):
    i = pl.program_id(0)
    j = pl.program_id(1)
    tm = acc_scr.shape[0]
    active = i < na_ref[0]

    @pl.when(active & (j == 0))
    def _():
        acc_scr[...] = jnp.zeros_like(acc_scr)
        for s in range(SUBLANES):
            xbf_scr[:, s * LANES:(s + 1) * LANES] = xt_ref[pl.ds(s, tm, stride=SUBLANES), :].astype(BF16)

    @pl.when(active)
    def _():
        x = xbf_scr[...]
        hg = jnp.dot(x, wg_ref[0].astype(BF16), preferred_element_type=F32)
        hu = jnp.dot(x, wu_ref[0].astype(BF16), preferred_element_type=F32)
        h = (_silu(hg) * hu).astype(BF16)
        acc_scr[...] += jnp.dot(h, wd_ref[0].astype(BF16), preferred_element_type=F32)

    @pl.when(active & (j == pl.num_programs(1) - 1))
    def _():
        for s in range(SUBLANES):
            yt_ref[pl.ds(s, tm, stride=SUBLANES), :] = acc_scr[:, s * LANES:(s + 1) * LANES]

    @pl.when(jnp.logical_not(active) & (j == 0))
    def _():
        yt_ref[...] = jnp.zeros_like(yt_ref)


def _ffn_grouped(xt, tile_expert, n_active, wg, wu, wd, tm):
    R = xt.shape[0] // SUBLANES
    E, D, FF = wg.shape
    tf = FFN_TF
    nf = FF // tf
    n_tiles = R // tm

    def row_map(i, j, te, na):
        return (jnp.minimum(i, na[0] - 1), 0)

    def ff_idx(i, j, na):
        return jnp.where(i < na[0], j, nf - 1)

    return pl.pallas_call(
        _ffn_grouped_kernel,
        out_shape=jax.ShapeDtypeStruct((R * SUBLANES, LANES), F32),
        grid_spec=pltpu.PrefetchScalarGridSpec(
            num_scalar_prefetch=2,
            grid=(n_tiles, nf),
            in_specs=[pl.BlockSpec((tm * SUBLANES, LANES), row_map),
                      pl.BlockSpec((1, D, tf), lambda i, j, te, na: (te[i], 0, ff_idx(i, j, na))),
                      pl.BlockSpec((1, D, tf), lambda i, j, te, na: (te[i], 0, ff_idx(i, j, na))),
                      pl.BlockSpec((1, tf, D), lambda i, j, te, na: (te[i], ff_idx(i, j, na), 0))],
            out_specs=pl.BlockSpec((tm * SUBLANES, LANES), lambda i, j, te, na: (i, 0)),
            scratch_shapes=[pltpu.VMEM((tm, D), BF16), pltpu.VMEM((tm, D), F32)]),
        compiler_params=_cparams("arbitrary", "arbitrary"),
        name="ffn_grouped",
    )(tile_expert, n_active, xt, wg, wu, wd)


def _row_tile(ref, r):
    return ref.at[pl.ds(pl.multiple_of(r, SUBLANES), SUBLANES), :]


def _dispatch_kernel(dest_ref, xt_ref, xs_in_hbm, xs_hbm, sem):
    del xs_in_hbm
    ch = dest_ref.shape[2] // 2

    def body(t, carry):
        src = _row_tile(xt_ref, t * SUBLANES)
        for k in range(2):
            pltpu.make_async_copy(src, _row_tile(xs_hbm, dest_ref[0, 0, 2 * t + k]), sem).start()
        return carry

    lax.fori_loop(0, ch, body, 0, unroll=8)
    for _ in range(2):
        pltpu.make_async_copy(xt_ref, xs_hbm.at[pl.ds(0, ch * SUBLANES), :], sem).wait()


def _dispatch_rows(xt, dest8, xs_init):
    N = xt.shape[0] // SUBLANES
    ch = min(DISPATCH_CHUNK, N)
    dest3 = dest8.reshape(N // ch, 1, 2 * ch)
    return pl.pallas_call(
        _dispatch_kernel,
        out_shape=jax.ShapeDtypeStruct(xs_init.shape, F32),
        grid=(N // ch,),
        in_specs=[pl.BlockSpec((1, 1, 2 * ch), lambda i: (i, 0, 0), memory_space=pltpu.SMEM),
                  pl.BlockSpec((ch * SUBLANES, LANES), lambda i: (i, 0)),
                  pl.BlockSpec(memory_space=pl.ANY)],
        out_specs=pl.BlockSpec(memory_space=pl.ANY),
        scratch_shapes=[pltpu.SemaphoreType.DMA(())],
        input_output_aliases={2: 0},
        compiler_params=_cparams("arbitrary"),
        name="moe_dispatch",
    )(dest3, xt, xs_init)


def _moe_ln_kernel(dest_ref, xt_ref, ys_hbm, route_ref, g_ref, b_ref, o_ref, ya_scr, yb_scr, sem):
    tm = o_ref.shape[0]

    def body(t, carry):
        pltpu.make_async_copy(_row_tile(ys_hbm, dest_ref[0, 0, 2 * t]),
                              _row_tile(ya_scr, t * SUBLANES), sem).start()
        pltpu.make_async_copy(_row_tile(ys_hbm, dest_ref[0, 0, 2 * t + 1]),
                              _row_tile(yb_scr, t * SUBLANES), sem).start()
        return carry

    lax.fori_loop(0, tm, body, 0, unroll=8)
    for scr in (ya_scr, yb_scr):
        pltpu.make_async_copy(ys_hbm.at[pl.ds(0, tm * SUBLANES), :], scr, sem).wait()

    w1 = jnp.broadcast_to(route_ref[:, 2:3], (tm, LANES))
    w2 = jnp.broadcast_to(route_ref[:, 3:4], (tm, LANES))
    parts = []
    for s in range(SUBLANES):
        rows = pl.ds(s, tm, stride=SUBLANES)
        parts.append(DEEPNORM_ALPHA * xt_ref[rows, :] + (w1 * ya_scr[rows, :] + w2 * yb_scr[rows, :]))
    d_model = SUBLANES * LANES
    mu = jnp.sum(sum(parts), axis=-1, keepdims=True) / d_model
    var = jnp.sum(sum((z - mu) * (z - mu) for z in parts), axis=-1, keepdims=True) / d_model
    rstd = lax.rsqrt(var + LN_EPS)
    for s in range(SUBLANES):
        cols = slice(s * LANES, (s + 1) * LANES)
        o_ref[:, cols] = (parts[s] - mu) * rstd * g_ref[:, cols] + b_ref[:, cols]


def _moe_combine_ln(xt, ys, dest8, route, g, b):
    M = route.shape[0]
    D = SUBLANES * LANES
    tm = min(LN_TM, M)
    row = lambda i: (i, 0)
    const = lambda i: (0, 0)
    dest3 = dest8.reshape(M // tm, 1, 2 * tm)
    return pl.pallas_call(
        _moe_ln_kernel,
        out_shape=jax.ShapeDtypeStruct((M, D), F32),
        grid=(M // tm,),
        in_specs=[pl.BlockSpec((1, 1, 2 * tm), lambda i: (i, 0, 0), memory_space=pltpu.SMEM),
                  pl.BlockSpec((tm * SUBLANES, LANES), row),
                  pl.BlockSpec(memory_space=pl.ANY),
                  pl.BlockSpec((tm, LANES), row),
                  pl.BlockSpec((1, D), const), pl.BlockSpec((1, D), const)],
        out_specs=pl.BlockSpec((tm, D), row),
        scratch_shapes=[pltpu.VMEM((tm * SUBLANES, LANES), F32), pltpu.VMEM((tm * SUBLANES, LANES), F32),
                        pltpu.SemaphoreType.DMA(())],
        compiler_params=_cparams("arbitrary"),
        name="moe_combine_ln",
    )(dest3, xt, ys, route, g.reshape(1, D), b.reshape(1, D))


def _gdn_kernel(alog_ref, dt_ref, pq_ref, pk_ref, pv_ref, hq_ref, hk_ref, hv_ref,
                cq_ref, ck_ref, cv_ref, br_ref, ar_ref, ng_ref, gate_ref, o_ref,
                state_scr, sq_scr, sk_scr, sv_scr, *, blk, hp):
    C = GDN_CHUNK
    Dh = GDN_HEAD_DIM
    nchunk = blk // C
    h0 = pl.program_id(1) * hp
    sb = pl.program_id(2)
    heads = range(hp)

    @pl.when(sb == 0)
    def _():
        state_scr[...] = jnp.zeros_like(state_scr)

    have_prev = (sb > 0).astype(F32)

    def conv_silu(cur_ref, halo_ref, w_ref, stage_scr):
        outs = []
        for c in range(stage_scr.shape[0]):
            lanes = slice(c * LANES, (c + 1) * LANES)
            stage_scr[c, pl.ds(0, SUBLANES, stride=2), :] = halo_ref[0, :, lanes] * have_prev
            stage_scr[c, pl.ds(2 * SUBLANES, blk, stride=2), :] = cur_ref[0, :, lanes]
            out = None
            for j in range(GDN_CONV):
                off = 2 * (SUBLANES - (GDN_CONV - 1) + j)
                term = w_ref[j:j + 1, lanes] * stage_scr[c, pl.ds(off, blk, stride=2), :]
                out = term if out is None else out + term
            outs.append(out)
        return _silu(jnp.concatenate(outs, axis=1))

    def split(t):
        return [t[:, hh * Dh:(hh + 1) * Dh] for hh in heads]

    def l2n(t):
        return t * lax.rsqrt(jnp.sum(t * t, axis=-1, keepdims=True) + RMS_EPS)

    q_h = [l2n(t) * (Dh ** -0.5) for t in split(conv_silu(pq_ref, hq_ref, cq_ref, sq_scr))]
    k_h = [l2n(t) for t in split(conv_silu(pk_ref, hk_ref, ck_ref, sk_scr))]
    v_h = split(conv_silu(pv_ref, hv_ref, cv_ref, sv_scr))

    lanes_row = _iota((hp, blk), 0)
    dt_rows = jnp.zeros((hp, blk), F32)
    alog_rows = jnp.zeros((hp, blk), F32)
    for hh in heads:
        dt_rows = jnp.where(lanes_row == hh, dt_ref[h0 + hh], dt_rows)
        alog_rows = jnp.where(lanes_row == hh, alog_ref[h0 + hh], alog_rows)
    beta_rows = jax.nn.sigmoid(br_ref[0, 0, 0])
    za = ar_ref[0, 0, 0] + dt_rows
    g_rows = -jnp.exp(alog_rows) * (jnp.maximum(za, 0.0) + jnp.log1p(jnp.exp(-jnp.abs(za))))
    ri = _iota((blk, blk), 0)
    ci = _iota((blk, blk), 1)
    same = (ri // C) == (ci // C)
    g8 = jnp.concatenate([g_rows, jnp.zeros((SUBLANES - hp, blk), F32)], axis=0) if hp < SUBLANES else g_rows
    gam_rows = jnp.dot(g8, (same & (ri <= ci)).astype(F32), precision=HIGHEST,
                       preferred_element_type=F32)
    gl_rows = jnp.dot(g8, same.astype(F32), precision=HIGHEST,
                      preferred_element_type=F32)
    beta = [_row_to_col(beta_rows[hh:hh + 1, :]) for hh in heads]
    gam = [_row_to_col(gam_rows[hh:hh + 1, :]) for hh in heads]
    gl = [_row_to_col(gl_rows[hh:hh + 1, :]) for hh in heads]
    eg = [jnp.exp(t) for t in gam]
    ekd = [jnp.exp(a - b) for a, b in zip(gl, gam)]

    incl = same & (ri >= ci)
    strict = same & (ri > ci)
    contract_last = (((1,), (1,)), ((), ()))
    decay = [jnp.where(incl, jnp.exp(jnp.where(incl, gam[hh] - gam_rows[hh:hh + 1, :], 0.0)), 0.0)
             for hh in heads]
    kb = [t.astype(BF16) for t in k_h]
    kk = [lax.dot_general(t, t, contract_last, preferred_element_type=F32) for t in kb]
    x_acc = [jnp.where(strict, -(beta[hh] * kk[hh] * decay[hh]), 0.0) for hh in heads]
    pw = x_acc
    for _ in range(int(np.log2(C)) - 1):
        pwb = [t.astype(BF16) for t in pw]
        pw = [jnp.dot(t, t, preferred_element_type=F32) for t in pwb]
        x_acc = [x + p + jnp.dot(p.astype(BF16), x.astype(BF16), preferred_element_type=F32)
                 for x, p in zip(x_acc, pw)]
    rhs = [jnp.concatenate([v_h[hh] * beta[hh], k_h[hh] * (beta[hh] * eg[hh])], axis=1) for hh in heads]
    sol = [r + jnp.dot(x.astype(BF16), r.astype(BF16), preferred_element_type=F32)
           for x, r in zip(x_acc, rhs)]
    u = [t[:, :Dh] for t in sol]
    w_b = [t[:, Dh:].astype(BF16) for t in sol]
    qk = [lax.dot_general(q_h[hh].astype(BF16), kb[hh], contract_last, preferred_element_type=F32) * decay[hh]
          for hh in heads]
    q_dec = [(q_h[hh] * eg[hh]).astype(BF16) for hh in heads]
    k_dec = [(k_h[hh] * ekd[hh]).astype(BF16) for hh in heads]

    state = [state_scr[hh] for hh in heads]
    v_new = [[] for _ in heads]
    o_inter = [[] for _ in heads]
    for c in range(nchunk):
        rows = slice(c * C, (c + 1) * C)
        sbf = [t.astype(BF16) for t in state]
        vn = [u[hh][rows] - jnp.dot(w_b[hh][rows], sbf[hh], preferred_element_type=F32) for hh in heads]
        for hh in heads:
            o_inter[hh].append(jnp.dot(q_dec[hh][rows], sbf[hh], preferred_element_type=F32))
            v_new[hh].append(vn[hh])
        state = [state[hh] * jnp.exp(gl[hh][c * C:c * C + 1, :])
                 + lax.dot_general(k_dec[hh][rows], vn[hh].astype(BF16), (((0,), (0,)), ((), ())),
                                   preferred_element_type=F32) for hh in heads]
    for hh in heads:
        state_scr[hh] = state[hh]
    o = [jnp.concatenate(o_inter[hh], axis=0)
         + jnp.dot(qk[hh].astype(BF16), jnp.concatenate(v_new[hh], axis=0).astype(BF16),
                   preferred_element_type=F32) for hh in heads]
    o = [t * lax.rsqrt(jnp.mean(t * t, axis=-1, keepdims=True) + RMS_EPS) * ng_ref[...] for t in o]
    o = o[0] if hp == 1 else jnp.concatenate(o, axis=1)
    o_ref[0] = (o * _silu(gate_ref[0])).astype(o_ref.dtype)


def _gated_deltanet(pre, ab_rows, gate, conv_w, a_log, dt_bias, norm_g, B, S):
    blk = min(GDN_BLK, S)
    H = N_HEADS_GDN
    hp = GDN_HEADS_PER_STEP
    hg = H // hp
    wide = hp * GDN_HEAD_DIM
    pre3 = pre.reshape(B, S, pre.shape[1])
    gate3 = gate.reshape(B, S, GDN_WIDTH)
    ab5 = ab_rows.reshape(B, 2, hg, hp, S)
    halo_blocks = blk // SUBLANES

    def cur(sec):
        return pl.BlockSpec((1, blk, wide), lambda b, h, s: (b, s, sec * hg + h))

    def halo(sec):
        return pl.BlockSpec((1, SUBLANES, wide),
                            lambda b, h, s: (b, jnp.maximum(s * halo_blocks - 1, 0), sec * hg + h))

    def cw(sec):
        return pl.BlockSpec((GDN_CONV, wide), lambda b, h, s: (0, sec * hg + h))

    smem = pl.BlockSpec(memory_space=pltpu.SMEM)
    out = pl.pallas_call(
        functools.partial(_gdn_kernel, blk=blk, hp=hp),
        out_shape=jax.ShapeDtypeStruct((B, S, GDN_WIDTH), BF16),
        grid=(B, hg, S // blk),
        in_specs=[smem, smem,
                  cur(0), cur(1), cur(2), halo(0), halo(1), halo(2),
                  cw(0), cw(1), cw(2),
                  pl.BlockSpec((1, 1, 1, hp, blk), lambda b, h, s: (b, 0, h, 0, s)),
                  pl.BlockSpec((1, 1, 1, hp, blk), lambda b, h, s: (b, 1, h, 0, s)),
                  pl.BlockSpec((1, LANES), lambda b, h, s: (0, 0)),
                  pl.BlockSpec((1, blk, wide), lambda b, h, s: (b, s, h))],
        out_specs=pl.BlockSpec((1, blk, wide), lambda b, h, s: (b, s, h)),
        scratch_shapes=[pltpu.VMEM((hp, GDN_HEAD_DIM, GDN_HEAD_DIM), F32)]
                       + [pltpu.VMEM((wide // LANES, 2 * (blk + SUBLANES), LANES), F32) for _ in range(3)],
        compiler_params=_cparams("parallel", "parallel", "arbitrary"),
        name="gated_deltanet",
    )(a_log.astype(F32), dt_bias.astype(F32), pre3, pre3, pre3, pre3, pre3, pre3,
      conv_w, conv_w, conv_w, ab5, ab5, norm_g.reshape(1, LANES).astype(F32), gate3)
    return out.reshape(B * S, GDN_WIDTH)


def _pad_cols(w, width):
    return jnp.zeros((w.shape[0], width), w.dtype).at[:, :w.shape[1]].set(w)


def _attention_layer(x, xbf, B, S, w_in, forget_bias, w_out, ln_g, ln_b,
                     w_gate, w_up, w_down, ln2_g, ln2_b):
    W = ATT_WIDTH
    H = N_HEADS_ATT
    w_fox = jnp.concatenate([w_in[:, :W] * (HEAD_DIM ** -0.5 * LOG2E), w_in[:, W:3 * W]], axis=1).astype(BF16)
    w_dil = jnp.concatenate([w_in[:, 3 * W + H:], _pad_cols(w_in[:, 3 * W:3 * W + H], LANES)], axis=1).astype(BF16)
    qkv_fox = _proj(xbf, w_fox, BF16)
    qkv_dil = _proj(xbf, w_dil, F32)
    f_rows = qkv_dil[:, 3 * W:3 * W + H].reshape(B, S, H).transpose(0, 2, 1)
    f_cum = _forget_cumsum(f_rows, forget_bias)
    o_fox = _fox_attention(qkv_fox, f_cum, B, S)
    o_dil = _dilated_attention(qkv_dil, B, S)
    x1, x1bf = _attn_out_ln(o_fox, o_dil, w_out.astype(BF16), x, ln_g, ln_b)
    return _ffn_dense_ln(x1bf, x1, w_gate.astype(BF16), w_up.astype(BF16), w_down.astype(BF16),
                         ln2_g, ln2_b)


def _slot_indices(route, tm):
    N = route.shape[0]
    experts = route[:, 0:2].astype(jnp.int32).reshape(2 * N)
    onehot = (experts[:, None] == jnp.arange(N_EXPERTS, dtype=jnp.int32)[None, :]).astype(jnp.int32)
    csum = jnp.cumsum(onehot, axis=0)
    counts = csum[-1]
    padded = ((counts + tm - 1) // tm) * tm
    ends = jnp.cumsum(padded)
    starts = ends - padded
    dest = jnp.sum(onehot * (csum - 1 + starts[None, :]), axis=1).astype(jnp.int32)
    n_tiles = (2 * N) // tm + N_EXPERTS
    tile_start = jnp.arange(n_tiles, dtype=jnp.int32) * tm
    tile_expert = jnp.minimum(jnp.sum((tile_start[:, None] >= ends[None, :]).astype(jnp.int32), axis=1),
                              N_EXPERTS - 1).astype(jnp.int32)
    n_active = (ends[-1] // tm).astype(jnp.int32).reshape(1)
    return dest, tile_expert, n_active, n_tiles


def _deltanet_layer(x, xbf, B, S, w_in, conv_w, a_log, dt_bias, norm_g, w_out, ln_g, ln_b,
                    router, w_gate, w_up, w_down, ln2_g, ln2_b):
    N = B * S
    W = GDN_WIDTH
    H = N_HEADS_GDN
    w_qkv = jnp.concatenate([w_in[:, :3 * W], _pad_cols(w_in[:, 3 * W:3 * W + 2 * H], 2 * LANES)], axis=1).astype(BF16)
    w_gt = w_in[:, 3 * W + 2 * H:].astype(BF16)
    pre = _proj(xbf, w_qkv, F32)
    gate = _proj(xbf, w_gt, F32)
    ab_rows = pre[:, 3 * W:3 * W + 2 * H].reshape(B, S, 2 * H).transpose(0, 2, 1).reshape(B, 2 * H, 1, S)
    o = _gated_deltanet(pre, ab_rows, gate, conv_w.astype(F32), a_log, dt_bias, norm_g, B, S)
    xt, route = _gdn_out_ln_route(o, w_out.astype(BF16), x, ln_g, ln_b, router)

    tm = min(FFN_TM, N)
    dest, tile_expert, n_active, n_tiles = _slot_indices(route, tm)
    dest8 = dest * SUBLANES
    xs_init = jnp.zeros((n_tiles * tm * SUBLANES, LANES), F32)
    xs = _dispatch_rows(xt, dest8, xs_init)
    ys = _ffn_grouped(xs, tile_expert, n_active,
                      w_gate, w_up, w_down, tm)
    return _moe_combine_ln(xt, ys, dest8, route, ln2_g, ln2_b)


def kernel(x, attn_w_in, fox_forget_bias, attn_w_out, ln_attn_g, ln_attn_b, ffn_w_gate, ffn_w_up,
           ffn_w_down, ln_ffn_g, ln_ffn_b, gdn_w_in, gdn_conv_w, gdn_a_log, gdn_dt_bias, gdn_norm_g,
           gdn_w_out, ln_gdn_g, ln_gdn_b, moe_router, moe_w_gate, moe_w_up, moe_w_down, ln_moe_g,
           ln_moe_b):
    B, S, D = x.shape
    x2 = x.reshape(B * S, D)
    x2bf = x2.astype(BF16)
    x2, x2bf = _attention_layer(x2, x2bf, B, S, attn_w_in[0], fox_forget_bias[0], attn_w_out[0],
                                ln_attn_g[0], ln_attn_b[0], ffn_w_gate[0], ffn_w_up[0], ffn_w_down[0],
                                ln_ffn_g[0], ln_ffn_b[0])
    y = _deltanet_layer(x2, x2bf, B, S, gdn_w_in[0], gdn_conv_w[0], gdn_a_log[0], gdn_dt_bias[0],
                        gdn_norm_g[0], gdn_w_out[0], ln_gdn_g[0], ln_gdn_b[0], moe_router[0],
                        moe_w_gate[0], moe_w_up[0], moe_w_down[0], ln_moe_g[0], ln_moe_b[0])
    return y.reshape(B, S, D)
```

```python
import functools

import numpy as np
import jax
import jax.numpy as jnp
from jax import lax
from jax.experimental import pallas as pl
from jax.experimental.pallas import tpu as pltpu

F32 = jnp.float32
BF16 = jnp.bfloat16
HIGHEST = lax.Precision.HIGHEST

LANES = 128
SUBLANES = 8
VMEM_LIMIT = 52 * 1024 * 1024

HEAD_DIM = 64
N_HEADS_ATT = 8
ATT_WIDTH = N_HEADS_ATT * HEAD_DIM
QUERY_BLOCK = 128
DIL_PATTERNS = ((128, 1), (512, 4), (2048, 16))
GDN_HEAD_DIM = 128
N_HEADS_GDN = 8
GDN_WIDTH = N_HEADS_GDN * GDN_HEAD_DIM
GDN_CONV = 4
GDN_CHUNK = 64
N_EXPERTS = 8
DEPTH = 2
DEEPNORM_ALPHA = (2.0 * DEPTH) ** 0.25
LN_EPS = 1e-5
RMS_EPS = 1e-6
NEG_INF = -1e30
LOG2E = 1.4426950408889634

PROJ_TM = 1024
PROJ_TN = 1664
FOX_TQ = 1024
FOX_GROUP = 512
FOX_STRIP = 32
DIL_GROUP = 4
LN_TM = 512
FFN_TM = 1024
FFN_TF = 512
GDN_BLK = 128
GDN_HEADS_PER_STEP = 8
DISPATCH_CHUNK = 512


def _cparams(*sem):
    return pltpu.CompilerParams(dimension_semantics=sem, vmem_limit_bytes=VMEM_LIMIT)


def _iota(shape, dim):
    return lax.broadcasted_iota(jnp.int32, shape, dim)


def _silu(x):
    return x * jax.nn.sigmoid(x)


def _row_to_col(row):
    n = row.shape[1]
    eye = _iota((LANES, LANES), 0) == _iota((LANES, LANES), 1)
    cols = []
    for c in range(n // LANES):
        seg = row[:, c * LANES:(c + 1) * LANES]
        cols.append(jnp.sum(jnp.where(eye, seg, 0.0), axis=1, keepdims=True))
    return cols[0] if len(cols) == 1 else jnp.concatenate(cols, axis=0)


def _layer_norm_rows(z, g, b):
    mu = jnp.mean(z, axis=-1, keepdims=True)
    zc = z - mu
    var = jnp.mean(zc * zc, axis=-1, keepdims=True)
    return zc * lax.rsqrt(var + LN_EPS) * g + b


def _proj_kernel(x_ref, w_ref, o_ref):
    o_ref[...] = jnp.dot(x_ref[...], w_ref[...], preferred_element_type=F32).astype(o_ref.dtype)


def _proj(x, w, out_dtype):
    M, K = x.shape
    C = w.shape[1]
    tm = min(PROJ_TM, M)
    tn = min(PROJ_TN, C)
    assert M % tm == 0 and C % tn == 0
    return pl.pallas_call(
        _proj_kernel,
        out_shape=jax.ShapeDtypeStruct((M, C), out_dtype),
        grid=(M // tm, C // tn),
        in_specs=[pl.BlockSpec((tm, K), lambda i, j: (i, 0)),
                  pl.BlockSpec((K, tn), lambda i, j: (0, j))],
        out_specs=pl.BlockSpec((tm, tn), lambda i, j: (i, j)),
        compiler_params=_cparams("parallel", "parallel"),
        name="proj",
    )(x, w)


def _forget_cumsum_kernel(f_ref, b_ref, o_ref):
    S = f_ref.shape[2]
    z = f_ref[0] + b_ref[...]
    lf = (jnp.minimum(z, 0.0) - jnp.log1p(jnp.exp(-jnp.abs(z)))) * LOG2E
    upper = (_iota((LANES, LANES), 0) <= _iota((LANES, LANES), 1)).astype(F32)
    carry = jnp.zeros((z.shape[0], 1), F32)
    for c in range(S // LANES):
        seg = jnp.dot(lf[:, c * LANES:(c + 1) * LANES], upper, precision=HIGHEST,
                      preferred_element_type=F32) + carry
        o_ref[0, :, c * LANES:(c + 1) * LANES] = seg
        carry = seg[:, LANES - 1:LANES]


def _forget_cumsum(f_rows, bias):
    B, H, S = f_rows.shape
    return pl.pallas_call(
        _forget_cumsum_kernel,
        out_shape=jax.ShapeDtypeStruct((B, H, S), F32),
        grid=(B,),
        in_specs=[pl.BlockSpec((1, H, S), lambda b: (b, 0, 0)),
                  pl.BlockSpec((H, 1), lambda b: (0, 0))],
        out_specs=pl.BlockSpec((1, H, S), lambda b: (b, 0, 0)),
        compiler_params=_cparams("parallel"),
        name="forget_cumsum",
    )(f_rows, bias.reshape(H, 1).astype(F32))


def _fox_kernel(q_ref, k_ref, v_ref, fq_ref, fk_ref, o_ref,
                s_scr, p_scr, m_scr, l_scr, a_scr, acc_scr, fq_scr, *, tq):
    G = min(FOX_GROUP, tq)
    R = FOX_STRIP
    i = pl.program_id(2)
    contract_last = (((1,), (1,)), ((), ()))
    units = [(hh, g) for hh in range(2) for g in range(tq // G)]
    lane_g = _iota((G, LANES), 1)
    qh = []
    for hh, g in units:
        qg = q_ref[0, g * G:(g + 1) * G, :]
        qh.append(jnp.where((lane_g < HEAD_DIM) == (hh == 0), qg, jnp.zeros_like(qg)))
    for hh in range(2):
        fq_scr[hh] = jnp.broadcast_to(_row_to_col(fq_ref[0, 0, 0, hh:hh + 1, :]), (tq, LANES))
    m_scr[...] = jnp.full(m_scr.shape, NEG_INF, F32)
    l_scr[...] = jnp.zeros(l_scr.shape, F32)
    acc_scr[...] = jnp.zeros(acc_scr.shape, F32)

    def kv_block(j, masked):
        kstart = pl.multiple_of(j * tq, tq)

        def ncols(g):
            return (g + 1) * G if masked else tq

        def scores(u):
            hh, g = units[u]
            n = ncols(g)
            kb = k_ref[0, pl.ds(kstart, n), :]
            s_scr[u, :, :n] = (lax.dot_general(qh[u], kb, contract_last, preferred_element_type=F32)
                               - fk_ref[0, 0, j, hh:hh + 1, :n])

        def strips(u):
            hh, g = units[u]
            n = ncols(g)
            for r in range(G // R):
                rows = slice(r * R, (r + 1) * R)
                grow = slice(g * G + r * R, g * G + (r + 1) * R)
                s = s_scr[u, rows, :n]
                if masked:
                    s = jnp.where(_iota((R, n), 1) <= _iota((R, n), 0) + (g * G + r * R), s, NEG_INF)
                fq = fq_scr[hh, grow, :]
                m_old = m_scr[hh, grow, :]
                m_new = jnp.maximum(m_old, jnp.max(s, axis=-1, keepdims=True) + fq)
                p = jnp.exp2(s - jnp.concatenate([m_new - fq] * (n // LANES), axis=1))
                alpha = jnp.exp2(m_old - m_new)
                a_scr[hh, grow, :] = alpha
                l_scr[hh, grow, :] = alpha * l_scr[hh, grow, :] + jnp.sum(p, axis=-1, keepdims=True)
                m_scr[hh, grow, :] = m_new
                p_scr[u, rows, :n] = p.astype(BF16)

        def values(u):
            hh, g = units[u]
            n = ncols(g)
            grow = slice(g * G, (g + 1) * G)
            vb = v_ref[0, pl.ds(kstart, n), :]
            acc_scr[hh, grow, :] = (a_scr[hh, grow, :] * acc_scr[hh, grow, :]
                                    + jnp.dot(p_scr[u, :, :n], vb, preferred_element_type=F32))

        scores(0)
        for u in range(1, len(units)):
            scores(u)
            strips(u - 1)
            values(u - 1)
        strips(len(units) - 1)
        values(len(units) - 1)

    def body(j, carry):
        kv_block(j, False)
        return carry

    lax.fori_loop(0, i, body, 0)
    kv_block(i, True)
    lane = _iota((tq, LANES), 1)
    o_ref[0] = jnp.where(lane < HEAD_DIM, acc_scr[0] / l_scr[0], acc_scr[1] / l_scr[1]).astype(o_ref.dtype)


def _fox_attention(qkv, f_cum, B, S):
    tq = min(FOX_TQ, S)
    nk = S // tq
    grp = min(FOX_GROUP, tq)
    n_units = 2 * (tq // grp)
    n_pairs = N_HEADS_ATT // 2
    qkv3 = qkv.reshape(B, S, 3 * ATT_WIDTH)
    f5 = f_cum.reshape(B, n_pairs, 2, nk, tq).transpose(0, 1, 3, 2, 4)
    out = pl.pallas_call(
        functools.partial(_fox_kernel, tq=tq),
        out_shape=jax.ShapeDtypeStruct((B, S, ATT_WIDTH), BF16),
        grid=(B, n_pairs, nk),
        in_specs=[pl.BlockSpec((1, tq, LANES), lambda b, p, i: (b, i, p)),
                  pl.BlockSpec((1, S, LANES), lambda b, p, i: (b, 0, n_pairs + p)),
                  pl.BlockSpec((1, S, LANES), lambda b, p, i: (b, 0, 2 * n_pairs + p)),
                  pl.BlockSpec((1, 1, 1, 2, tq), lambda b, p, i: (b, p, i, 0, 0)),
                  pl.BlockSpec((1, 1, nk, 2, tq), lambda b, p, i: (b, p, 0, 0, 0))],
        out_specs=pl.BlockSpec((1, tq, LANES), lambda b, p, i: (b, i, p)),
        scratch_shapes=[pltpu.VMEM((n_units, grp, tq), F32), pltpu.VMEM((n_units, grp, tq), BF16)]
                       + [pltpu.VMEM((2, tq, LANES), F32) for _ in range(5)],
        compiler_params=_cparams("parallel", "parallel", "arbitrary"),
        name="fox_attention",
    )(qkv3, qkv3, qkv3, f5, f5)
    return out.reshape(B * S, ATT_WIDTH)


def _dil_kernel(slope_ref, q_ref, k_ref, v_ref, o_ref, m0_scr, m1_scr, l_scr, acc_scr, *, S):
    QB = QUERY_BLOCK
    G = DIL_GROUP
    p_idx = pl.program_id(1)
    lane = _iota((QB, LANES), 1)
    head0 = lane < HEAD_DIM
    contract_last = (((1,), (1,)), ((), ()))
    qscale = HEAD_DIM ** -0.5 * LOG2E
    slopes = [slope_ref[2 * p_idx + hh] * LOG2E for hh in range(2)]
    delta_cur = _iota((QB, QB), 0) - _iota((QB, QB), 1)
    delta_two = QB + _iota((QB, 2 * QB), 0) - _iota((QB, 2 * QB), 1)

    def rows(start, dil):
        return pl.ds(start, QB, stride=dil) if dil > 1 else pl.ds(start, QB)

    def update(q0s, kp0s, dil, biases, first):
        loaded = []
        for g, q0 in enumerate(q0s):
            rq = rows(q0, dil)
            q = q_ref[0, rq, :] * qscale
            kc = k_ref[0, rq, :]
            vc = v_ref[0, rq, :]
            if kp0s is not None:
                rp = rows(kp0s[g], dil)
                kc = jnp.concatenate([k_ref[0, rp, :], kc], axis=0)
                vc = jnp.concatenate([v_ref[0, rp, :], vc], axis=0)
            old = None if first else (m0_scr[rq, :], m1_scr[rq, :], l_scr[rq, :], acc_scr[rq, :])
            loaded.append((rq, q, kc.astype(BF16), vc.astype(BF16), old))
        units = [(g, hh) for g in range(len(loaded)) for hh in range(2)]
        reps = loaded[0][2].shape[0] // LANES
        s_u = [lax.dot_general(jnp.where(head0 == (hh == 0), loaded[g][1], 0.0).astype(BF16), loaded[g][2],
                               contract_last, preferred_element_type=F32) - biases[hh] for g, hh in units]
        m_u = [jnp.broadcast_to(jnp.max(s, axis=-1, keepdims=True), (QB, LANES)) for s in s_u]
        if not first:
            m_u = [jnp.maximum(loaded[g][4][hh], m) for (g, hh), m in zip(units, m_u)]
            a_u = [jnp.exp2(loaded[g][4][hh] - m) for (g, hh), m in zip(units, m_u)]
        p_u = [jnp.exp2(s - jnp.concatenate([m] * reps, axis=1)) for s, m in zip(s_u, m_u)]
        ps_u = [jnp.sum(p, axis=-1, keepdims=True) for p in p_u]
        pv_u = [jnp.dot(p.astype(BF16), loaded[g][3], preferred_element_type=F32)
                for (g, hh), p in zip(units, p_u)]
        for g in range(len(loaded)):
            rq, old = loaded[g][0], loaded[g][4]
            l_new = jnp.where(head0, ps_u[2 * g], ps_u[2 * g + 1])
            acc_new = jnp.where(head0, pv_u[2 * g], pv_u[2 * g + 1])
            if not first:
                alpha = jnp.where(head0, a_u[2 * g], a_u[2 * g + 1])
                l_new = alpha * old[2] + l_new
                acc_new = alpha * old[3] + acc_new
            m0_scr[rq, :] = m_u[2 * g]
            m1_scr[rq, :] = m_u[2 * g + 1]
            l_scr[rq, :] = l_new
            acc_scr[rq, :] = acc_new

    for branch, (window, dil) in enumerate(sorted(DIL_PATTERNS, key=lambda wd: -wd[1])):
        span = window // dil
        assert span <= QB and (S // dil) % QB == 0
        nblk = S // dil // QB
        first = branch == 0

        def masked_bias(delta, hh, dil=dil, span=span):
            return jnp.where((delta >= 0) & (delta <= span), slopes[hh] * (delta * dil).astype(F32), -NEG_INF)

        bias_cur = [masked_bias(delta_cur, hh) for hh in range(2)]
        bias_two = [masked_bias(delta_two, hh) for hh in range(2)]

        ga = min(G, dil)

        def head_step(t, carry, dil=dil, ga=ga, bias_cur=bias_cur, first=first):
            update([t * ga + g for g in range(ga)], None, dil, bias_cur, first)
            return carry

        lax.fori_loop(0, dil // ga, head_step, 0)

        def starts(idx, dil=dil):
            q0 = (idx % dil) + (1 + idx // dil) * (QB * dil)
            return q0, q0 - QB * dil

        def tail_step(t, carry, base=0, count=G, dil=dil, bias_two=bias_two, first=first):
            pairs = [starts(base + t * count + g) for g in range(count)]
            update([a for a, _ in pairs], [b for _, b in pairs], dil, bias_two, first)
            return carry

        n_tail = dil * (nblk - 1)
        lax.fori_loop(0, n_tail // G, tail_step, 0)
        if n_tail % G:
            tail_step(0, 0, base=(n_tail // G) * G, count=n_tail % G)
    o_ref[0] = (acc_scr[...] / l_scr[...]).astype(o_ref.dtype)


def _dilated_attention(qkv, B, S):
    n_pairs = N_HEADS_ATT // 2
    slopes = jnp.asarray(2.0 ** (-8.0 * (np.arange(N_HEADS_ATT) + 1) / N_HEADS_ATT), dtype=F32)
    qkv3 = qkv.reshape(B, S, qkv.shape[1])
    out = pl.pallas_call(
        functools.partial(_dil_kernel, S=S),
        out_shape=jax.ShapeDtypeStruct((B, S, ATT_WIDTH), BF16),
        grid=(B, n_pairs),
        in_specs=[pl.BlockSpec(memory_space=pltpu.SMEM),
                  pl.BlockSpec((1, S, LANES), lambda b, p: (b, 0, p)),
                  pl.BlockSpec((1, S, LANES), lambda b, p: (b, 0, n_pairs + p)),
                  pl.BlockSpec((1, S, LANES), lambda b, p: (b, 0, 2 * n_pairs + p))],
        out_specs=pl.BlockSpec((1, S, LANES), lambda b, p: (b, 0, p)),
        scratch_shapes=[pltpu.VMEM((S, LANES), F32) for _ in range(4)],
        compiler_params=_cparams("parallel", "parallel"),
        name="dilated_attention",
    )(slopes, qkv3, qkv3, qkv3)
    return out.reshape(B * S, ATT_WIDTH)


def _attn_out_kernel(a0_ref, a1_ref, w_ref, x_ref, g_ref, b_ref, y_ref, ybf_ref):
    k0 = a0_ref.shape[1]
    mix = jnp.dot(a0_ref[...], w_ref[:k0, :], preferred_element_type=F32)
    mix = mix + jnp.dot(a1_ref[...], w_ref[k0:, :], preferred_element_type=F32)
    y = _layer_norm_rows(DEEPNORM_ALPHA * x_ref[...] + mix, g_ref[...], b_ref[...])
    y_ref[...] = y
    ybf_ref[...] = y.astype(BF16)


def _attn_out_ln(a0, a1, w, x, g, b):
    M, D = x.shape
    tm = min(LN_TM, M)
    row = lambda i: (i, 0)
    const = lambda i: (0, 0)
    return pl.pallas_call(
        _attn_out_kernel,
        out_shape=(jax.ShapeDtypeStruct((M, D), F32), jax.ShapeDtypeStruct((M, D), BF16)),
        grid=(M // tm,),
        in_specs=[pl.BlockSpec((tm, a0.shape[1]), row), pl.BlockSpec((tm, a1.shape[1]), row),
                  pl.BlockSpec(w.shape, const), pl.BlockSpec((tm, D), row),
                  pl.BlockSpec((1, D), const), pl.BlockSpec((1, D), const)],
        out_specs=(pl.BlockSpec((tm, D), row), pl.BlockSpec((tm, D), row)),
        compiler_params=_cparams("parallel"),
        name="attn_out_ln",
    )(a0, a1, w, x, g.reshape(1, D), b.reshape(1, D))


def _gdn_out_kernel(a_ref, w_ref, x_ref, g_ref, b_ref, r_ref, yt_ref, route_ref):
    tm = x_ref.shape[0]
    mix = jnp.dot(a_ref[...], w_ref[...], preferred_element_type=F32)
    y = _layer_norm_rows(DEEPNORM_ALPHA * x_ref[...] + mix, g_ref[...], b_ref[...])
    for s in range(SUBLANES):
        yt_ref[pl.ds(s, tm, stride=SUBLANES), :] = y[:, s * LANES:(s + 1) * LANES]
    y_hi = y.astype(BF16)
    y_lo = (y - y_hi.astype(F32)).astype(BF16)
    both = jnp.dot(y_hi, r_ref[...], preferred_element_type=F32)
    logits = both[:, :LANES] + (both[:, LANES:] + jnp.dot(y_lo, r_ref[:, :LANES], preferred_element_type=F32))
    lane = _iota((tm, LANES), 1)
    logits = jnp.where(lane < N_EXPERTS, logits, -jnp.inf)
    m1 = jnp.max(logits, axis=-1, keepdims=True)
    i1 = jnp.min(jnp.where(logits == m1, lane, LANES), axis=-1, keepdims=True)
    rest = jnp.where(lane == i1, -jnp.inf, logits)
    m2 = jnp.max(rest, axis=-1, keepdims=True)
    i2 = jnp.min(jnp.where(rest == m2, lane, LANES), axis=-1, keepdims=True)
    e2 = jnp.exp(m2 - m1)
    w1 = 1.0 / (1.0 + e2)
    w2 = e2 / (1.0 + e2)
    route = jnp.where(lane == 0, i1.astype(F32),
                      jnp.where(lane == 1, i2.astype(F32),
                                jnp.where(lane == 2, w1, jnp.where(lane == 3, w2, 0.0))))
    route_ref[...] = route


def _gdn_out_ln_route(a, w, x, g, b, router):
    M, D = x.shape
    tm = min(LN_TM, M)
    row = lambda i: (i, 0)
    const = lambda i: (0, 0)
    r_pad = jnp.zeros((D, LANES), F32).at[:, :N_EXPERTS].set(router.astype(F32))
    r_hi = r_pad.astype(BF16)
    r_lo = (r_pad - r_hi.astype(F32)).astype(BF16)
    r_pad = jnp.concatenate([r_hi, r_lo], axis=1)
    return pl.pallas_call(
        _gdn_out_kernel,
        out_shape=(jax.ShapeDtypeStruct((M * SUBLANES, LANES), F32),
                   jax.ShapeDtypeStruct((M, LANES), F32)),
        grid=(M // tm,),
        in_specs=[pl.BlockSpec((tm, a.shape[1]), row), pl.BlockSpec(w.shape, const),
                  pl.BlockSpec((tm, D), row), pl.BlockSpec((1, D), const),
                  pl.BlockSpec((1, D), const), pl.BlockSpec((D, 2 * LANES), const)],
        out_specs=(pl.BlockSpec((tm * SUBLANES, LANES), row), pl.BlockSpec((tm, LANES), row)),
        compiler_params=_cparams("parallel"),
        name="gdn_out_ln_route",
    )(a, w, x, g.reshape(1, D), b.reshape(1, D), r_pad)


def _ffn_dense_kernel(xbf_ref, wg_ref, wu_ref, wd_ref, x_ref, g_ref, b_ref, y_ref, ybf_ref, acc_scr):
    j = pl.program_id(1)

    @pl.when(j == 0)
    def _():
        acc_scr[...] = jnp.zeros_like(acc_scr)

    x = xbf_ref[...]
    hg = jnp.dot(x, wg_ref[...], preferred_element_type=F32)
    hu = jnp.dot(x, wu_ref[...], preferred_element_type=F32)
    h = (_silu(hg) * hu).astype(BF16)
    acc_scr[...] += jnp.dot(h, wd_ref[...], preferred_element_type=F32)

    @pl.when(j == pl.num_programs(1) - 1)
    def _():
        y = _layer_norm_rows(DEEPNORM_ALPHA * x_ref[...] + acc_scr[...], g_ref[...], b_ref[...])
        y_ref[...] = y
        ybf_ref[...] = y.astype(BF16)


def _ffn_dense_ln(xbf, x, wg, wu, wd, g, b):
    M, D = x.shape
    FF = wg.shape[1]
    tm = min(FFN_TM, M)
    tf = FFN_TF
    assert FF % tf == 0
    row = lambda i, j: (i, 0)
    const = lambda i, j: (0, 0)
    return pl.pallas_call(
        _ffn_dense_kernel,
        out_shape=(jax.ShapeDtypeStruct((M, D), F32), jax.ShapeDtypeStruct((M, D), BF16)),
        grid=(M // tm, FF // tf),
        in_specs=[pl.BlockSpec((tm, D), row),
                  pl.BlockSpec((D, tf), lambda i, j: (0, j)),
                  pl.BlockSpec((D, tf), lambda i, j: (0, j)),
                  pl.BlockSpec((tf, D), lambda i, j: (j, 0)),
                  pl.BlockSpec((tm, D), row),
                  pl.BlockSpec((1, D), const), pl.BlockSpec((1, D), const)],
        out_specs=(pl.BlockSpec((tm, D), row), pl.BlockSpec((tm, D), row)),
        scratch_shapes=[pltpu.VMEM((tm, D), F32)],
        compiler_params=_cparams("parallel", "arbitrary"),
        name="ffn_dense_ln",
    )(xbf, wg, wu, wd, x, g.reshape(1, D), b.reshape(1, D))


def _ffn_grouped_kernel(te_ref, na_ref, xt_ref, wg_ref, wu_ref, wd_ref, yt_ref, xbf_scr, acc_scr):
    i = pl.program_id(0)
    j = pl.program_id(1)
    tm = acc_scr.shape[0]
    active = i < na_ref[0]

    @pl.when(active & (j == 0))
    def _():
        acc_scr[...] = jnp.zeros_like(acc_scr)
        for s in range(SUBLANES):
            xbf_scr[:, s * LANES:(s + 1) * LANES] = xt_ref[pl.ds(s, tm, stride=SUBLANES), :].astype(BF16)

    @pl.when(active)
    def _():
        x = xbf_scr[...]
        hg = jnp.dot(x, wg_ref[0].astype(BF16), preferred_element_type=F32)
        hu = jnp.dot(x, wu_ref[0].astype(BF16), preferred_element_type=F32)
        h = (_silu(hg) * hu).astype(BF16)
        acc_scr[...] += jnp.dot(h, wd_ref[0].astype(BF16), preferred_element_type=F32)

    @pl.when(active & (j == pl.num_programs(1) - 1))
    def _():
        for s in range(SUBLANES):
            yt_ref[pl.ds(s, tm, stride=SUBLANES), :] = acc_scr[:, s * LANES:(s + 1) * LANES]

    @pl.when(jnp.logical_not(active) & (j == 0))
    def _():
        yt_ref[...] = jnp.zeros_like(yt_ref)


def _ffn_grouped(xt, tile_expert, n_active, wg, wu, wd, tm):
    R = xt.shape[0] // SUBLANES
    E, D, FF = wg.shape
    tf = FFN_TF
    nf = FF // tf
    n_tiles = R // tm

    def row_map(i, j, te, na):
        return (jnp.minimum(i, na[0] - 1), 0)

    def ff_idx(i, j, na):
        return jnp.where(i < na[0], j, nf - 1)

    return pl.pallas_call(
        _ffn_grouped_kernel,
        out_shape=jax.ShapeDtypeStruct((R * SUBLANES, LANES), F32),
        grid_spec=pltpu.PrefetchScalarGridSpec(
            num_scalar_prefetch=2,
            grid=(n_tiles, nf),
            in_specs=[pl.BlockSpec((tm * SUBLANES, LANES), row_map),
                      pl.BlockSpec((1, D, tf), lambda i, j, te, na: (te[i], 0, ff_idx(i, j, na))),
                      pl.BlockSpec((1, D, tf), lambda i, j, te, na: (te[i], 0, ff_idx(i, j, na))),
                      pl.BlockSpec((1, tf, D), lambda i, j, te, na: (te[i], ff_idx(i, j, na), 0))],
            out_specs=pl.BlockSpec((tm * SUBLANES, LANES), lambda i, j, te, na: (i, 0)),
            scratch_shapes=[pltpu.VMEM((tm, D), BF16), pltpu.VMEM((tm, D), F32)]),
        compiler_params=_cparams("arbitrary", "arbitrary"),
        name="ffn_grouped",
    )(tile_expert, n_active, xt, wg, wu, wd)


def _row_tile(ref, r):
    return ref.at[pl.ds(pl.multiple_of(r, SUBLANES), SUBLANES), :]


def _dispatch_kernel(dest_ref, xt_ref, xs_in_hbm, xs_hbm, sem):
    del xs_in_hbm
    ch = dest_ref.shape[2] // 2

    def body(t, carry):
        src = _row_tile(xt_ref, t * SUBLANES)
        for k in range(2):
            pltpu.make_async_copy(src, _row_tile(xs_hbm, dest_ref[0, 0, 2 * t + k]), sem).start()
        return carry

    lax.fori_loop(0, ch, body, 0, unroll=8)
    for _ in range(2):
        pltpu.make_async_copy(xt_ref, xs_hbm.at[pl.ds(0, ch * SUBLANES), :], sem).wait()


def _dispatch_rows(xt, dest8, xs_init):
    N = xt.shape[0] // SUBLANES
    ch = min(DISPATCH_CHUNK, N)
    dest3 = dest8.reshape(N // ch, 1, 2 * ch)
    return pl.pallas_call(
        _dispatch_kernel,
        out_shape=jax.ShapeDtypeStruct(xs_init.shape, F32),
        grid=(N // ch,),
        in_specs=[pl.BlockSpec((1, 1, 2 * ch), lambda i: (i, 0, 0), memory_space=pltpu.SMEM),
                  pl.BlockSpec((ch * SUBLANES, LANES), lambda i: (i, 0)),
                  pl.BlockSpec(memory_space=pl.ANY)],
        out_specs=pl.BlockSpec(memory_space=pl.ANY),
        scratch_shapes=[pltpu.SemaphoreType.DMA(())],
        input_output_aliases={2: 0},
        compiler_params=_cparams("arbitrary"),
        name="moe_dispatch",
    )(dest3, xt, xs_init)


def _moe_ln_kernel(dest_ref, dnext_ref, xt_ref, ys_hbm, route_ref, g_ref, b_ref, o_ref, ya_scr, yb_scr, sem):
    tm = o_ref.shape[0]
    i = pl.program_id(0)
    slot = i % 2

    def start_gathers(idx_ref, sl):
        def body(t, carry):
            pltpu.make_async_copy(_row_tile(ys_hbm, idx_ref[0, 0, 2 * t]),
                                  _row_tile(ya_scr.at[sl], t * SUBLANES), sem.at[sl]).start()
            pltpu.make_async_copy(_row_tile(ys_hbm, idx_ref[0, 0, 2 * t + 1]),
                                  _row_tile(yb_scr.at[sl], t * SUBLANES), sem.at[sl]).start()
            return carry

        lax.fori_loop(0, tm, body, 0, unroll=8)

    @pl.when(i == 0)
    def _():
        start_gathers(dest_ref, 0)

    @pl.when(i + 1 < pl.num_programs(0))
    def _():
        start_gathers(dnext_ref, 1 - slot)

    for scr in (ya_scr, yb_scr):
        pltpu.make_async_copy(ys_hbm.at[pl.ds(0, tm * SUBLANES), :], scr.at[slot], sem.at[slot]).wait()

    w1 = jnp.broadcast_to(route_ref[:, 2:3], (tm, LANES))
    w2 = jnp.broadcast_to(route_ref[:, 3:4], (tm, LANES))
    parts = []
    for s in range(SUBLANES):
        rows = pl.ds(s, tm, stride=SUBLANES)
        parts.append(DEEPNORM_ALPHA * xt_ref[rows, :]
                     + (w1 * ya_scr[slot, rows, :] + w2 * yb_scr[slot, rows, :]))
    d_model = SUBLANES * LANES
    mu = jnp.sum(sum(parts), axis=-1, keepdims=True) / d_model
    var = jnp.sum(sum((z - mu) * (z - mu) for z in parts), axis=-1, keepdims=True) / d_model
    rstd = lax.rsqrt(var + LN_EPS)
    for s in range(SUBLANES):
        cols = slice(s * LANES, (s + 1) * LANES)
        o_ref[:, cols] = (parts[s] - mu) * rstd * g_ref[:, cols] + b_ref[:, cols]


def _moe_combine_ln(xt, ys, dest8, route, g, b):
    M = route.shape[0]
    D = SUBLANES * LANES
    tm = min(LN_TM, M)
    row = lambda i: (i, 0)
    const = lambda i: (0, 0)
    n_blocks = M // tm
    dest3 = dest8.reshape(n_blocks, 1, 2 * tm)
    return pl.pallas_call(
        _moe_ln_kernel,
        out_shape=jax.ShapeDtypeStruct((M, D), F32),
        grid=(n_blocks,),
        in_specs=[pl.BlockSpec((1, 1, 2 * tm), lambda i: (i, 0, 0), memory_space=pltpu.SMEM),
                  pl.BlockSpec((1, 1, 2 * tm), lambda i: (jnp.minimum(i + 1, n_blocks - 1), 0, 0),
                               memory_space=pltpu.SMEM),
                  pl.BlockSpec((tm * SUBLANES, LANES), row),
                  pl.BlockSpec(memory_space=pl.ANY),
                  pl.BlockSpec((tm, LANES), row),
                  pl.BlockSpec((1, D), const), pl.BlockSpec((1, D), const)],
        out_specs=pl.BlockSpec((tm, D), row),
        scratch_shapes=[pltpu.VMEM((2, tm * SUBLANES, LANES), F32), pltpu.VMEM((2, tm * SUBLANES, LANES), F32),
                        pltpu.SemaphoreType.DMA((2,))],
        compiler_params=_cparams("arbitrary"),
        name="moe_combine_ln",
    )(dest3, dest3, xt, ys, route, g.reshape(1, D), b.reshape(1, D))


def _gdn_kernel(alog_ref, dt_ref, pq_ref, pk_ref, pv_ref, hq_ref, hk_ref, hv_ref,
                cq_ref, ck_ref, cv_ref, br_ref, ar_ref, ng_ref, gate_ref, o_ref,
                state_scr, sq_scr, sk_scr, sv_scr, *, blk, hp):
    C = GDN_CHUNK
    Dh = GDN_HEAD_DIM
    nchunk = blk // C
    h0 = pl.program_id(1) * hp
    sb = pl.program_id(2)
    heads = range(hp)

    @pl.when(sb == 0)
    def _():
        state_scr[...] = jnp.zeros_like(state_scr)

    have_prev = (sb > 0).astype(F32)

    def conv_silu(cur_ref, halo_ref, w_ref, stage_scr):
        outs = []
        for c in range(stage_scr.shape[0]):
            lanes = slice(c * LANES, (c + 1) * LANES)
            stage_scr[c, pl.ds(0, SUBLANES, stride=2), :] = halo_ref[0, :, lanes] * have_prev
            stage_scr[c, pl.ds(2 * SUBLANES, blk, stride=2), :] = cur_ref[0, :, lanes]
            out = None
            for j in range(GDN_CONV):
                off = 2 * (SUBLANES - (GDN_CONV - 1) + j)
                term = w_ref[j:j + 1, lanes] * stage_scr[c, pl.ds(off, blk, stride=2), :]
                out = term if out is None else out + term
            outs.append(out)
        return _silu(jnp.concatenate(outs, axis=1))

    def split(t):
        return [t[:, hh * Dh:(hh + 1) * Dh] for hh in heads]

    def l2n(t):
        return t * lax.rsqrt(jnp.sum(t * t, axis=-1, keepdims=True) + RMS_EPS)

    q_h = [l2n(t) * (Dh ** -0.5) for t in split(conv_silu(pq_ref, hq_ref, cq_ref, sq_scr))]
    k_h = [l2n(t) for t in split(conv_silu(pk_ref, hk_ref, ck_ref, sk_scr))]
    v_h = split(conv_silu(pv_ref, hv_ref, cv_ref, sv_scr))

    lanes_row = _iota((hp, blk), 0)
    dt_rows = jnp.zeros((hp, blk), F32)
    alog_rows = jnp.zeros((hp, blk), F32)
    for hh in heads:
        dt_rows = jnp.where(lanes_row == hh, dt_ref[h0 + hh], dt_rows)
        alog_rows = jnp.where(lanes_row == hh, alog_ref[h0 + hh], alog_rows)
    beta_rows = jax.nn.sigmoid(br_ref[0, 0, 0])
    za = ar_ref[0, 0, 0] + dt_rows
    g_rows = -jnp.exp(alog_rows) * (jnp.maximum(za, 0.0) + jnp.log1p(jnp.exp(-jnp.abs(za))))
    ri = _iota((blk, blk), 0)
    ci = _iota((blk, blk), 1)
    same = (ri // C) == (ci // C)
    g8 = jnp.concatenate([g_rows, jnp.zeros((SUBLANES - hp, blk), F32)], axis=0) if hp < SUBLANES else g_rows
    gam_rows = jnp.dot(g8, (same & (ri <= ci)).astype(F32), precision=HIGHEST,
                       preferred_element_type=F32)
    gl_rows = jnp.dot(g8, same.astype(F32), precision=HIGHEST,
                      preferred_element_type=F32)
    beta = [_row_to_col(beta_rows[hh:hh + 1, :]) for hh in heads]
    gam = [_row_to_col(gam_rows[hh:hh + 1, :]) for hh in heads]
    gl = [_row_to_col(gl_rows[hh:hh + 1, :]) for hh in heads]
    eg = [jnp.exp(t) for t in gam]
    ekd = [jnp.exp(a - b) for a, b in zip(gl, gam)]

    incl = same & (ri >= ci)
    strict = same & (ri > ci)
    contract_last = (((1,), (1,)), ((), ()))
    decay = [jnp.where(incl, jnp.exp(jnp.where(incl, gam[hh] - gam_rows[hh:hh + 1, :], 0.0)), 0.0)
             for hh in heads]
    kb = [t.astype(BF16) for t in k_h]
    kk = [lax.dot_general(t, t, contract_last, preferred_element_type=F32) for t in kb]
    x_acc = [jnp.where(strict, -(beta[hh] * kk[hh] * decay[hh]), 0.0) for hh in heads]
    pw = x_acc
    for _ in range(int(np.log2(C)) - 1):
        pwb = [t.astype(BF16) for t in pw]
        pw = [jnp.dot(t, t, preferred_element_type=F32) for t in pwb]
        x_acc = [x + p + jnp.dot(p.astype(BF16), x.astype(BF16), preferred_element_type=F32)
                 for x, p in zip(x_acc, pw)]
    rhs = [jnp.concatenate([v_h[hh] * beta[hh], k_h[hh] * (beta[hh] * eg[hh])], axis=1) for hh in heads]
    sol = [r + jnp.dot(x.astype(BF16), r.astype(BF16), preferred_element_type=F32)
           for x, r in zip(x_acc, rhs)]
    u = [t[:, :Dh] for t in sol]
    w_b = [t[:, Dh:].astype(BF16) for t in sol]
    qk = [lax.dot_general(q_h[hh].astype(BF16), kb[hh], contract_last, preferred_element_type=F32) * decay[hh]
          for hh in heads]
    q_dec = [(q_h[hh] * eg[hh]).astype(BF16) for hh in heads]
    k_dec = [(k_h[hh] * ekd[hh]).astype(BF16) for hh in heads]

    state = [state_scr[hh] for hh in heads]
    v_new = [[] for _ in heads]
    o_inter = [[] for _ in heads]
    for c in range(nchunk):
        rows = slice(c * C, (c + 1) * C)
        sbf = [t.astype(BF16) for t in state]
        vn = [u[hh][rows] - jnp.dot(w_b[hh][rows], sbf[hh], preferred_element_type=F32) for hh in heads]
        for hh in heads:
            o_inter[hh].append(jnp.dot(q_dec[hh][rows], sbf[hh], preferred_element_type=F32))
            v_new[hh].append(vn[hh])
        state = [state[hh] * jnp.exp(gl[hh][c * C:c * C + 1, :])
                 + lax.dot_general(k_dec[hh][rows], vn[hh].astype(BF16), (((0,), (0,)), ((), ())),
                                   preferred_element_type=F32) for hh in heads]
    for hh in heads:
        state_scr[hh] = state[hh]
    o = [jnp.concatenate(o_inter[hh], axis=0)
         + jnp.dot(qk[hh].astype(BF16), jnp.concatenate(v_new[hh], axis=0).astype(BF16),
                   preferred_element_type=F32) for hh in heads]
    o = [t * lax.rsqrt(jnp.mean(t * t, axis=-1, keepdims=True) + RMS_EPS) * ng_ref[...] for t in o]
    o = o[0] if hp == 1 else jnp.concatenate(o, axis=1)
    o_ref[0] = (o * _silu(gate_ref[0])).astype(o_ref.dtype)


def _gated_deltanet(pre, ab_rows, gate, conv_w, a_log, dt_bias, norm_g, B, S):
    blk = min(GDN_BLK, S)
    H = N_HEADS_GDN
    hp = GDN_HEADS_PER_STEP
    hg = H // hp
    wide = hp * GDN_HEAD_DIM
    pre3 = pre.reshape(B, S, pre.shape[1])
    gate3 = gate.reshape(B, S, GDN_WIDTH)
    ab5 = ab_rows.reshape(B, 2, hg, hp, S)
    halo_blocks = blk // SUBLANES

    def cur(sec):
        return pl.BlockSpec((1, blk, wide), lambda b, h, s: (b, s, sec * hg + h))

    def halo(sec):
        return pl.BlockSpec((1, SUBLANES, wide),
                            lambda b, h, s: (b, jnp.maximum(s * halo_blocks - 1, 0), sec * hg + h))

    def cw(sec):
        return pl.BlockSpec((GDN_CONV, wide), lambda b, h, s: (0, sec * hg + h))

    smem = pl.BlockSpec(memory_space=pltpu.SMEM)
    out = pl.pallas_call(
        functools.partial(_gdn_kernel, blk=blk, hp=hp),
        out_shape=jax.ShapeDtypeStruct((B, S, GDN_WIDTH), BF16),
        grid=(B, hg, S // blk),
        in_specs=[smem, smem,
                  cur(0), cur(1), cur(2), halo(0), halo(1), halo(2),
                  cw(0), cw(1), cw(2),
                  pl.BlockSpec((1, 1, 1, hp, blk), lambda b, h, s: (b, 0, h, 0, s)),
                  pl.BlockSpec((1, 1, 1, hp, blk), lambda b, h, s: (b, 1, h, 0, s)),
                  pl.BlockSpec((1, LANES), lambda b, h, s: (0, 0)),
                  pl.BlockSpec((1, blk, wide), lambda b, h, s: (b, s, h))],
        out_specs=pl.BlockSpec((1, blk, wide), lambda b, h, s: (b, s, h)),
        scratch_shapes=[pltpu.VMEM((hp, GDN_HEAD_DIM, GDN_HEAD_DIM), F32)]
                       + [pltpu.VMEM((wide // LANES, 2 * (blk + SUBLANES), LANES), F32) for _ in range(3)],
        compiler_params=_cparams("parallel", "parallel", "arbitrary"),
        name="gated_deltanet",
    )(a_log.astype(F32), dt_bias.astype(F32), pre3, pre3, pre3, pre3, pre3, pre3,
      conv_w, conv_w, conv_w, ab5, ab5, norm_g.reshape(1, LANES).astype(F32), gate3)
    return out.reshape(B * S, GDN_WIDTH)


def _pad_cols(w, width):
    return jnp.zeros((w.shape[0], width), w.dtype).at[:, :w.shape[1]].set(w)


def _attention_layer(x, xbf, B, S, w_in, forget_bias, w_out, ln_g, ln_b,
                     w_gate, w_up, w_down, ln2_g, ln2_b):
    W = ATT_WIDTH
    H = N_HEADS_ATT
    w_fox = jnp.concatenate([w_in[:, :W] * (HEAD_DIM ** -0.5 * LOG2E), w_in[:, W:3 * W]], axis=1).astype(BF16)
    w_dil = jnp.concatenate([w_in[:, 3 * W + H:], _pad_cols(w_in[:, 3 * W:3 * W + H], LANES)], axis=1).astype(BF16)
    qkv_fox = _proj(xbf, w_fox, BF16)
    qkv_dil = _proj(xbf, w_dil, F32)
    f_rows = qkv_dil[:, 3 * W:3 * W + H].reshape(B, S, H).transpose(0, 2, 1)
    f_cum = _forget_cumsum(f_rows, forget_bias)
    o_fox = _fox_attention(qkv_fox, f_cum, B, S)
    o_dil = _dilated_attention(qkv_dil, B, S)
    x1, x1bf = _attn_out_ln(o_fox, o_dil, w_out.astype(BF16), x, ln_g, ln_b)
    return _ffn_dense_ln(x1bf, x1, w_gate.astype(BF16), w_up.astype(BF16), w_down.astype(BF16),
                         ln2_g, ln2_b)


def _slot_indices(route, tm):
    N = route.shape[0]
    experts = route[:, 0:2].astype(jnp.int32).reshape(2 * N)
    onehot = (experts[:, None] == jnp.arange(N_EXPERTS, dtype=jnp.int32)[None, :]).astype(jnp.int32)
    csum = jnp.cumsum(onehot, axis=0)
    counts = csum[-1]
    padded = ((counts + tm - 1) // tm) * tm
    ends = jnp.cumsum(padded)
    starts = ends - padded
    dest = jnp.sum(onehot * (csum - 1 + starts[None, :]), axis=1).astype(jnp.int32)
    n_tiles = (2 * N) // tm + N_EXPERTS
    tile_start = jnp.arange(n_tiles, dtype=jnp.int32) * tm
    tile_expert = jnp.minimum(jnp.sum((tile_start[:, None] >= ends[None, :]).astype(jnp.int32), axis=1),
                              N_EXPERTS - 1).astype(jnp.int32)
    n_active = (ends[-1] // tm).astype(jnp.int32).reshape(1)
    return dest, tile_expert, n_active, n_tiles


def _deltanet_layer(x, xbf, B, S, w_in, conv_w, a_log, dt_bias, norm_g, w_out, ln_g, ln_b,
                    router, w_gate, w_up, w_down, ln2_g, ln2_b):
    N = B * S
    W = GDN_WIDTH
    H = N_HEADS_GDN
    w_qkv = jnp.concatenate([w_in[:, :3 * W], _pad_cols(w_in[:, 3 * W:3 * W + 2 * H], 2 * LANES)], axis=1).astype(BF16)
    w_gt = w_in[:, 3 * W + 2 * H:].astype(BF16)
    pre = _proj(xbf, w_qkv, F32)
    gate = _proj(xbf, w_gt, F32)
    ab_rows = pre[:, 3 * W:3 * W + 2 * H].reshape(B, S, 2 * H).transpose(0, 2, 1).reshape(B, 2 * H, 1, S)
    o = _gated_deltanet(pre, ab_rows, gate, conv_w.astype(F32), a_log, dt_bias, norm_g, B, S)
    xt, route = _gdn_out_ln_route(o, w_out.astype(BF16), x, ln_g, ln_b, router)

    tm = min(FFN_TM, N)
    dest, tile_expert, n_active, n_tiles = _slot_indices(route, tm)
    dest8 = dest * SUBLANES
    xs_init = jnp.zeros((n_tiles * tm * SUBLANES, LANES), F32)
    xs = _dispatch_rows(xt, dest8, xs_init)
    ys = _ffn_grouped(xs, tile_expert, n_active,
                      w_gate, w_up, w_down, tm)
    return _moe_combine_ln(xt, ys, dest8, route, ln2_g, ln2_b)


def kernel(x, attn_w_in, fox_forget_bias, attn_w_out, ln_attn_g, ln_attn_b, ffn_w_gate, ffn_w_up,
           ffn_w_down, ln_ffn_g, ln_ffn_b, gdn_w_in, gdn_conv_w, gdn_a_log, gdn_dt_bias, gdn_norm_g,
           gdn_w_out, ln_gdn_g, ln_gdn_b, moe_router, moe_w_gate, moe_w_up, moe_w_down, ln_moe_g,
           ln_moe_b):
    B, S, D = x.shape
    x2 = x.reshape(B * S, D)
    x2bf = x2.astype(BF16)
    x2, x2bf = _attention_layer(x2, x2bf, B, S, attn_w_in[0], fox_forget_bias[0], attn_w_out[0],
                                ln_attn_g[0], ln_attn_b[0], ffn_w_gate[0], ffn_w_up[0], ffn_w_down[0],
                                ln_ffn_g[0], ln_ffn_b[0])
    y = _deltanet_layer(x2, x2bf, B, S, gdn_w_in[0], gdn_conv_w[0], gdn_a_log[0], gdn_dt_bias[0],
                        gdn_norm_g[0], gdn_w_out[0], ln_gdn_g[0], ln_gdn_b[0], moe_router[0],
                        moe_w_gate[0], moe_w_up[0], moe_w_down[0], ln_moe_g[0], ln_moe_b[0])
    return y.reshape(B, S, D)
```

```python
import functools

import numpy as np
import jax
import jax.numpy as jnp
from jax import lax
from jax.experimental import pallas as pl
from jax.experimental.pallas import tpu as pltpu

F32 = jnp.float32
BF16 = jnp.bfloat16
HIGHEST = lax.Precision.HIGHEST

LANES = 128
SUBLANES = 8
VMEM_LIMIT = 52 * 1024 * 1024

HEAD_DIM = 64
N_HEADS_ATT = 8
ATT_WIDTH = N_HEADS_ATT * HEAD_DIM
QUERY_BLOCK = 128
DIL_PATTERNS = ((128, 1), (512, 4), (2048, 16))
GDN_HEAD_DIM = 128
N_HEADS_GDN = 8
GDN_WIDTH = N_HEADS_GDN * GDN_HEAD_DIM
GDN_CONV = 4
GDN_CHUNK = 64
N_EXPERTS = 8
DEPTH = 2
DEEPNORM_ALPHA = (2.0 * DEPTH) ** 0.25
LN_EPS = 1e-5
RMS_EPS = 1e-6
NEG_INF = -1e30
LOG2E = 1.4426950408889634

PROJ_TM = 1024
PROJ_TN = 1664
FOX_TQ = 1024
FOX_GROUP = 512
FOX_STRIP = 32
DIL_GROUP = 4
LN_TM = 512
FFN_TM = 1024
FFN_TF = 512
GDN_BLK = 128
GDN_HEADS_PER_STEP = 8
DISPATCH_CHUNK = 512


def _cparams(*sem):
    return pltpu.CompilerParams(dimension_semantics=sem, vmem_limit_bytes=VMEM_LIMIT)


def _iota(shape, dim):
    return lax.broadcasted_iota(jnp.int32, shape, dim)


def _silu(x):
    return x * jax.nn.sigmoid(x)


def _row_to_col(row):
    n = row.shape[1]
    eye = _iota((LANES, LANES), 0) == _iota((LANES, LANES), 1)
    cols = []
    for c in range(n // LANES):
        seg = row[:, c * LANES:(c + 1) * LANES]
        cols.append(jnp.sum(jnp.where(eye, seg, 0.0), axis=1, keepdims=True))
    return cols[0] if len(cols) == 1 else jnp.concatenate(cols, axis=0)


def _layer_norm_rows(z, g, b):
    mu = jnp.mean(z, axis=-1, keepdims=True)
    zc = z - mu
    var = jnp.mean(zc * zc, axis=-1, keepdims=True)
    return zc * lax.rsqrt(var + LN_EPS) * g + b


def _proj_kernel(x_ref, w_ref, o_ref):
    o_ref[...] = jnp.dot(x_ref[...], w_ref[...], preferred_element_type=F32).astype(o_ref.dtype)


def _proj(x, w, out_dtype):
    M, K = x.shape
    C = w.shape[1]
    tm = min(PROJ_TM, M)
    tn = min(PROJ_TN, C)
    assert M % tm == 0 and C % tn == 0
    return pl.pallas_call(
        _proj_kernel,
        out_shape=jax.ShapeDtypeStruct((M, C), out_dtype),
        grid=(M // tm, C // tn),
        in_specs=[pl.BlockSpec((tm, K), lambda i, j: (i, 0)),
                  pl.BlockSpec((K, tn), lambda i, j: (0, j))],
        out_specs=pl.BlockSpec((tm, tn), lambda i, j: (i, j)),
        compiler_params=_cparams("parallel", "parallel"),
        name="proj",
    )(x, w)


def _forget_cumsum_kernel(f_ref, b_ref, o_ref):
    S = f_ref.shape[2]
    z = f_ref[0] + b_ref[...]
    lf = (jnp.minimum(z, 0.0) - jnp.log1p(jnp.exp(-jnp.abs(z)))) * LOG2E
    upper = (_iota((LANES, LANES), 0) <= _iota((LANES, LANES), 1)).astype(F32)
    carry = jnp.zeros((z.shape[0], 1), F32)
    for c in range(S // LANES):
        seg = jnp.dot(lf[:, c * LANES:(c + 1) * LANES], upper, precision=HIGHEST,
                      preferred_element_type=F32) + carry
        o_ref[0, :, c * LANES:(c + 1) * LANES] = seg
        carry = seg[:, LANES - 1:LANES]


def _forget_cumsum(f_rows, bias):
    B, H, S = f_rows.shape
    return pl.pallas_call(
        _forget_cumsum_kernel,
        out_shape=jax.ShapeDtypeStruct((B, H, S), F32),
        grid=(B,),
        in_specs=[pl.BlockSpec((1, H, S), lambda b: (b, 0, 0)),
                  pl.BlockSpec((H, 1), lambda b: (0, 0))],
        out_specs=pl.BlockSpec((1, H, S), lambda b: (b, 0, 0)),
        compiler_params=_cparams("parallel"),
        name="forget_cumsum",
    )(f_rows, bias.reshape(H, 1).astype(F32))


def _fox_kernel(q_ref, k_ref, v_ref, fq_ref, fk_ref, o_ref,
                s_scr, p_scr, m_scr, l_scr, a_scr, acc_scr, fq_scr, *, tq):
    G = min(FOX_GROUP, tq)
    R = FOX_STRIP
    i = pl.program_id(2)
    contract_last = (((1,), (1,)), ((), ()))
    units = [(hh, g) for hh in range(2) for g in range(tq // G)]
    lane_g = _iota((G, LANES), 1)
    qh = []
    for hh, g in units:
        qg = q_ref[0, g * G:(g + 1) * G, :]
        qh.append(jnp.where((lane_g < HEAD_DIM) == (hh == 0), qg, jnp.zeros_like(qg)))
    for hh in range(2):
        fq_scr[hh] = jnp.broadcast_to(_row_to_col(fq_ref[0, 0, 0, hh:hh + 1, :]), (tq, LANES))
    m_scr[...] = jnp.full(m_scr.shape, NEG_INF, F32)
    l_scr[...] = jnp.zeros(l_scr.shape, F32)
    acc_scr[...] = jnp.zeros(acc_scr.shape, F32)

    def kv_block(j, masked):
        kstart = pl.multiple_of(j * tq, tq)

        def ncols(g):
            return (g + 1) * G if masked else tq

        def scores(u):
            hh, g = units[u]
            n = ncols(g)
            kb = k_ref[0, pl.ds(kstart, n), :]
            s_scr[u, :, :n] = (lax.dot_general(qh[u], kb, contract_last, preferred_element_type=F32)
                               - fk_ref[0, 0, j, hh:hh + 1, :n])

        def strips(u):
            hh, g = units[u]
            n = ncols(g)
            for r in range(G // R):
                rows = slice(r * R, (r + 1) * R)
                grow = slice(g * G + r * R, g * G + (r + 1) * R)
                s = s_scr[u, rows, :n]
                if masked:
                    s = jnp.where(_iota((R, n), 1) <= _iota((R, n), 0) + (g * G + r * R), s, NEG_INF)
                fq = fq_scr[hh, grow, :]
                m_old = m_scr[hh, grow, :]
                m_new = jnp.maximum(m_old, jnp.max(s, axis=-1, keepdims=True) + fq)
                p = jnp.exp2(s - jnp.concatenate([m_new - fq] * (n // LANES), axis=1))
                alpha = jnp.exp2(m_old - m_new)
                a_scr[hh, grow, :] = alpha
                l_scr[hh, grow, :] = alpha * l_scr[hh, grow, :] + jnp.sum(p, axis=-1, keepdims=True)
                m_scr[hh, grow, :] = m_new
                p_scr[u, rows, :n] = p.astype(BF16)

        def values(u):
            hh, g = units[u]
            n = ncols(g)
            grow = slice(g * G, (g + 1) * G)
            vb = v_ref[0, pl.ds(kstart, n), :]
            acc_scr[hh, grow, :] = (a_scr[hh, grow, :] * acc_scr[hh, grow, :]
                                    + jnp.dot(p_scr[u, :, :n], vb, preferred_element_type=F32))

        scores(0)
        for u in range(1, len(units)):
            scores(u)
            strips(u - 1)
            values(u - 1)
        strips(len(units) - 1)
        values(len(units) - 1)

    def body(j, carry):
        kv_block(j, False)
        return carry

    lax.fori_loop(0, i, body, 0)
    kv_block(i, True)
    lane = _iota((tq, LANES), 1)
    o_ref[0] = jnp.where(lane < HEAD_DIM, acc_scr[0] / l_scr[0], acc_scr[1] / l_scr[1]).astype(o_ref.dtype)


def _fox_attention(qkv, f_cum, B, S):
    tq = min(FOX_TQ, S)
    nk = S // tq
    grp = min(FOX_GROUP, tq)
    n_units = 2 * (tq // grp)
    n_pairs = N_HEADS_ATT // 2
    qkv3 = qkv.reshape(B, S, 3 * ATT_WIDTH)
    f5 = f_cum.reshape(B, n_pairs, 2, nk, tq).transpose(0, 1, 3, 2, 4)
    out = pl.pallas_call(
        functools.partial(_fox_kernel, tq=tq),
        out_shape=jax.ShapeDtypeStruct((B, S, ATT_WIDTH), BF16),
        grid=(B, n_pairs, nk),
        in_specs=[pl.BlockSpec((1, tq, LANES), lambda b, p, i: (b, i, p)),
                  pl.BlockSpec((1, S, LANES), lambda b, p, i: (b, 0, n_pairs + p)),
                  pl.BlockSpec((1, S, LANES), lambda b, p, i: (b, 0, 2 * n_pairs + p)),
                  pl.BlockSpec((1, 1, 1, 2, tq), lambda b, p, i: (b, p, i, 0, 0)),
                  pl.BlockSpec((1, 1, nk, 2, tq), lambda b, p, i: (b, p, 0, 0, 0))],
        out_specs=pl.BlockSpec((1, tq, LANES), lambda b, p, i: (b, i, p)),
        scratch_shapes=[pltpu.VMEM((n_units, grp, tq), F32), pltpu.VMEM((n_units, grp, tq), BF16)]
                       + [pltpu.VMEM((2, tq, LANES), F32) for _ in range(5)],
        compiler_params=_cparams("parallel", "parallel", "arbitrary"),
        name="fox_attention",
    )(qkv3, qkv3, qkv3, f5, f5)
    return out.reshape(B * S, ATT_WIDTH)


def _dil_kernel(slope_ref, q_ref, k_ref, v_ref, o_ref, m0_scr, m1_scr, l_scr, acc_scr, *, S):
    QB = QUERY_BLOCK
    G = DIL_GROUP
    p_idx = pl.program_id(1)
    lane = _iota((QB, LANES), 1)
    head0 = lane < HEAD_DIM
    contract_last = (((1,), (1,)), ((), ()))
    qscale = HEAD_DIM ** -0.5 * LOG2E
    slopes = [slope_ref[2 * p_idx + hh] * LOG2E for hh in range(2)]
    delta_cur = _iota((QB, QB), 0) - _iota((QB, QB), 1)
    delta_two = QB + _iota((QB, 2 * QB), 0) - _iota((QB, 2 * QB), 1)

    def rows(start, dil):
        return pl.ds(start, QB, stride=dil) if dil > 1 else pl.ds(start, QB)

    def update(q0s, kp0s, dil, biases, first):
        loaded = []
        for g, q0 in enumerate(q0s):
            rq = rows(q0, dil)
            q = q_ref[0, rq, :] * qscale
            kc = k_ref[0, rq, :]
            vc = v_ref[0, rq, :]
            if kp0s is not None:
                rp = rows(kp0s[g], dil)
                kc = jnp.concatenate([k_ref[0, rp, :], kc], axis=0)
                vc = jnp.concatenate([v_ref[0, rp, :], vc], axis=0)
            old = None if first else (m0_scr[rq, :], m1_scr[rq, :], l_scr[rq, :], acc_scr[rq, :])
            loaded.append((rq, q, kc.astype(BF16), vc.astype(BF16), old))
        units = [(g, hh) for g in range(len(loaded)) for hh in range(2)]
        reps = loaded[0][2].shape[0] // LANES
        s_u = [lax.dot_general(jnp.where(head0 == (hh == 0), loaded[g][1], 0.0).astype(BF16), loaded[g][2],
                               contract_last, preferred_element_type=F32) - biases[hh] for g, hh in units]
        m_u = [jnp.broadcast_to(jnp.max(s, axis=-1, keepdims=True), (QB, LANES)) for s in s_u]
        if not first:
            m_u = [jnp.maximum(loaded[g][4][hh], m) for (g, hh), m in zip(units, m_u)]
            a_u = [jnp.exp2(loaded[g][4][hh] - m) for (g, hh), m in zip(units, m_u)]
        p_u = [jnp.exp2(s - jnp.concatenate([m] * reps, axis=1)) for s, m in zip(s_u, m_u)]
        ps_u = [jnp.sum(p, axis=-1, keepdims=True) for p in p_u]
        pv_u = [jnp.dot(p.astype(BF16), loaded[g][3], preferred_element_type=F32)
                for (g, hh), p in zip(units, p_u)]
        for g in range(len(loaded)):
            rq, old = loaded[g][0], loaded[g][4]
            l_new = jnp.where(head0, ps_u[2 * g], ps_u[2 * g + 1])
            acc_new = jnp.where(head0, pv_u[2 * g], pv_u[2 * g + 1])
            if not first:
                alpha = jnp.where(head0, a_u[2 * g], a_u[2 * g + 1])
                l_new = alpha * old[2] + l_new
                acc_new = alpha * old[3] + acc_new
            m0_scr[rq, :] = m_u[2 * g]
            m1_scr[rq, :] = m_u[2 * g + 1]
            l_scr[rq, :] = l_new
            acc_scr[rq, :] = acc_new

    for branch, (window, dil) in enumerate(sorted(DIL_PATTERNS, key=lambda wd: -wd[1])):
        span = window // dil
        assert span <= QB and (S // dil) % QB == 0
        nblk = S // dil // QB
        first = branch == 0

        def masked_bias(delta, hh, dil=dil, span=span):
            return jnp.where((delta >= 0) & (delta <= span), slopes[hh] * (delta * dil).astype(F32), -NEG_INF)

        bias_cur = [masked_bias(delta_cur, hh) for hh in range(2)]
        bias_two = [masked_bias(delta_two, hh) for hh in range(2)]

        ga = min(G, dil)

        def head_step(t, carry, dil=dil, ga=ga, bias_cur=bias_cur, first=first):
            update([t * ga + g for g in range(ga)], None, dil, bias_cur, first)
            return carry

        lax.fori_loop(0, dil // ga, head_step, 0)

        def starts(idx, dil=dil):
            q0 = (idx % dil) + (1 + idx // dil) * (QB * dil)
            return q0, q0 - QB * dil

        def tail_step(t, carry, base=0, count=G, dil=dil, bias_two=bias_two, first=first):
            pairs = [starts(base + t * count + g) for g in range(count)]
            update([a for a, _ in pairs], [b for _, b in pairs], dil, bias_two, first)
            return carry

        n_tail = dil * (nblk - 1)
        lax.fori_loop(0, n_tail // G, tail_step, 0)
        if n_tail % G:
            tail_step(0, 0, base=(n_tail // G) * G, count=n_tail % G)
    o_ref[0] = (acc_scr[...] / l_scr[...]).astype(o_ref.dtype)


def _dilated_attention(qkv, B, S):
    n_pairs = N_HEADS_ATT // 2
    slopes = jnp.asarray(2.0 ** (-8.0 * (np.arange(N_HEADS_ATT) + 1) / N_HEADS_ATT), dtype=F32)
    qkv3 = qkv.reshape(B, S, qkv.shape[1])
    out = pl.pallas_call(
        functools.partial(_dil_kernel, S=S),
        out_shape=jax.ShapeDtypeStruct((B, S, ATT_WIDTH), BF16),
        grid=(B, n_pairs),
        in_specs=[pl.BlockSpec(memory_space=pltpu.SMEM),
                  pl.BlockSpec((1, S, LANES), lambda b, p: (b, 0, p)),
                  pl.BlockSpec((1, S, LANES), lambda b, p: (b, 0, n_pairs + p)),
                  pl.BlockSpec((1, S, LANES), lambda b, p: (b, 0, 2 * n_pairs + p))],
        out_specs=pl.BlockSpec((1, S, LANES), lambda b, p: (b, 0, p)),
        scratch_shapes=[pltpu.VMEM((S, LANES), F32) for _ in range(4)],
        compiler_params=_cparams("parallel", "parallel"),
        name="dilated_attention",
    )(slopes, qkv3, qkv3, qkv3)
    return out.reshape(B * S, ATT_WIDTH)


def _attn_out_kernel(a0_ref, a1_ref, w_ref, x_ref, g_ref, b_ref, y_ref, ybf_ref):
    k0 = a0_ref.shape[1]
    mix = jnp.dot(a0_ref[...], w_ref[:k0, :], preferred_element_type=F32)
    mix = mix + jnp.dot(a1_ref[...], w_ref[k0:, :], preferred_element_type=F32)
    y = _layer_norm_rows(DEEPNORM_ALPHA * x_ref[...] + mix, g_ref[...], b_ref[...])
    y_ref[...] = y
    ybf_ref[...] = y.astype(BF16)


def _attn_out_ln(a0, a1, w, x, g, b):
    M, D = x.shape
    tm = min(LN_TM, M)
    row = lambda i: (i, 0)
    const = lambda i: (0, 0)
    return pl.pallas_call(
        _attn_out_kernel,
        out_shape=(jax.ShapeDtypeStruct((M, D), F32), jax.ShapeDtypeStruct((M, D), BF16)),
        grid=(M // tm,),
        in_specs=[pl.BlockSpec((tm, a0.shape[1]), row), pl.BlockSpec((tm, a1.shape[1]), row),
                  pl.BlockSpec(w.shape, const), pl.BlockSpec((tm, D), row),
                  pl.BlockSpec((1, D), const), pl.BlockSpec((1, D), const)],
        out_specs=(pl.BlockSpec((tm, D), row), pl.BlockSpec((tm, D), row)),
        compiler_params=_cparams("parallel"),
        name="attn_out_ln",
    )(a0, a1, w, x, g.reshape(1, D), b.reshape(1, D))


def _gdn_out_kernel(a_ref, w_ref, x_ref, g_ref, b_ref, r_ref, yt_ref, route_ref):
    tm = x_ref.shape[0]
    mix = jnp.dot(a_ref[...], w_ref[...], preferred_element_type=F32)
    y = _layer_norm_rows(DEEPNORM_ALPHA * x_ref[...] + mix, g_ref[...], b_ref[...])
    for s in range(SUBLANES):
        yt_ref[pl.ds(s, tm, stride=SUBLANES), :] = y[:, s * LANES:(s + 1) * LANES]
    y_hi = y.astype(BF16)
    y_lo = (y - y_hi.astype(F32)).astype(BF16)
    both = jnp.dot(y_hi, r_ref[...], preferred_element_type=F32)
    logits = both[:, :LANES] + (both[:, LANES:] + jnp.dot(y_lo, r_ref[:, :LANES], preferred_element_type=F32))
    lane = _iota((tm, LANES), 1)
    logits = jnp.where(lane < N_EXPERTS, logits, -jnp.inf)
    m1 = jnp.max(logits, axis=-1, keepdims=True)
    i1 = jnp.min(jnp.where(logits == m1, lane, LANES), axis=-1, keepdims=True)
    rest = jnp.where(lane == i1, -jnp.inf, logits)
    m2 = jnp.max(rest, axis=-1, keepdims=True)
    i2 = jnp.min(jnp.where(rest == m2, lane, LANES), axis=-1, keepdims=True)
    e2 = jnp.exp(m2 - m1)
    w1 = 1.0 / (1.0 + e2)
    w2 = e2 / (1.0 + e2)
    route = jnp.where(lane == 0, i1.astype(F32),
                      jnp.where(lane == 1, i2.astype(F32),
                                jnp.where(lane == 2, w1, jnp.where(lane == 3, w2, 0.0))))
    route_ref[...] = route


def _gdn_out_ln_route(a, w, x, g, b, router):
    M, D = x.shape
    tm = min(LN_TM, M)
    row = lambda i: (i, 0)
    const = lambda i: (0, 0)
    r_pad = jnp.zeros((D, LANES), F32).at[:, :N_EXPERTS].set(router.astype(F32))
    r_hi = r_pad.astype(BF16)
    r_lo = (r_pad - r_hi.astype(F32)).astype(BF16)
    r_pad = jnp.concatenate([r_hi, r_lo], axis=1)
    return pl.pallas_call(
        _gdn_out_kernel,
        out_shape=(jax.ShapeDtypeStruct((M * SUBLANES, LANES), F32),
                   jax.ShapeDtypeStruct((M, LANES), F32)),
        grid=(M // tm,),
        in_specs=[pl.BlockSpec((tm, a.shape[1]), row), pl.BlockSpec(w.shape, const),
                  pl.BlockSpec((tm, D), row), pl.BlockSpec((1, D), const),
                  pl.BlockSpec((1, D), const), pl.BlockSpec((D, 2 * LANES), const)],
        out_specs=(pl.BlockSpec((tm * SUBLANES, LANES), row), pl.BlockSpec((tm, LANES), row)),
        compiler_params=_cparams("parallel"),
        name="gdn_out_ln_route",
    )(a, w, x, g.reshape(1, D), b.reshape(1, D), r_pad)


def _ffn_dense_kernel(xbf_ref, wg_ref, wu_ref, wd_ref, x_ref, g_ref, b_ref, y_ref, ybf_ref, acc_scr):
    j = pl.program_id(1)

    @pl.when(j == 0)
    def _():
        acc_scr[...] = jnp.zeros_like(acc_scr)

    x = xbf_ref[...]
    hg = jnp.dot(x, wg_ref[...].astype(BF16), preferred_element_type=F32)
    hu = jnp.dot(x, wu_ref[...].astype(BF16), preferred_element_type=F32)
    h = (_silu(hg) * hu).astype(BF16)
    acc_scr[...] += jnp.dot(h, wd_ref[...].astype(BF16), preferred_element_type=F32)

    @pl.when(j == pl.num_programs(1) - 1)
    def _():
        y = _layer_norm_rows(DEEPNORM_ALPHA * x_ref[...] + acc_scr[...], g_ref[...], b_ref[...])
        y_ref[...] = y
        ybf_ref[...] = y.astype(BF16)


def _ffn_dense_ln(xbf, x, wg, wu, wd, g, b):
    M, D = x.shape
    FF = wg.shape[1]
    tm = min(FFN_TM, M)
    tf = FFN_TF
    assert FF % tf == 0
    row = lambda i, j: (i, 0)
    const = lambda i, j: (0, 0)
    return pl.pallas_call(
        _ffn_dense_kernel,
        out_shape=(jax.ShapeDtypeStruct((M, D), F32), jax.ShapeDtypeStruct((M, D), BF16)),
        grid=(M // tm, FF // tf),
        in_specs=[pl.BlockSpec((tm, D), row),
                  pl.BlockSpec((D, tf), lambda i, j: (0, j)),
                  pl.BlockSpec((D, tf), lambda i, j: (0, j)),
                  pl.BlockSpec((tf, D), lambda i, j: (j, 0)),
                  pl.BlockSpec((tm, D), row),
                  pl.BlockSpec((1, D), const), pl.BlockSpec((1, D), const)],
        out_specs=(pl.BlockSpec((tm, D), row), pl.BlockSpec((tm, D), row)),
        scratch_shapes=[pltpu.VMEM((tm, D), F32)],
        compiler_params=_cparams("parallel", "arbitrary"),
        name="ffn_dense_ln",
    )(xbf, wg, wu, wd, x, g.reshape(1, D), b.reshape(1, D))


def _ffn_grouped_kernel(te_ref, na_ref, xs_hbm, wg_ref, wu_ref, wd_ref, ys_hbm,
                        xin_scr, yout_scr, xbf_scr, acc_scr, sem_in, sem_out):
    i = pl.program_id(0)
    j = pl.program_id(1)
    nj = pl.num_programs(1)
    tm = acc_scr.shape[0]
    rows = tm * SUBLANES
    na = na_ref[0]
    active = i < na
    slot = i % 2

    def x_copy(tile, sl):
        src = xs_hbm.at[pl.ds(pl.multiple_of(tile * rows, rows), rows), :]
        return pltpu.make_async_copy(src, xin_scr.at[sl], sem_in.at[sl])

    def y_copy(tile, sl):
        dst = ys_hbm.at[pl.ds(pl.multiple_of(tile * rows, rows), rows), :]
        return pltpu.make_async_copy(yout_scr.at[sl], dst, sem_out.at[sl])

    @pl.when(active & (j == 0))
    def _():
        @pl.when(i == 0)
        def _():
            x_copy(0, 0).start()

        x_copy(i, slot).wait()

        @pl.when(i + 1 < na)
        def _():
            x_copy(i + 1, 1 - slot).start()

        acc_scr[...] = jnp.zeros_like(acc_scr)
        for s in range(SUBLANES):
            xbf_scr[:, s * LANES:(s + 1) * LANES] = xin_scr[slot, pl.ds(s, tm, stride=SUBLANES), :].astype(BF16)

    @pl.when(active)
    def _():
        x = xbf_scr[...]
        hg = jnp.dot(x, wg_ref[0].astype(BF16), preferred_element_type=F32)
        hu = jnp.dot(x, wu_ref[0].astype(BF16), preferred_element_type=F32)
        h = (_silu(hg) * hu).astype(BF16)
        acc_scr[...] += jnp.dot(h, wd_ref[0].astype(BF16), preferred_element_type=F32)

    @pl.when(active & (j == nj - 1))
    def _():
        @pl.when(i >= 2)
        def _():
            y_copy(i - 2, slot).wait()

        for s in range(SUBLANES):
            yout_scr[slot, pl.ds(s, tm, stride=SUBLANES), :] = acc_scr[:, s * LANES:(s + 1) * LANES]
        y_copy(i, slot).start()

        @pl.when(i == na - 1)
        def _():
            @pl.when(i >= 1)
            def _():
                y_copy(i - 1, 1 - slot).wait()

            y_copy(i, slot).wait()

    @pl.when(jnp.logical_not(active) & (j == 0))
    def _():
        yout_scr[0] = jnp.zeros(yout_scr.shape[1:], F32)
        y_copy(i, 0).start()
        y_copy(i, 0).wait()


def _ffn_grouped(xt, tile_expert, n_active, wg, wu, wd, tm):
    R = xt.shape[0] // SUBLANES
    E, D, FF = wg.shape
    tf = FFN_TF
    nf = FF // tf
    n_tiles = R // tm

    def ff_idx(i, j, na):
        return jnp.where(i < na[0], j, nf - 1)

    return pl.pallas_call(
        _ffn_grouped_kernel,
        out_shape=jax.ShapeDtypeStruct((R * SUBLANES, LANES), F32),
        grid_spec=pltpu.PrefetchScalarGridSpec(
            num_scalar_prefetch=2,
            grid=(n_tiles, nf),
            in_specs=[pl.BlockSpec(memory_space=pl.ANY),
                      pl.BlockSpec((1, D, tf), lambda i, j, te, na: (te[i], 0, ff_idx(i, j, na))),
                      pl.BlockSpec((1, D, tf), lambda i, j, te, na: (te[i], 0, ff_idx(i, j, na))),
                      pl.BlockSpec((1, tf, D), lambda i, j, te, na: (te[i], ff_idx(i, j, na), 0))],
            out_specs=pl.BlockSpec(memory_space=pl.ANY),
            scratch_shapes=[pltpu.VMEM((2, tm * SUBLANES, LANES), F32), pltpu.VMEM((2, tm * SUBLANES, LANES), F32),
                            pltpu.VMEM((tm, D), BF16), pltpu.VMEM((tm, D), F32),
                            pltpu.SemaphoreType.DMA((2,)), pltpu.SemaphoreType.DMA((2,))]),
        compiler_params=_cparams("arbitrary", "arbitrary"),
        name="ffn_grouped",
    )(tile_expert, n_active, xt, wg, wu, wd)


def _row_tile(ref, r):
    return ref.at[pl.ds(pl.multiple_of(r, SUBLANES), SUBLANES), :]


def _dispatch_kernel(dest_ref, xt_ref, xs_in_hbm, xs_hbm, sem):
    del xs_in_hbm
    ch = dest_ref.shape[2] // 2

    def body(t, carry):
        src = _row_tile(xt_ref, t * SUBLANES)
        for k in range(2):
            pltpu.make_async_copy(src, _row_tile(xs_hbm, dest_ref[0, 0, 2 * t + k]), sem).start()
        return carry

    lax.fori_loop(0, ch, body, 0, unroll=8)
    for _ in range(2):
        pltpu.make_async_copy(xt_ref, xs_hbm.at[pl.ds(0, ch * SUBLANES), :], sem).wait()


def _dispatch_rows(xt, dest8, xs_init):
    N = xt.shape[0] // SUBLANES
    ch = min(DISPATCH_CHUNK, N)
    dest3 = dest8.reshape(N // ch, 1, 2 * ch)
    return pl.pallas_call(
        _dispatch_kernel,
        out_shape=jax.ShapeDtypeStruct(xs_init.shape, F32),
        grid=(N // ch,),
        in_specs=[pl.BlockSpec((1, 1, 2 * ch), lambda i: (i, 0, 0), memory_space=pltpu.SMEM),
                  pl.BlockSpec((ch * SUBLANES, LANES), lambda i: (i, 0)),
                  pl.BlockSpec(memory_space=pl.ANY)],
        out_specs=pl.BlockSpec(memory_space=pl.ANY),
        scratch_shapes=[pltpu.SemaphoreType.DMA(())],
        input_output_aliases={2: 0},
        compiler_params=_cparams("arbitrary"),
        name="moe_dispatch",
    )(dest3, xt, xs_init)


def _moe_ln_kernel(dest_ref, dnext_ref, xt_ref, ys_hbm, route_ref, g_ref, b_ref, o_ref, ya_scr, yb_scr, sem):
    tm = o_ref.shape[0]
    i = pl.program_id(0)
    slot = i % 2

    def start_gathers(idx_ref, sl):
        def body(t, carry):
            pltpu.make_async_copy(_row_tile(ys_hbm, idx_ref[0, 0, 2 * t]),
                                  _row_tile(ya_scr.at[sl], t * SUBLANES), sem.at[sl]).start()
            pltpu.make_async_copy(_row_tile(ys_hbm, idx_ref[0, 0, 2 * t + 1]),
                                  _row_tile(yb_scr.at[sl], t * SUBLANES), sem.at[sl]).start()
            return carry

        lax.fori_loop(0, tm, body, 0, unroll=8)

    @pl.when(i == 0)
    def _():
        start_gathers(dest_ref, 0)

    @pl.when(i + 1 < pl.num_programs(0))
    def _():
        start_gathers(dnext_ref, 1 - slot)

    for scr in (ya_scr, yb_scr):
        pltpu.make_async_copy(ys_hbm.at[pl.ds(0, tm * SUBLANES), :], scr.at[slot], sem.at[slot]).wait()

    w1 = jnp.broadcast_to(route_ref[:, 2:3], (tm, LANES))
    w2 = jnp.broadcast_to(route_ref[:, 3:4], (tm, LANES))
    parts = []
    for s in range(SUBLANES):
        rows = pl.ds(s, tm, stride=SUBLANES)
        parts.append(DEEPNORM_ALPHA * xt_ref[rows, :]
                     + (w1 * ya_scr[slot, rows, :] + w2 * yb_scr[slot, rows, :]))
    d_model = SUBLANES * LANES
    mu = jnp.sum(sum(parts), axis=-1, keepdims=True) / d_model
    var = jnp.sum(sum((z - mu) * (z - mu) for z in parts), axis=-1, keepdims=True) / d_model
    rstd = lax.rsqrt(var + LN_EPS)
    for s in range(SUBLANES):
        cols = slice(s * LANES, (s + 1) * LANES)
        o_ref[:, cols] = (parts[s] - mu) * rstd * g_ref[:, cols] + b_ref[:, cols]


def _moe_combine_ln(xt, ys, dest8, route, g, b):
    M = route.shape[0]
    D = SUBLANES * LANES
    tm = min(LN_TM, M)
    row = lambda i: (i, 0)
    const = lambda i: (0, 0)
    n_blocks = M // tm
    dest3 = dest8.reshape(n_blocks, 1, 2 * tm)
    return pl.pallas_call(
        _moe_ln_kernel,
        out_shape=jax.ShapeDtypeStruct((M, D), F32),
        grid=(n_blocks,),
        in_specs=[pl.BlockSpec((1, 1, 2 * tm), lambda i: (i, 0, 0), memory_space=pltpu.SMEM),
                  pl.BlockSpec((1, 1, 2 * tm), lambda i: (jnp.minimum(i + 1, n_blocks - 1), 0, 0),
                               memory_space=pltpu.SMEM),
                  pl.BlockSpec((tm * SUBLANES, LANES), row),
                  pl.BlockSpec(memory_space=pl.ANY),
                  pl.BlockSpec((tm, LANES), row),
                  pl.BlockSpec((1, D), const), pl.BlockSpec((1, D), const)],
        out_specs=pl.BlockSpec((tm, D), row),
        scratch_shapes=[pltpu.VMEM((2, tm * SUBLANES, LANES), F32), pltpu.VMEM((2, tm * SUBLANES, LANES), F32),
                        pltpu.SemaphoreType.DMA((2,))],
        compiler_params=_cparams("arbitrary"),
        name="moe_combine_ln",
    )(dest3, dest3, xt, ys, route, g.reshape(1, D), b.reshape(1, D))


def _gdn_kernel(alog_ref, dt_ref, pq_ref, pk_ref, pv_ref, hq_ref, hk_ref, hv_ref,
                cq_ref, ck_ref, cv_ref, br_ref, ar_ref, ng_ref, gate_ref, o_ref,
                state_scr, sq_scr, sk_scr, sv_scr, *, blk, hp):
    C = GDN_CHUNK
    Dh = GDN_HEAD_DIM
    nchunk = blk // C
    h0 = pl.program_id(1) * hp
    sb = pl.program_id(2)
    heads = range(hp)

    @pl.when(sb == 0)
    def _():
        state_scr[...] = jnp.zeros_like(state_scr)

    have_prev = (sb > 0).astype(F32)

    def conv_silu(cur_ref, halo_ref, w_ref, stage_scr):
        outs = []
        for c in range(stage_scr.shape[0]):
            lanes = slice(c * LANES, (c + 1) * LANES)
            stage_scr[c, pl.ds(0, SUBLANES, stride=2), :] = halo_ref[0, :, lanes] * have_prev
            stage_scr[c, pl.ds(2 * SUBLANES, blk, stride=2), :] = cur_ref[0, :, lanes]
            out = None
            for j in range(GDN_CONV):
                off = 2 * (SUBLANES - (GDN_CONV - 1) + j)
                term = w_ref[j:j + 1, lanes] * stage_scr[c, pl.ds(off, blk, stride=2), :]
                out = term if out is None else out + term
            outs.append(out)
        return _silu(jnp.concatenate(outs, axis=1))

    def split(t):
        return [t[:, hh * Dh:(hh + 1) * Dh] for hh in heads]

    def l2n(t):
        return t * lax.rsqrt(jnp.sum(t * t, axis=-1, keepdims=True) + RMS_EPS)

    q_h = [l2n(t) * (Dh ** -0.5) for t in split(conv_silu(pq_ref, hq_ref, cq_ref, sq_scr))]
    k_h = [l2n(t) for t in split(conv_silu(pk_ref, hk_ref, ck_ref, sk_scr))]
    v_h = split(conv_silu(pv_ref, hv_ref, cv_ref, sv_scr))

    lanes_row = _iota((hp, blk), 0)
    dt_rows = jnp.zeros((hp, blk), F32)
    alog_rows = jnp.zeros((hp, blk), F32)
    for hh in heads:
        dt_rows = jnp.where(lanes_row == hh, dt_ref[h0 + hh], dt_rows)
        alog_rows = jnp.where(lanes_row == hh, alog_ref[h0 + hh], alog_rows)
    beta_rows = jax.nn.sigmoid(br_ref[0, 0, 0])
    za = ar_ref[0, 0, 0] + dt_rows
    g_rows = -jnp.exp(alog_rows) * (jnp.maximum(za, 0.0) + jnp.log1p(jnp.exp(-jnp.abs(za))))
    ri = _iota((blk, blk), 0)
    ci = _iota((blk, blk), 1)
    same = (ri // C) == (ci // C)
    g8 = jnp.concatenate([g_rows, jnp.zeros((SUBLANES - hp, blk), F32)], axis=0) if hp < SUBLANES else g_rows
    gam_rows = jnp.dot(g8, (same & (ri <= ci)).astype(F32), precision=HIGHEST,
                       preferred_element_type=F32)
    gl_rows = jnp.dot(g8, same.astype(F32), precision=HIGHEST,
                      preferred_element_type=F32)
    beta = [_row_to_col(beta_rows[hh:hh + 1, :]) for hh in heads]
    gam = [_row_to_col(gam_rows[hh:hh + 1, :]) for hh in heads]
    gl = [_row_to_col(gl_rows[hh:hh + 1, :]) for hh in heads]
    eg = [jnp.exp(t) for t in gam]
    ekd = [jnp.exp(a - b) for a, b in zip(gl, gam)]

    incl = same & (ri >= ci)
    strict = same & (ri > ci)
    contract_last = (((1,), (1,)), ((), ()))
    decay = [jnp.where(incl, jnp.exp(jnp.where(incl, gam[hh] - gam_rows[hh:hh + 1, :], 0.0)), 0.0)
             for hh in heads]
    kb = [t.astype(BF16) for t in k_h]
    kk = [lax.dot_general(t, t, contract_last, preferred_element_type=F32) for t in kb]
    x_acc = [jnp.where(strict, -(beta[hh] * kk[hh] * decay[hh]), 0.0) for hh in heads]
    pw = x_acc
    for _ in range(int(np.log2(C)) - 1):
        pwb = [t.astype(BF16) for t in pw]
        pw = [jnp.dot(t, t, preferred_element_type=F32) for t in pwb]
        x_acc = [x + p + jnp.dot(p.astype(BF16), x.astype(BF16), preferred_element_type=F32)
                 for x, p in zip(x_acc, pw)]
    rhs = [jnp.concatenate([v_h[hh] * beta[hh], k_h[hh] * (beta[hh] * eg[hh])], axis=1) for hh in heads]
    sol = [r + jnp.dot(x.astype(BF16), r.astype(BF16), preferred_element_type=F32)
           for x, r in zip(x_acc, rhs)]
    u = [t[:, :Dh] for t in sol]
    w_b = [t[:, Dh:].astype(BF16) for t in sol]
    qk = [lax.dot_general(q_h[hh].astype(BF16), kb[hh], contract_last, preferred_element_type=F32) * decay[hh]
          for hh in heads]
    q_dec = [(q_h[hh] * eg[hh]).astype(BF16) for hh in heads]
    k_dec = [(k_h[hh] * ekd[hh]).astype(BF16) for hh in heads]

    state = [state_scr[hh] for hh in heads]
    v_new = [[] for _ in heads]
    o_inter = [[] for _ in heads]
    for c in range(nchunk):
        rows = slice(c * C, (c + 1) * C)
        sbf = [t.astype(BF16) for t in state]
        vn = [u[hh][rows] - jnp.dot(w_b[hh][rows], sbf[hh], preferred_element_type=F32) for hh in heads]
        for hh in heads:
            o_inter[hh].append(jnp.dot(q_dec[hh][rows], sbf[hh], preferred_element_type=F32))
            v_new[hh].append(vn[hh])
        state = [state[hh] * jnp.exp(gl[hh][c * C:c * C + 1, :])
                 + lax.dot_general(k_dec[hh][rows], vn[hh].astype(BF16), (((0,), (0,)), ((), ())),
                                   preferred_element_type=F32) for hh in heads]
    for hh in heads:
        state_scr[hh] = state[hh]
    o = [jnp.concatenate(o_inter[hh], axis=0)
         + jnp.dot(qk[hh].astype(BF16), jnp.concatenate(v_new[hh], axis=0).astype(BF16),
                   preferred_element_type=F32) for hh in heads]
    o = [t * lax.rsqrt(jnp.mean(t * t, axis=-1, keepdims=True) + RMS_EPS) * ng_ref[...] for t in o]
    o = o[0] if hp == 1 else jnp.concatenate(o, axis=1)
    o_ref[0] = (o * _silu(gate_ref[0])).astype(o_ref.dtype)


def _gated_deltanet(pre, ab_rows, gate, conv_w, a_log, dt_bias, norm_g, B, S):
    blk = min(GDN_BLK, S)
    H = N_HEADS_GDN
    hp = GDN_HEADS_PER_STEP
    hg = H // hp
    wide = hp * GDN_HEAD_DIM
    pre3 = pre.reshape(B, S, pre.shape[1])
    gate3 = gate.reshape(B, S, GDN_WIDTH)
    ab5 = ab_rows.reshape(B, 2, hg, hp, S)
    halo_blocks = blk // SUBLANES

    def cur(sec):
        return pl.BlockSpec((1, blk, wide), lambda b, h, s: (b, s, sec * hg + h))

    def halo(sec):
        return pl.BlockSpec((1, SUBLANES, wide),
                            lambda b, h, s: (b, jnp.maximum(s * halo_blocks - 1, 0), sec * hg + h))

    def cw(sec):
        return pl.BlockSpec((GDN_CONV, wide), lambda b, h, s: (0, sec * hg + h))

    smem = pl.BlockSpec(memory_space=pltpu.SMEM)
    out = pl.pallas_call(
        functools.partial(_gdn_kernel, blk=blk, hp=hp),
        out_shape=jax.ShapeDtypeStruct((B, S, GDN_WIDTH), BF16),
        grid=(B, hg, S // blk),
        in_specs=[smem, smem,
                  cur(0), cur(1), cur(2), halo(0), halo(1), halo(2),
                  cw(0), cw(1), cw(2),
                  pl.BlockSpec((1, 1, 1, hp, blk), lambda b, h, s: (b, 0, h, 0, s)),
                  pl.BlockSpec((1, 1, 1, hp, blk), lambda b, h, s: (b, 1, h, 0, s)),
                  pl.BlockSpec((1, LANES), lambda b, h, s: (0, 0)),
                  pl.BlockSpec((1, blk, wide), lambda b, h, s: (b, s, h))],
        out_specs=pl.BlockSpec((1, blk, wide), lambda b, h, s: (b, s, h)),
        scratch_shapes=[pltpu.VMEM((hp, GDN_HEAD_DIM, GDN_HEAD_DIM), F32)]
                       + [pltpu.VMEM((wide // LANES, 2 * (blk + SUBLANES), LANES), F32) for _ in range(3)],
        compiler_params=_cparams("parallel", "parallel", "arbitrary"),
        name="gated_deltanet",
    )(a_log.astype(F32), dt_bias.astype(F32), pre3, pre3, pre3, pre3, pre3, pre3,
      conv_w, conv_w, conv_w, ab5, ab5, norm_g.reshape(1, LANES).astype(F32), gate3)
    return out.reshape(B * S, GDN_WIDTH)


def _pad_cols(w, width):
    return jnp.zeros((w.shape[0], width), w.dtype).at[:, :w.shape[1]].set(w)


def _attention_layer(x, xbf, B, S, w_in, forget_bias, w_out, ln_g, ln_b,
                     w_gate, w_up, w_down, ln2_g, ln2_b):
    W = ATT_WIDTH
    H = N_HEADS_ATT
    w_fox = jnp.concatenate([w_in[:, :W] * (HEAD_DIM ** -0.5 * LOG2E), w_in[:, W:3 * W]], axis=1).astype(BF16)
    w_dil = jnp.concatenate([w_in[:, 3 * W + H:], _pad_cols(w_in[:, 3 * W:3 * W + H], LANES)], axis=1).astype(BF16)
    qkv_fox = _proj(xbf, w_fox, BF16)
    qkv_dil = _proj(xbf, w_dil, F32)
    f_rows = qkv_dil[:, 3 * W:3 * W + H].reshape(B, S, H).transpose(0, 2, 1)
    f_cum = _forget_cumsum(f_rows, forget_bias)
    o_fox = _fox_attention(qkv_fox, f_cum, B, S)
    o_dil = _dilated_attention(qkv_dil, B, S)
    x1, x1bf = _attn_out_ln(o_fox, o_dil, w_out.astype(BF16), x, ln_g, ln_b)
    return _ffn_dense_ln(x1bf, x1, w_gate, w_up, w_down, ln2_g, ln2_b)


def _slot_indices(route, tm):
    N = route.shape[0]
    experts = route[:, 0:2].astype(jnp.int32).reshape(2 * N)
    onehot = (experts[:, None] == jnp.arange(N_EXPERTS, dtype=jnp.int32)[None, :]).astype(jnp.int32)
    csum = jnp.cumsum(onehot, axis=0)
    counts = csum[-1]
    padded = ((counts + tm - 1) // tm) * tm
    ends = jnp.cumsum(padded)
    starts = ends - padded
    dest = jnp.sum(onehot * (csum - 1 + starts[None, :]), axis=1).astype(jnp.int32)
    n_tiles = (2 * N) // tm + N_EXPERTS
    tile_start = jnp.arange(n_tiles, dtype=jnp.int32) * tm
    tile_expert = jnp.minimum(jnp.sum((tile_start[:, None] >= ends[None, :]).astype(jnp.int32), axis=1),
                              N_EXPERTS - 1).astype(jnp.int32)
    n_active = (ends[-1] // tm).astype(jnp.int32).reshape(1)
    return dest, tile_expert, n_active, n_tiles


def _deltanet_layer(x, xbf, B, S, w_in, conv_w, a_log, dt_bias, norm_g, w_out, ln_g, ln_b,
                    router, w_gate, w_up, w_down, ln2_g, ln2_b):
    N = B * S
    W = GDN_WIDTH
    H = N_HEADS_GDN
    w_qkv = jnp.concatenate([w_in[:, :3 * W], _pad_cols(w_in[:, 3 * W:3 * W + 2 * H], 2 * LANES)], axis=1).astype(BF16)
    w_gt = w_in[:, 3 * W + 2 * H:].astype(BF16)
    pre = _proj(xbf, w_qkv, F32)
    gate = _proj(xbf, w_gt, F32)
    ab_rows = pre[:, 3 * W:3 * W + 2 * H].reshape(B, S, 2 * H).transpose(0, 2, 1).reshape(B, 2 * H, 1, S)
    o = _gated_deltanet(pre, ab_rows, gate, conv_w.astype(F32), a_log, dt_bias, norm_g, B, S)
    xt, route = _gdn_out_ln_route(o, w_out.astype(BF16), x, ln_g, ln_b, router)

    tm = min(FFN_TM, N)
    dest, tile_expert, n_active, n_tiles = _slot_indices(route, tm)
    dest8 = dest * SUBLANES
    xs_init = jnp.zeros((n_tiles * tm * SUBLANES, LANES), F32)
    xs = _dispatch_rows(xt, dest8, xs_init)
    ys = _ffn_grouped(xs, tile_expert, n_active,
                      w_gate, w_up, w_down, tm)
    return _moe_combine_ln(xt, ys, dest8, route, ln2_g, ln2_b)


def kernel(x, attn_w_in, fox_forget_bias, attn_w_out, ln_attn_g, ln_attn_b, ffn_w_gate, ffn_w_up,
           ffn_w_down, ln_ffn_g, ln_ffn_b, gdn_w_in, gdn_conv_w, gdn_a_log, gdn_dt_bias, gdn_norm_g,
           gdn_w_out, ln_gdn_g, ln_gdn_b, moe_router, moe_w_gate, moe_w_up, moe_w_down, ln_moe_g,
           ln_moe_b):
    B, S, D = x.shape
    x2 = x.reshape(B * S, D)
    x2bf = x2.astype(BF16)
    x2, x2bf = _attention_layer(x2, x2bf, B, S, attn_w_in[0], fox_forget_bias[0], attn_w_out[0],
                                ln_attn_g[0], ln_attn_b[0], ffn_w_gate[0], ffn_w_up[0], ffn_w_down[0],
                                ln_ffn_g[0], ln_ffn_b[0])
    y = _deltanet_layer(x2, x2bf, B, S, gdn_w_in[0], gdn_conv_w[0], gdn_a_log[0], gdn_dt_bias[0],
                        gdn_norm_g[0], gdn_w_out[0], ln_gdn_g[0], ln_gdn_b[0], moe_router[0],
                        moe_w_gate[0], moe_w_up[0], moe_w_down[0], ln_moe_g[0], ln_moe_b[0])
    return y.reshape(B, S, D)
```

```python
import functools

import numpy as np
import jax
import jax.numpy as jnp
from jax import lax
from jax.experimental import pallas as pl
from jax.experimental.pallas import tpu as pltpu

F32 = jnp.float32
BF16 = jnp.bfloat16
HIGHEST = lax.Precision.HIGHEST

LANES = 128
SUBLANES = 8
VMEM_LIMIT = 52 * 1024 * 1024

HEAD_DIM = 64
N_HEADS_ATT = 8
ATT_WIDTH = N_HEADS_ATT * HEAD_DIM
QUERY_BLOCK = 128
DIL_PATTERNS = ((128, 1), (512, 4), (2048, 16))
GDN_HEAD_DIM = 128
N_HEADS_GDN = 8
GDN_WIDTH = N_HEADS_GDN * GDN_HEAD_DIM
GDN_CONV = 4
GDN_CHUNK = 64
N_EXPERTS = 8
DEPTH = 2
DEEPNORM_ALPHA = (2.0 * DEPTH) ** 0.25
LN_EPS = 1e-5
RMS_EPS = 1e-6
NEG_INF = -1e30
LOG2E = 1.4426950408889634

PROJ_TM = 1024
PROJ_TN = 1664
FOX_TQ = 1024
FOX_GROUP = 512
FOX_STRIP = 32
DIL_GROUP = 8
LN_TM = 512
FFN_TM = 1024
FFN_TF = 512
GDN_BLK = 128
GDN_HEADS_PER_STEP = 8
DISPATCH_CHUNK = 512


def _cparams(*sem):
    return pltpu.CompilerParams(dimension_semantics=sem, vmem_limit_bytes=VMEM_LIMIT)


def _iota(shape, dim):
    return lax.broadcasted_iota(jnp.int32, shape, dim)


def _silu(x):
    return x * jax.nn.sigmoid(x)


def _row_to_col(row):
    n = row.shape[1]
    eye = _iota((LANES, LANES), 0) == _iota((LANES, LANES), 1)
    cols = []
    for c in range(n // LANES):
        seg = row[:, c * LANES:(c + 1) * LANES]
        cols.append(jnp.sum(jnp.where(eye, seg, 0.0), axis=1, keepdims=True))
    return cols[0] if len(cols) == 1 else jnp.concatenate(cols, axis=0)


def _layer_norm_rows(z, g, b):
    mu = jnp.mean(z, axis=-1, keepdims=True)
    zc = z - mu
    var = jnp.mean(zc * zc, axis=-1, keepdims=True)
    return zc * lax.rsqrt(var + LN_EPS) * g + b


def _proj_kernel(x_ref, w_ref, o_ref):
    o_ref[...] = jnp.dot(x_ref[...], w_ref[...], preferred_element_type=F32).astype(o_ref.dtype)


def _proj(x, w, out_dtype):
    M, K = x.shape
    C = w.shape[1]
    tm = min(PROJ_TM, M)
    tn = min(PROJ_TN, C)
    assert M % tm == 0 and C % tn == 0
    return pl.pallas_call(
        _proj_kernel,
        out_shape=jax.ShapeDtypeStruct((M, C), out_dtype),
        grid=(M // tm, C // tn),
        in_specs=[pl.BlockSpec((tm, K), lambda i, j: (i, 0)),
                  pl.BlockSpec((K, tn), lambda i, j: (0, j))],
        out_specs=pl.BlockSpec((tm, tn), lambda i, j: (i, j)),
        compiler_params=_cparams("parallel", "parallel"),
        name="proj",
    )(x, w)


def _forget_cumsum_kernel(f_ref, b_ref, o_ref):
    S = f_ref.shape[2]
    z = f_ref[0] + b_ref[...]
    lf = (jnp.minimum(z, 0.0) - jnp.log1p(jnp.exp(-jnp.abs(z)))) * LOG2E
    upper = (_iota((LANES, LANES), 0) <= _iota((LANES, LANES), 1)).astype(F32)
    carry = jnp.zeros((z.shape[0], 1), F32)
    for c in range(S // LANES):
        seg = jnp.dot(lf[:, c * LANES:(c + 1) * LANES], upper, precision=HIGHEST,
                      preferred_element_type=F32) + carry
        o_ref[0, :, c * LANES:(c + 1) * LANES] = seg
        carry = seg[:, LANES - 1:LANES]


def _forget_cumsum(f_rows, bias):
    B, H, S = f_rows.shape
    return pl.pallas_call(
        _forget_cumsum_kernel,
        out_shape=jax.ShapeDtypeStruct((B, H, S), F32),
        grid=(B,),
        in_specs=[pl.BlockSpec((1, H, S), lambda b: (b, 0, 0)),
                  pl.BlockSpec((H, 1), lambda b: (0, 0))],
        out_specs=pl.BlockSpec((1, H, S), lambda b: (b, 0, 0)),
        compiler_params=_cparams("parallel"),
        name="forget_cumsum",
    )(f_rows, bias.reshape(H, 1).astype(F32))


def _fox_kernel(q_ref, k_ref, v_ref, fq_ref, fk_ref, o_ref,
                s_scr, p_scr, m_scr, l_scr, a_scr, acc_scr, fq_scr, *, tq):
    G = min(FOX_GROUP, tq)
    R = FOX_STRIP
    i = pl.program_id(2)
    contract_last = (((1,), (1,)), ((), ()))
    units = [(hh, g) for hh in range(2) for g in range(tq // G)]
    lane_g = _iota((G, LANES), 1)
    qh = []
    for hh, g in units:
        qg = q_ref[0, g * G:(g + 1) * G, :]
        qh.append(jnp.where((lane_g < HEAD_DIM) == (hh == 0), qg, jnp.zeros_like(qg)))
    for hh in range(2):
        fq_scr[hh] = jnp.broadcast_to(_row_to_col(fq_ref[0, 0, 0, hh:hh + 1, :]), (tq, LANES))
    m_scr[...] = jnp.full(m_scr.shape, NEG_INF, F32)
    l_scr[...] = jnp.zeros(l_scr.shape, F32)
    acc_scr[...] = jnp.zeros(acc_scr.shape, F32)

    def kv_block(j, masked):
        kstart = pl.multiple_of(j * tq, tq)

        def ncols(g):
            return (g + 1) * G if masked else tq

        def scores(u):
            hh, g = units[u]
            n = ncols(g)
            kb = k_ref[0, pl.ds(kstart, n), :]
            s_scr[u, :, :n] = (lax.dot_general(qh[u], kb, contract_last, preferred_element_type=F32)
                               - fk_ref[0, 0, j, hh:hh + 1, :n])

        def strips(u):
            hh, g = units[u]
            n_all = ncols(g)
            for r in range(G // R):
                rows = slice(r * R, (r + 1) * R)
                grow = slice(g * G + r * R, g * G + (r + 1) * R)
                n = n_all
                if masked:
                    n = min(n_all, -(-(g * G + (r + 1) * R) // LANES) * LANES)
                    if n < n_all:
                        p_scr[u, rows, n:n_all] = jnp.zeros((R, n_all - n), BF16)
                s = s_scr[u, rows, :n]
                if masked:
                    s = jnp.where(_iota((R, n), 1) <= _iota((R, n), 0) + (g * G + r * R), s, NEG_INF)
                fq = fq_scr[hh, grow, :]
                m_old = m_scr[hh, grow, :]
                m_new = jnp.maximum(m_old, jnp.max(s, axis=-1, keepdims=True) + fq)
                p = jnp.exp2(s - jnp.concatenate([m_new - fq] * (n // LANES), axis=1))
                alpha = jnp.exp2(m_old - m_new)
                a_scr[hh, grow, :] = alpha
                l_scr[hh, grow, :] = alpha * l_scr[hh, grow, :] + jnp.sum(p, axis=-1, keepdims=True)
                m_scr[hh, grow, :] = m_new
                p_scr[u, rows, :n] = p.astype(BF16)

        def values(u):
            hh, g = units[u]
            n = ncols(g)
            grow = slice(g * G, (g + 1) * G)
            vb = v_ref[0, pl.ds(kstart, n), :]
            acc_scr[hh, grow, :] = (a_scr[hh, grow, :] * acc_scr[hh, grow, :]
                                    + jnp.dot(p_scr[u, :, :n], vb, preferred_element_type=F32))

        scores(0)
        for u in range(1, len(units)):
            scores(u)
            strips(u - 1)
            values(u - 1)
        strips(len(units) - 1)
        values(len(units) - 1)

    def body(j, carry):
        kv_block(j, False)
        return carry

    lax.fori_loop(0, i, body, 0)
    kv_block(i, True)
    lane = _iota((tq, LANES), 1)
    o_ref[0] = jnp.where(lane < HEAD_DIM, acc_scr[0] / l_scr[0], acc_scr[1] / l_scr[1]).astype(o_ref.dtype)


def _fox_attention(qkv, f_cum, B, S):
    tq = min(FOX_TQ, S)
    nk = S // tq
    grp = min(FOX_GROUP, tq)
    n_units = 2 * (tq // grp)
    n_pairs = N_HEADS_ATT // 2
    qkv3 = qkv.reshape(B, S, 3 * ATT_WIDTH)
    f5 = f_cum.reshape(B, n_pairs, 2, nk, tq).transpose(0, 1, 3, 2, 4)
    out = pl.pallas_call(
        functools.partial(_fox_kernel, tq=tq),
        out_shape=jax.ShapeDtypeStruct((B, S, ATT_WIDTH), BF16),
        grid=(B, n_pairs, nk),
        in_specs=[pl.BlockSpec((1, tq, LANES), lambda b, p, i: (b, i, p)),
                  pl.BlockSpec((1, S, LANES), lambda b, p, i: (b, 0, n_pairs + p)),
                  pl.BlockSpec((1, S, LANES), lambda b, p, i: (b, 0, 2 * n_pairs + p)),
                  pl.BlockSpec((1, 1, 1, 2, tq), lambda b, p, i: (b, p, i, 0, 0)),
                  pl.BlockSpec((1, 1, nk, 2, tq), lambda b, p, i: (b, p, 0, 0, 0))],
        out_specs=pl.BlockSpec((1, tq, LANES), lambda b, p, i: (b, i, p)),
        scratch_shapes=[pltpu.VMEM((n_units, grp, tq), F32), pltpu.VMEM((n_units, grp, tq), BF16)]
                       + [pltpu.VMEM((2, tq, LANES), F32) for _ in range(5)],
        compiler_params=_cparams("parallel", "parallel", "arbitrary"),
        name="fox_attention",
    )(qkv3, qkv3, qkv3, f5, f5)
    return out.reshape(B * S, ATT_WIDTH)


def _dil_kernel(slope_ref, q_ref, k_ref, v_ref, o_ref, m0_scr, m1_scr, l_scr, acc_scr, *, S):
    QB = QUERY_BLOCK
    G = DIL_GROUP
    p_idx = pl.program_id(1)
    lane = _iota((QB, LANES), 1)
    head0 = lane < HEAD_DIM
    contract_last = (((1,), (1,)), ((), ()))
    qscale = HEAD_DIM ** -0.5 * LOG2E
    slopes = [slope_ref[2 * p_idx + hh] * LOG2E for hh in range(2)]
    delta_cur = _iota((QB, QB), 0) - _iota((QB, QB), 1)
    delta_two = QB + _iota((QB, 2 * QB), 0) - _iota((QB, 2 * QB), 1)

    def rows(start, dil):
        return pl.ds(start, QB, stride=dil) if dil > 1 else pl.ds(start, QB)

    def update(q0s, kp0s, dil, biases, first):
        loaded = []
        for g, q0 in enumerate(q0s):
            rq = rows(q0, dil)
            q = q_ref[0, rq, :] * qscale
            kc = k_ref[0, rq, :]
            vc = v_ref[0, rq, :]
            if kp0s is not None:
                rp = rows(kp0s[g], dil)
                kc = jnp.concatenate([k_ref[0, rp, :], kc], axis=0)
                vc = jnp.concatenate([v_ref[0, rp, :], vc], axis=0)
            old = None if first else (m0_scr[rq, :], m1_scr[rq, :], l_scr[rq, :], acc_scr[rq, :])
            loaded.append((rq, q, kc.astype(BF16), vc.astype(BF16), old))
        units = [(g, hh) for g in range(len(loaded)) for hh in range(2)]
        reps = loaded[0][2].shape[0] // LANES
        s_u = [lax.dot_general(jnp.where(head0 == (hh == 0), loaded[g][1], 0.0).astype(BF16), loaded[g][2],
                               contract_last, preferred_element_type=F32) - biases[hh] for g, hh in units]
        m_u = [jnp.broadcast_to(jnp.max(s, axis=-1, keepdims=True), (QB, LANES)) for s in s_u]
        if not first:
            m_u = [jnp.maximum(loaded[g][4][hh], m) for (g, hh), m in zip(units, m_u)]
            a_u = [jnp.exp2(loaded[g][4][hh] - m) for (g, hh), m in zip(units, m_u)]
        p_u = [jnp.exp2(s - jnp.concatenate([m] * reps, axis=1)) for s, m in zip(s_u, m_u)]
        ps_u = [jnp.sum(p, axis=-1, keepdims=True) for p in p_u]
        pv_u = [jnp.dot(p.astype(BF16), loaded[g][3], preferred_element_type=F32)
                for (g, hh), p in zip(units, p_u)]
        for g in range(len(loaded)):
            rq, old = loaded[g][0], loaded[g][4]
            l_new = jnp.where(head0, ps_u[2 * g], ps_u[2 * g + 1])
            acc_new = jnp.where(head0, pv_u[2 * g], pv_u[2 * g + 1])
            if not first:
                alpha = jnp.where(head0, a_u[2 * g], a_u[2 * g + 1])
                l_new = alpha * old[2] + l_new
                acc_new = alpha * old[3] + acc_new
            m0_scr[rq, :] = m_u[2 * g]
            m1_scr[rq, :] = m_u[2 * g + 1]
            l_scr[rq, :] = l_new
            acc_scr[rq, :] = acc_new

    for branch, (window, dil) in enumerate(sorted(DIL_PATTERNS, key=lambda wd: -wd[1])):
        span = window // dil
        assert span <= QB and (S // dil) % QB == 0
        nblk = S // dil // QB
        first = branch == 0

        def masked_bias(delta, hh, dil=dil, span=span):
            return jnp.where((delta >= 0) & (delta <= span), slopes[hh] * (delta * dil).astype(F32), -NEG_INF)

        bias_cur = [masked_bias(delta_cur, hh) for hh in range(2)]
        bias_two = [masked_bias(delta_two, hh) for hh in range(2)]

        ga = min(G, dil)

        def head_step(t, carry, dil=dil, ga=ga, bias_cur=bias_cur, first=first):
            update([t * ga + g for g in range(ga)], None, dil, bias_cur, first)
            return carry

        lax.fori_loop(0, dil // ga, head_step, 0)

        def starts(idx, dil=dil):
            q0 = (idx % dil) + (1 + idx // dil) * (QB * dil)
            return q0, q0 - QB * dil

        def tail_step(t, carry, base=0, count=G, dil=dil, bias_two=bias_two, first=first):
            pairs = [starts(base + t * count + g) for g in range(count)]
            update([a for a, _ in pairs], [b for _, b in pairs], dil, bias_two, first)
            return carry

        n_tail = dil * (nblk - 1)
        lax.fori_loop(0, n_tail // G, tail_step, 0)
        if n_tail % G:
            tail_step(0, 0, base=(n_tail // G) * G, count=n_tail % G)
    o_ref[0] = (acc_scr[...] / l_scr[...]).astype(o_ref.dtype)


def _dilated_attention(qkv, B, S):
    n_pairs = N_HEADS_ATT // 2
    slopes = jnp.asarray(2.0 ** (-8.0 * (np.arange(N_HEADS_ATT) + 1) / N_HEADS_ATT), dtype=F32)
    qkv3 = qkv.reshape(B, S, qkv.shape[1])
    out = pl.pallas_call(
        functools.partial(_dil_kernel, S=S),
        out_shape=jax.ShapeDtypeStruct((B, S, ATT_WIDTH), BF16),
        grid=(B, n_pairs),
        in_specs=[pl.BlockSpec(memory_space=pltpu.SMEM),
                  pl.BlockSpec((1, S, LANES), lambda b, p: (b, 0, p)),
                  pl.BlockSpec((1, S, LANES), lambda b, p: (b, 0, n_pairs + p)),
                  pl.BlockSpec((1, S, LANES), lambda b, p: (b, 0, 2 * n_pairs + p))],
        out_specs=pl.BlockSpec((1, S, LANES), lambda b, p: (b, 0, p)),
        scratch_shapes=[pltpu.VMEM((S, LANES), F32) for _ in range(4)],
        compiler_params=_cparams("parallel", "parallel"),
        name="dilated_attention",
    )(slopes, qkv3, qkv3, qkv3)
    return out.reshape(B * S, ATT_WIDTH)


def _attn_out_kernel(a0_ref, a1_ref, w_ref, x_ref, g_ref, b_ref, y_ref):
    k0 = a0_ref.shape[1]
    mix = jnp.dot(a0_ref[...], w_ref[:k0, :], preferred_element_type=F32)
    mix = mix + jnp.dot(a1_ref[...], w_ref[k0:, :], preferred_element_type=F32)
    y_ref[...] = _layer_norm_rows(DEEPNORM_ALPHA * x_ref[...] + mix, g_ref[...], b_ref[...])


def _attn_out_ln(a0, a1, w, x, g, b):
    M, D = x.shape
    tm = min(LN_TM, M)
    row = lambda i: (i, 0)
    const = lambda i: (0, 0)
    return pl.pallas_call(
        _attn_out_kernel,
        out_shape=jax.ShapeDtypeStruct((M, D), F32),
        grid=(M // tm,),
        in_specs=[pl.BlockSpec((tm, a0.shape[1]), row), pl.BlockSpec((tm, a1.shape[1]), row),
                  pl.BlockSpec(w.shape, const), pl.BlockSpec((tm, D), row),
                  pl.BlockSpec((1, D), const), pl.BlockSpec((1, D), const)],
        out_specs=pl.BlockSpec((tm, D), row),
        compiler_params=_cparams("parallel"),
        name="attn_out_ln",
    )(a0, a1, w, x, g.reshape(1, D), b.reshape(1, D))


def _gdn_out_kernel(a_ref, w_ref, x_ref, g_ref, b_ref, r_ref, yt_ref, route_ref):
    tm = x_ref.shape[0]
    mix = jnp.dot(a_ref[...], w_ref[...], preferred_element_type=F32)
    y = _layer_norm_rows(DEEPNORM_ALPHA * x_ref[...] + mix, g_ref[...], b_ref[...])
    for s in range(SUBLANES):
        yt_ref[pl.ds(s, tm, stride=SUBLANES), :] = y[:, s * LANES:(s + 1) * LANES]
    y_hi = y.astype(BF16)
    y_lo = (y - y_hi.astype(F32)).astype(BF16)
    both = jnp.dot(y_hi, r_ref[...], preferred_element_type=F32)
    logits = both[:, :LANES] + (both[:, LANES:] + jnp.dot(y_lo, r_ref[:, :LANES], preferred_element_type=F32))
    lane = _iota((tm, LANES), 1)
    logits = jnp.where(lane < N_EXPERTS, logits, -jnp.inf)
    m1 = jnp.max(logits, axis=-1, keepdims=True)
    i1 = jnp.min(jnp.where(logits == m1, lane, LANES), axis=-1, keepdims=True)
    rest = jnp.where(lane == i1, -jnp.inf, logits)
    m2 = jnp.max(rest, axis=-1, keepdims=True)
    i2 = jnp.min(jnp.where(rest == m2, lane, LANES), axis=-1, keepdims=True)
    e2 = jnp.exp(m2 - m1)
    w1 = 1.0 / (1.0 + e2)
    w2 = e2 / (1.0 + e2)
    route = jnp.where(lane == 0, i1.astype(F32),
                      jnp.where(lane == 1, i2.astype(F32),
                                jnp.where(lane == 2, w1, jnp.where(lane == 3, w2, 0.0))))
    route_ref[...] = route


def _gdn_out_ln_route(a, w, x, g, b, router):
    M, D = x.shape
    tm = min(LN_TM, M)
    row = lambda i: (i, 0)
    const = lambda i: (0, 0)
    r_pad = jnp.zeros((D, LANES), F32).at[:, :N_EXPERTS].set(router.astype(F32))
    r_hi = r_pad.astype(BF16)
    r_lo = (r_pad - r_hi.astype(F32)).astype(BF16)
    r_pad = jnp.concatenate([r_hi, r_lo], axis=1)
    return pl.pallas_call(
        _gdn_out_kernel,
        out_shape=(jax.ShapeDtypeStruct((M * SUBLANES, LANES), F32),
                   jax.ShapeDtypeStruct((M, LANES), F32)),
        grid=(M // tm,),
        in_specs=[pl.BlockSpec((tm, a.shape[1]), row), pl.BlockSpec(w.shape, const),
                  pl.BlockSpec((tm, D), row), pl.BlockSpec((1, D), const),
                  pl.BlockSpec((1, D), const), pl.BlockSpec((D, 2 * LANES), const)],
        out_specs=(pl.BlockSpec((tm * SUBLANES, LANES), row), pl.BlockSpec((tm, LANES), row)),
        compiler_params=_cparams("parallel"),
        name="gdn_out_ln_route",
    )(a, w, x, g.reshape(1, D), b.reshape(1, D), r_pad)


def _ffn_dense_kernel(wg_ref, wu_ref, wd_ref, x_ref, g_ref, b_ref, y_ref, ybf_ref, acc_scr, xbf_scr):
    j = pl.program_id(1)

    @pl.when(j == 0)
    def _():
        acc_scr[...] = jnp.zeros_like(acc_scr)
        xbf_scr[...] = x_ref[...].astype(BF16)

    x = xbf_scr[...]
    hg = jnp.dot(x, wg_ref[...], preferred_element_type=F32)
    hu = jnp.dot(x, wu_ref[...], preferred_element_type=F32)
    h = (_silu(hg) * hu).astype(BF16)
    acc_scr[...] += jnp.dot(h, wd_ref[...], preferred_element_type=F32)

    @pl.when(j == pl.num_programs(1) - 1)
    def _():
        y = _layer_norm_rows(DEEPNORM_ALPHA * x_ref[...] + acc_scr[...], g_ref[...], b_ref[...])
        y_ref[...] = y
        ybf_ref[...] = y.astype(BF16)


def _ffn_dense_ln(x, wg, wu, wd, g, b):
    M, D = x.shape
    FF = wg.shape[1]
    tm = min(FFN_TM, M)
    tf = FFN_TF
    assert FF % tf == 0
    row = lambda i, j: (i, 0)
    const = lambda i, j: (0, 0)
    return pl.pallas_call(
        _ffn_dense_kernel,
        out_shape=(jax.ShapeDtypeStruct((M, D), F32), jax.ShapeDtypeStruct((M, D), BF16)),
        grid=(M // tm, FF // tf),
        in_specs=[pl.BlockSpec((D, tf), lambda i, j: (0, j)),
                  pl.BlockSpec((D, tf), lambda i, j: (0, j)),
                  pl.BlockSpec((tf, D), lambda i, j: (j, 0)),
                  pl.BlockSpec((tm, D), row),
                  pl.BlockSpec((1, D), const), pl.BlockSpec((1, D), const)],
        out_specs=(pl.BlockSpec((tm, D), row), pl.BlockSpec((tm, D), row)),
        scratch_shapes=[pltpu.VMEM((tm, D), F32), pltpu.VMEM((tm, D), BF16)],
        compiler_params=_cparams("parallel", "arbitrary"),
        name="ffn_dense_ln",
    )(wg, wu, wd, x, g.reshape(1, D), b.reshape(1, D))


def _ffn_grouped_kernel(te_ref, na_ref, xs_hbm, wg_ref, wu_ref, wd_ref, ys_hbm,
                        xin_scr, yout_scr, xbf_scr, acc_scr, sem_in, sem_out):
    i = pl.program_id(0)
    j = pl.program_id(1)
    nj = pl.num_programs(1)
    tm = acc_scr.shape[0]
    rows = tm * SUBLANES
    na = na_ref[0]
    active = i < na
    slot = i % 2

    def x_copy(tile, sl):
        src = xs_hbm.at[pl.ds(pl.multiple_of(tile * rows, rows), rows), :]
        return pltpu.make_async_copy(src, xin_scr.at[sl], sem_in.at[sl])

    def y_copy(tile, sl):
        dst = ys_hbm.at[pl.ds(pl.multiple_of(tile * rows, rows), rows), :]
        return pltpu.make_async_copy(yout_scr.at[sl], dst, sem_out.at[sl])

    @pl.when(active & (j == 0))
    def _():
        @pl.when(i == 0)
        def _():
            x_copy(0, 0).start()

        x_copy(i, slot).wait()

        @pl.when(i + 1 < na)
        def _():
            x_copy(i + 1, 1 - slot).start()

        acc_scr[...] = jnp.zeros_like(acc_scr)
        for s in range(SUBLANES):
            xbf_scr[:, s * LANES:(s + 1) * LANES] = xin_scr[slot, pl.ds(s, tm, stride=SUBLANES), :].astype(BF16)

    @pl.when(active)
    def _():
        x = xbf_scr[...]
        hg = jnp.dot(x, wg_ref[0].astype(BF16), preferred_element_type=F32)
        hu = jnp.dot(x, wu_ref[0].astype(BF16), preferred_element_type=F32)
        h = (_silu(hg) * hu).astype(BF16)
        acc_scr[...] += jnp.dot(h, wd_ref[0].astype(BF16), preferred_element_type=F32)

    @pl.when(active & (j == nj - 1))
    def _():
        @pl.when(i >= 2)
        def _():
            y_copy(i - 2, slot).wait()

        for s in range(SUBLANES):
            yout_scr[slot, pl.ds(s, tm, stride=SUBLANES), :] = acc_scr[:, s * LANES:(s + 1) * LANES]
        y_copy(i, slot).start()

        @pl.when(i == na - 1)
        def _():
            @pl.when(i >= 1)
            def _():
                y_copy(i - 1, 1 - slot).wait()

            y_copy(i, slot).wait()

    @pl.when(jnp.logical_not(active) & (j == 0))
    def _():
        yout_scr[0] = jnp.zeros(yout_scr.shape[1:], F32)
        y_copy(i, 0).start()
        y_copy(i, 0).wait()


def _ffn_grouped(xt, tile_expert, n_active, wg, wu, wd, tm):
    R = xt.shape[0] // SUBLANES
    E, D, FF = wg.shape
    tf = FFN_TF
    nf = FF // tf
    n_tiles = R // tm

    def ff_idx(i, j, na):
        return jnp.where(i < na[0], j, nf - 1)

    return pl.pallas_call(
        _ffn_grouped_kernel,
        out_shape=jax.ShapeDtypeStruct((R * SUBLANES, LANES), F32),
        grid_spec=pltpu.PrefetchScalarGridSpec(
            num_scalar_prefetch=2,
            grid=(n_tiles, nf),
            in_specs=[pl.BlockSpec(memory_space=pl.ANY),
                      pl.BlockSpec((1, D, tf), lambda i, j, te, na: (te[i], 0, ff_idx(i, j, na))),
                      pl.BlockSpec((1, D, tf), lambda i, j, te, na: (te[i], 0, ff_idx(i, j, na))),
                      pl.BlockSpec((1, tf, D), lambda i, j, te, na: (te[i], ff_idx(i, j, na), 0))],
            out_specs=pl.BlockSpec(memory_space=pl.ANY),
            scratch_shapes=[pltpu.VMEM((2, tm * SUBLANES, LANES), F32), pltpu.VMEM((2, tm * SUBLANES, LANES), F32),
                            pltpu.VMEM((tm, D), BF16), pltpu.VMEM((tm, D), F32),
                            pltpu.SemaphoreType.DMA((2,)), pltpu.SemaphoreType.DMA((2,))]),
        compiler_params=_cparams("arbitrary", "arbitrary"),
        name="ffn_grouped",
    )(tile_expert, n_active, xt, wg, wu, wd)


def _row_tile(ref, r):
    return ref.at[pl.ds(pl.multiple_of(r, SUBLANES), SUBLANES), :]


def _dispatch_kernel(dest_ref, xt_ref, xs_in_hbm, xs_hbm, sem):
    del xs_in_hbm
    ch = dest_ref.shape[2] // 2

    def body(t, carry):
        src = _row_tile(xt_ref, t * SUBLANES)
        for k in range(2):
            pltpu.make_async_copy(src, _row_tile(xs_hbm, dest_ref[0, 0, 2 * t + k]), sem).start()
        return carry

    lax.fori_loop(0, ch, body, 0, unroll=8)
    for _ in range(2):
        pltpu.make_async_copy(xt_ref, xs_hbm.at[pl.ds(0, ch * SUBLANES), :], sem).wait()


def _dispatch_rows(xt, dest8, xs_init):
    N = xt.shape[0] // SUBLANES
    ch = min(DISPATCH_CHUNK, N)
    dest3 = dest8.reshape(N // ch, 1, 2 * ch)
    return pl.pallas_call(
        _dispatch_kernel,
        out_shape=jax.ShapeDtypeStruct(xs_init.shape, F32),
        grid=(N // ch,),
        in_specs=[pl.BlockSpec((1, 1, 2 * ch), lambda i: (i, 0, 0), memory_space=pltpu.SMEM),
                  pl.BlockSpec((ch * SUBLANES, LANES), lambda i: (i, 0)),
                  pl.BlockSpec(memory_space=pl.ANY)],
        out_specs=pl.BlockSpec(memory_space=pl.ANY),
        scratch_shapes=[pltpu.SemaphoreType.DMA(())],
        input_output_aliases={2: 0},
        compiler_params=_cparams("arbitrary"),
        name="moe_dispatch",
    )(dest3, xt, xs_init)


def _moe_ln_kernel(dest_ref, dnext_ref, xt_ref, ys_hbm, route_ref, g_ref, b_ref, o_ref, ya_scr, yb_scr, sem):
    tm = o_ref.shape[0]
    i = pl.program_id(0)
    slot = i % 2

    def start_gathers(idx_ref, sl):
        def body(t, carry):
            pltpu.make_async_copy(_row_tile(ys_hbm, idx_ref[0, 0, 2 * t]),
                                  _row_tile(ya_scr.at[sl], t * SUBLANES), sem.at[sl]).start()
            pltpu.make_async_copy(_row_tile(ys_hbm, idx_ref[0, 0, 2 * t + 1]),
                                  _row_tile(yb_scr.at[sl], t * SUBLANES), sem.at[sl]).start()
            return carry

        lax.fori_loop(0, tm, body, 0, unroll=8)

    @pl.when(i == 0)
    def _():
        start_gathers(dest_ref, 0)

    @pl.when(i + 1 < pl.num_programs(0))
    def _():
        start_gathers(dnext_ref, 1 - slot)

    for scr in (ya_scr, yb_scr):
        pltpu.make_async_copy(ys_hbm.at[pl.ds(0, tm * SUBLANES), :], scr.at[slot], sem.at[slot]).wait()

    w1 = jnp.broadcast_to(route_ref[:, 2:3], (tm, LANES))
    w2 = jnp.broadcast_to(route_ref[:, 3:4], (tm, LANES))
    parts = []
    for s in range(SUBLANES):
        rows = pl.ds(s, tm, stride=SUBLANES)
        parts.append(DEEPNORM_ALPHA * xt_ref[rows, :]
                     + (w1 * ya_scr[slot, rows, :] + w2 * yb_scr[slot, rows, :]))
    d_model = SUBLANES * LANES
    mu = jnp.sum(sum(parts), axis=-1, keepdims=True) / d_model
    var = jnp.sum(sum((z - mu) * (z - mu) for z in parts), axis=-1, keepdims=True) / d_model
    rstd = lax.rsqrt(var + LN_EPS)
    for s in range(SUBLANES):
        cols = slice(s * LANES, (s + 1) * LANES)
        o_ref[:, cols] = (parts[s] - mu) * rstd * g_ref[:, cols] + b_ref[:, cols]


def _moe_combine_ln(xt, ys, dest8, route, g, b):
    M = route.shape[0]
    D = SUBLANES * LANES
    tm = min(LN_TM, M)
    row = lambda i: (i, 0)
    const = lambda i: (0, 0)
    n_blocks = M // tm
    dest3 = dest8.reshape(n_blocks, 1, 2 * tm)
    return pl.pallas_call(
        _moe_ln_kernel,
        out_shape=jax.ShapeDtypeStruct((M, D), F32),
        grid=(n_blocks,),
        in_specs=[pl.BlockSpec((1, 1, 2 * tm), lambda i: (i, 0, 0), memory_space=pltpu.SMEM),
                  pl.BlockSpec((1, 1, 2 * tm), lambda i: (jnp.minimum(i + 1, n_blocks - 1), 0, 0),
                               memory_space=pltpu.SMEM),
                  pl.BlockSpec((tm * SUBLANES, LANES), row),
                  pl.BlockSpec(memory_space=pl.ANY),
                  pl.BlockSpec((tm, LANES), row),
                  pl.BlockSpec((1, D), const), pl.BlockSpec((1, D), const)],
        out_specs=pl.BlockSpec((tm, D), row),
        scratch_shapes=[pltpu.VMEM((2, tm * SUBLANES, LANES), F32), pltpu.VMEM((2, tm * SUBLANES, LANES), F32),
                        pltpu.SemaphoreType.DMA((2,))],
        compiler_params=_cparams("arbitrary"),
        name="moe_combine_ln",
    )(dest3, dest3, xt, ys, route, g.reshape(1, D), b.reshape(1, D))


def _gdn_kernel(alog_ref, dt_ref, pq_ref, pk_ref, pv_ref, hq_ref, hk_ref, hv_ref,
                cq_ref, ck_ref, cv_ref, br_ref, ar_ref, ng_ref, gate_ref, o_ref,
                state_scr, sq_scr, sk_scr, sv_scr, *, blk, hp):
    C = GDN_CHUNK
    Dh = GDN_HEAD_DIM
    nchunk = blk // C
    h0 = pl.program_id(1) * hp
    sb = pl.program_id(2)
    heads = range(hp)

    @pl.when(sb == 0)
    def _():
        state_scr[...] = jnp.zeros_like(state_scr)

    have_prev = (sb > 0).astype(F32)

    def conv_silu(cur_ref, halo_ref, w_ref, stage_scr):
        outs = []
        for c in range(stage_scr.shape[0]):
            lanes = slice(c * LANES, (c + 1) * LANES)
            stage_scr[c, pl.ds(0, SUBLANES, stride=2), :] = halo_ref[0, :, lanes] * have_prev
            stage_scr[c, pl.ds(2 * SUBLANES, blk, stride=2), :] = cur_ref[0, :, lanes]
            out = None
            for j in range(GDN_CONV):
                off = 2 * (SUBLANES - (GDN_CONV - 1) + j)
                term = w_ref[j:j + 1, lanes] * stage_scr[c, pl.ds(off, blk, stride=2), :]
                out = term if out is None else out + term
            outs.append(out)
        return _silu(jnp.concatenate(outs, axis=1))

    def split(t):
        return [t[:, hh * Dh:(hh + 1) * Dh] for hh in heads]

    def l2n(t):
        return t * lax.rsqrt(jnp.sum(t * t, axis=-1, keepdims=True) + RMS_EPS)

    q_h = [l2n(t) * (Dh ** -0.5) for t in split(conv_silu(pq_ref, hq_ref, cq_ref, sq_scr))]
    k_h = [l2n(t) for t in split(conv_silu(pk_ref, hk_ref, ck_ref, sk_scr))]
    v_h = split(conv_silu(pv_ref, hv_ref, cv_ref, sv_scr))

    lanes_row = _iota((hp, blk), 0)
    dt_rows = jnp.zeros((hp, blk), F32)
    alog_rows = jnp.zeros((hp, blk), F32)
    for hh in heads:
        dt_rows = jnp.where(lanes_row == hh, dt_ref[h0 + hh], dt_rows)
        alog_rows = jnp.where(lanes_row == hh, alog_ref[h0 + hh], alog_rows)
    beta_rows = jax.nn.sigmoid(br_ref[0, 0, 0])
    za = ar_ref[0, 0, 0] + dt_rows
    g_rows = -jnp.exp(alog_rows) * (jnp.maximum(za, 0.0) + jnp.log1p(jnp.exp(-jnp.abs(za))))
    ri = _iota((blk, blk), 0)
    ci = _iota((blk, blk), 1)
    same = (ri // C) == (ci // C)
    g8 = jnp.concatenate([g_rows, jnp.zeros((SUBLANES - hp, blk), F32)], axis=0) if hp < SUBLANES else g_rows
    gam_rows = jnp.dot(g8, (same & (ri <= ci)).astype(F32), precision=HIGHEST,
                       preferred_element_type=F32)
    gl_rows = jnp.dot(g8, same.astype(F32), precision=HIGHEST,
                      preferred_element_type=F32)
    beta = [_row_to_col(beta_rows[hh:hh + 1, :]) for hh in heads]
    gam = [_row_to_col(gam_rows[hh:hh + 1, :]) for hh in heads]
    gl = [_row_to_col(gl_rows[hh:hh + 1, :]) for hh in heads]
    eg = [jnp.exp(t) for t in gam]
    ekd = [jnp.exp(a - b) for a, b in zip(gl, gam)]

    incl = same & (ri >= ci)
    strict = same & (ri > ci)
    contract_last = (((1,), (1,)), ((), ()))
    decay = [jnp.where(incl, jnp.exp(jnp.where(incl, gam[hh] - gam_rows[hh:hh + 1, :], 0.0)), 0.0)
             for hh in heads]
    kb = [t.astype(BF16) for t in k_h]
    kk = [lax.dot_general(t, t, contract_last, preferred_element_type=F32) for t in kb]
    x_acc = [jnp.where(strict, -(beta[hh] * kk[hh] * decay[hh]), 0.0) for hh in heads]
    pw = x_acc
    for _ in range(int(np.log2(C)) - 1):
        pwb = [t.astype(BF16) for t in pw]
        pw = [jnp.dot(t, t, preferred_element_type=F32) for t in pwb]
        x_acc = [x + p + jnp.dot(p.astype(BF16), x.astype(BF16), preferred_element_type=F32)
                 for x, p in zip(x_acc, pw)]
    rhs = [jnp.concatenate([v_h[hh] * beta[hh], k_h[hh] * (beta[hh] * eg[hh])], axis=1) for hh in heads]
    sol = [r + jnp.dot(x.astype(BF16), r.astype(BF16), preferred_element_type=F32)
           for x, r in zip(x_acc, rhs)]
    u = [t[:, :Dh] for t in sol]
    w_b = [t[:, Dh:].astype(BF16) for t in sol]
    qk = [lax.dot_general(q_h[hh].astype(BF16), kb[hh], contract_last, preferred_element_type=F32) * decay[hh]
          for hh in heads]
    q_dec = [(q_h[hh] * eg[hh]).astype(BF16) for hh in heads]
    k_dec = [(k_h[hh] * ekd[hh]).astype(BF16) for hh in heads]

    state = [state_scr[hh] for hh in heads]
    v_new = [[] for _ in heads]
    o_inter = [[] for _ in heads]
    for c in range(nchunk):
        rows = slice(c * C, (c + 1) * C)
        sbf = [t.astype(BF16) for t in state]
        vn = [u[hh][rows] - jnp.dot(w_b[hh][rows], sbf[hh], preferred_element_type=F32) for hh in heads]
        for hh in heads:
            o_inter[hh].append(jnp.dot(q_dec[hh][rows], sbf[hh], preferred_element_type=F32))
            v_new[hh].append(vn[hh])
        state = [state[hh] * jnp.exp(gl[hh][c * C:c * C + 1, :])
                 + lax.dot_general(k_dec[hh][rows], vn[hh].astype(BF16), (((0,), (0,)), ((), ())),
                                   preferred_element_type=F32) for hh in heads]
    for hh in heads:
        state_scr[hh] = state[hh]
    o = [jnp.concatenate(o_inter[hh], axis=0)
         + jnp.dot(qk[hh].astype(BF16), jnp.concatenate(v_new[hh], axis=0).astype(BF16),
                   preferred_element_type=F32) for hh in heads]
    o = [t * lax.rsqrt(jnp.mean(t * t, axis=-1, keepdims=True) + RMS_EPS) * ng_ref[...] for t in o]
    o = o[0] if hp == 1 else jnp.concatenate(o, axis=1)
    o_ref[0] = (o * _silu(gate_ref[0])).astype(o_ref.dtype)


def _gated_deltanet(pre, ab_rows, gate, conv_w, a_log, dt_bias, norm_g, B, S):
    blk = min(GDN_BLK, S)
    H = N_HEADS_GDN
    hp = GDN_HEADS_PER_STEP
    hg = H // hp
    wide = hp * GDN_HEAD_DIM
    pre3 = pre.reshape(B, S, pre.shape[1])
    gate3 = gate.reshape(B, S, GDN_WIDTH)
    ab5 = ab_rows.reshape(B, 2, hg, hp, S)
    halo_blocks = blk // SUBLANES

    def cur(sec):
        return pl.BlockSpec((1, blk, wide), lambda b, h, s: (b, s, sec * hg + h))

    def halo(sec):
        return pl.BlockSpec((1, SUBLANES, wide),
                            lambda b, h, s: (b, jnp.maximum(s * halo_blocks - 1, 0), sec * hg + h))

    def cw(sec):
        return pl.BlockSpec((GDN_CONV, wide), lambda b, h, s: (0, sec * hg + h))

    smem = pl.BlockSpec(memory_space=pltpu.SMEM)
    out = pl.pallas_call(
        functools.partial(_gdn_kernel, blk=blk, hp=hp),
        out_shape=jax.ShapeDtypeStruct((B, S, GDN_WIDTH), BF16),
        grid=(B, hg, S // blk),
        in_specs=[smem, smem,
                  cur(0), cur(1), cur(2), halo(0), halo(1), halo(2),
                  cw(0), cw(1), cw(2),
                  pl.BlockSpec((1, 1, 1, hp, blk), lambda b, h, s: (b, 0, h, 0, s)),
                  pl.BlockSpec((1, 1, 1, hp, blk), lambda b, h, s: (b, 1, h, 0, s)),
                  pl.BlockSpec((1, LANES), lambda b, h, s: (0, 0)),
                  pl.BlockSpec((1, blk, wide), lambda b, h, s: (b, s, h))],
        out_specs=pl.BlockSpec((1, blk, wide), lambda b, h, s: (b, s, h)),
        scratch_shapes=[pltpu.VMEM((hp, GDN_HEAD_DIM, GDN_HEAD_DIM), F32)]
                       + [pltpu.VMEM((wide // LANES, 2 * (blk + SUBLANES), LANES), F32) for _ in range(3)],
        compiler_params=_cparams("parallel", "parallel", "arbitrary"),
        name="gated_deltanet",
    )(a_log.astype(F32), dt_bias.astype(F32), pre3, pre3, pre3, pre3, pre3, pre3,
      conv_w, conv_w, conv_w, ab5, ab5, norm_g.reshape(1, LANES).astype(F32), gate3)
    return out.reshape(B * S, GDN_WIDTH)


def _pad_cols(w, width):
    return jnp.zeros((w.shape[0], width), w.dtype).at[:, :w.shape[1]].set(w)


def _attention_layer(x, xbf, B, S, w_in, forget_bias, w_out, ln_g, ln_b,
                     w_gate, w_up, w_down, ln2_g, ln2_b):
    W = ATT_WIDTH
    H = N_HEADS_ATT
    w_fox = jnp.concatenate([w_in[:, :W] * (HEAD_DIM ** -0.5 * LOG2E), w_in[:, W:3 * W]], axis=1).astype(BF16)
    w_dil = jnp.concatenate([w_in[:, 3 * W + H:], _pad_cols(w_in[:, 3 * W:3 * W + H], LANES)], axis=1).astype(BF16)
    qkv_fox = _proj(xbf, w_fox, BF16)
    qkv_dil = _proj(xbf, w_dil, F32)
    f_rows = qkv_dil[:, 3 * W:3 * W + H].reshape(B, S, H).transpose(0, 2, 1)
    f_cum = _forget_cumsum(f_rows, forget_bias)
    o_fox = _fox_attention(qkv_fox, f_cum, B, S)
    o_dil = _dilated_attention(qkv_dil, B, S)
    x1 = _attn_out_ln(o_fox, o_dil, w_out.astype(BF16), x, ln_g, ln_b)
    return _ffn_dense_ln(x1, w_gate.astype(BF16), w_up.astype(BF16), w_down.astype(BF16),
                         ln2_g, ln2_b)


def _slot_indices(route, tm):
    N = route.shape[0]
    experts = route[:, 0:2].astype(jnp.int32).reshape(2 * N)
    onehot = (experts[:, None] == jnp.arange(N_EXPERTS, dtype=jnp.int32)[None, :]).astype(jnp.int32)
    csum = jnp.cumsum(onehot, axis=0)
    counts = csum[-1]
    padded = ((counts + tm - 1) // tm) * tm
    ends = jnp.cumsum(padded)
    starts = ends - padded
    dest = jnp.sum(onehot * (csum - 1 + starts[None, :]), axis=1).astype(jnp.int32)
    n_tiles = (2 * N) // tm + N_EXPERTS
    tile_start = jnp.arange(n_tiles, dtype=jnp.int32) * tm
    tile_expert = jnp.minimum(jnp.sum((tile_start[:, None] >= ends[None, :]).astype(jnp.int32), axis=1),
                              N_EXPERTS - 1).astype(jnp.int32)
    n_active = (ends[-1] // tm).astype(jnp.int32).reshape(1)
    return dest, tile_expert, n_active, n_tiles


def _deltanet_layer(x, xbf, B, S, w_in, conv_w, a_log, dt_bias, norm_g, w_out, ln_g, ln_b,
                    router, w_gate, w_up, w_down, ln2_g, ln2_b):
    N = B * S
    W = GDN_WIDTH
    H = N_HEADS_GDN
    w_qkv = jnp.concatenate([w_in[:, :3 * W], _pad_cols(w_in[:, 3 * W:3 * W + 2 * H], 2 * LANES)], axis=1).astype(BF16)
    w_gt = w_in[:, 3 * W + 2 * H:].astype(BF16)
    pre = _proj(xbf, w_qkv, F32)
    gate = _proj(xbf, w_gt, F32)
    ab_rows = pre[:, 3 * W:3 * W + 2 * H].reshape(B, S, 2 * H).transpose(0, 2, 1).reshape(B, 2 * H, 1, S)
    o = _gated_deltanet(pre, ab_rows, gate, conv_w.astype(F32), a_log, dt_bias, norm_g, B, S)
    xt, route = _gdn_out_ln_route(o, w_out.astype(BF16), x, ln_g, ln_b, router)

    tm = min(FFN_TM, N)
    dest, tile_expert, n_active, n_tiles = _slot_indices(route, tm)
    dest8 = dest * SUBLANES
    xs_init = jnp.zeros((n_tiles * tm * SUBLANES, LANES), F32)
    xs = _dispatch_rows(xt, dest8, xs_init)
    ys = _ffn_grouped(xs, tile_expert, n_active,
                      w_gate, w_up, w_down, tm)
    return _moe_combine_ln(xt, ys, dest8, route, ln2_g, ln2_b)


def kernel(x, attn_w_in, fox_forget_bias, attn_w_out, ln_attn_g, ln_attn_b, ffn_w_gate, ffn_w_up,
           ffn_w_down, ln_ffn_g, ln_ffn_b, gdn_w_in, gdn_conv_w, gdn_a_log, gdn_dt_bias, gdn_norm_g,
           gdn_w_out, ln_gdn_g, ln_gdn_b, moe_router, moe_w_gate, moe_w_up, moe_w_down, ln_moe_g,
           ln_moe_b):
    B, S, D = x.shape
    x2 = x.reshape(B * S, D)
    x2bf = x2.astype(BF16)
    x2, x2bf = _attention_layer(x2, x2bf, B, S, attn_w_in[0], fox_forget_bias[0], attn_w_out[0],
                                ln_attn_g[0], ln_attn_b[0], ffn_w_gate[0], ffn_w_up[0], ffn_w_down[0],
                                ln_ffn_g[0], ln_ffn_b[0])
    y = _deltanet_layer(x2, x2bf, B, S, gdn_w_in[0], gdn_conv_w[0], gdn_a_log[0], gdn_dt_bias[0],
                        gdn_norm_g[0], gdn_w_out[0], ln_gdn_g[0], ln_gdn_b[0], moe_router[0],
                        moe_w_gate[0], moe_w_up[0], moe_w_down[0], ln_moe_g[0], ln_moe_b[0])
    return y.reshape(B, S, D)
```

```python
import functools

import numpy as np
import jax
import jax.numpy as jnp
from jax import lax
from jax.experimental import pallas as pl
from jax.experimental.pallas import tpu as pltpu

F32 = jnp.float32
BF16 = jnp.bfloat16
HIGHEST = lax.Precision.HIGHEST

LANES = 128
SUBLANES = 8
VMEM_LIMIT = 52 * 1024 * 1024

HEAD_DIM = 64
N_HEADS_ATT = 8
ATT_WIDTH = N_HEADS_ATT * HEAD_DIM
QUERY_BLOCK = 128
DIL_PATTERNS = ((128, 1), (512, 4), (2048, 16))
GDN_HEAD_DIM = 128
N_HEADS_GDN = 8
GDN_WIDTH = N_HEADS_GDN * GDN_HEAD_DIM
GDN_CONV = 4
GDN_CHUNK = 64
N_EXPERTS = 8
DEPTH = 2
DEEPNORM_ALPHA = (2.0 * DEPTH) ** 0.25
LN_EPS = 1e-5
RMS_EPS = 1e-6
NEG_INF = -1e30
LOG2E = 1.4426950408889634

PROJ_TM = 1024
PROJ_TN = 1664
FOX_TQ = 1024
FOX_GROUP = 512
FOX_STRIP = 32
DIL_GROUP = 8
LN_TM = 512
FFN_TM = 1024
FFN_TF = 512
GDN_BLK = 128
GDN_HEADS_PER_STEP = 8
DISPATCH_CHUNK = 512


def _cparams(*sem):
    return pltpu.CompilerParams(dimension_semantics=sem, vmem_limit_bytes=VMEM_LIMIT)


def _iota(shape, dim):
    return lax.broadcasted_iota(jnp.int32, shape, dim)


def _silu(x):
    return x * jax.nn.sigmoid(x)


def _row_to_col(row):
    n = row.shape[1]
    eye = _iota((LANES, LANES), 0) == _iota((LANES, LANES), 1)
    cols = []
    for c in range(n // LANES):
        seg = row[:, c * LANES:(c + 1) * LANES]
        cols.append(jnp.sum(jnp.where(eye, seg, 0.0), axis=1, keepdims=True))
    return cols[0] if len(cols) == 1 else jnp.concatenate(cols, axis=0)


def _layer_norm_rows(z, g, b):
    mu = jnp.mean(z, axis=-1, keepdims=True)
    zc = z - mu
    var = jnp.mean(zc * zc, axis=-1, keepdims=True)
    return zc * lax.rsqrt(var + LN_EPS) * g + b


def _proj_kernel(x_ref, w_ref, o_ref):
    o_ref[...] = jnp.dot(x_ref[...], w_ref[...], preferred_element_type=F32).astype(o_ref.dtype)


def _proj(x, w, out_dtype):
    M, K = x.shape
    C = w.shape[1]
    tm = min(PROJ_TM, M)
    tn = min(PROJ_TN, C)
    assert M % tm == 0 and C % tn == 0
    return pl.pallas_call(
        _proj_kernel,
        out_shape=jax.ShapeDtypeStruct((M, C), out_dtype),
        grid=(M // tm, C // tn),
        in_specs=[pl.BlockSpec((tm, K), lambda i, j: (i, 0)),
                  pl.BlockSpec((K, tn), lambda i, j: (0, j))],
        out_specs=pl.BlockSpec((tm, tn), lambda i, j: (i, j)),
        compiler_params=_cparams("parallel", "parallel"),
        name="proj",
    )(x, w)


def _forget_cumsum_kernel(f_ref, b_ref, o_ref):
    S = f_ref.shape[2]
    z = f_ref[0] + b_ref[...]
    lf = (jnp.minimum(z, 0.0) - jnp.log1p(jnp.exp(-jnp.abs(z)))) * LOG2E
    upper = (_iota((LANES, LANES), 0) <= _iota((LANES, LANES), 1)).astype(F32)
    carry = jnp.zeros((z.shape[0], 1), F32)
    for c in range(S // LANES):
        seg = jnp.dot(lf[:, c * LANES:(c + 1) * LANES], upper, precision=HIGHEST,
                      preferred_element_type=F32) + carry
        o_ref[0, :, c * LANES:(c + 1) * LANES] = seg
        carry = seg[:, LANES - 1:LANES]


def _forget_cumsum(f_rows, bias):
    B, H, S = f_rows.shape
    return pl.pallas_call(
        _forget_cumsum_kernel,
        out_shape=jax.ShapeDtypeStruct((B, H, S), F32),
        grid=(B,),
        in_specs=[pl.BlockSpec((1, H, S), lambda b: (b, 0, 0)),
                  pl.BlockSpec((H, 1), lambda b: (0, 0))],
        out_specs=pl.BlockSpec((1, H, S), lambda b: (b, 0, 0)),
        compiler_params=_cparams("parallel"),
        name="forget_cumsum",
    )(f_rows, bias.reshape(H, 1).astype(F32))


def _fox_kernel(q_ref, k_ref, v_ref, fq_ref, fk_ref, o_ref,
                s_scr, p_scr, m_scr, l_scr, a_scr, acc_scr, fq_scr, *, tq):
    G = min(FOX_GROUP, tq)
    R = FOX_STRIP
    i = pl.program_id(2)
    contract_last = (((1,), (1,)), ((), ()))
    units = [(hh, g) for hh in range(2) for g in range(tq // G)]
    lane_g = _iota((G, LANES), 1)
    qh = []
    for hh, g in units:
        qg = q_ref[0, g * G:(g + 1) * G, :]
        qh.append(jnp.where((lane_g < HEAD_DIM) == (hh == 0), qg, jnp.zeros_like(qg)))
    for hh in range(2):
        fq_scr[hh] = jnp.broadcast_to(_row_to_col(fq_ref[0, 0, 0, hh:hh + 1, :]), (tq, LANES))
    m_scr[...] = jnp.full(m_scr.shape, NEG_INF, F32)
    l_scr[...] = jnp.zeros(l_scr.shape, F32)
    acc_scr[...] = jnp.zeros(acc_scr.shape, F32)

    def kv_block(j, masked):
        kstart = pl.multiple_of(j * tq, tq)

        def ncols(g):
            return (g + 1) * G if masked else tq

        def scores(u):
            hh, g = units[u]
            n = ncols(g)
            kb = k_ref[0, pl.ds(kstart, n), :]
            s_scr[u, :, :n] = (lax.dot_general(qh[u], kb, contract_last, preferred_element_type=F32)
                               - fk_ref[0, 0, j, hh:hh + 1, :n])

        def strips(u):
            hh, g = units[u]
            n_all = ncols(g)
            for r in range(G // R):
                rows = slice(r * R, (r + 1) * R)
                grow = slice(g * G + r * R, g * G + (r + 1) * R)
                n = n_all
                if masked:
                    n = min(n_all, -(-(g * G + (r + 1) * R) // LANES) * LANES)
                    if n < n_all:
                        p_scr[u, rows, n:n_all] = jnp.zeros((R, n_all - n), BF16)
                s = s_scr[u, rows, :n]
                if masked:
                    s = jnp.where(_iota((R, n), 1) <= _iota((R, n), 0) + (g * G + r * R), s, NEG_INF)
                fq = fq_scr[hh, grow, :]
                m_old = m_scr[hh, grow, :]
                m_new = jnp.maximum(m_old, jnp.max(s, axis=-1, keepdims=True) + fq)
                p = jnp.exp2(s - jnp.concatenate([m_new - fq] * (n // LANES), axis=1))
                alpha = jnp.exp2(m_old - m_new)
                a_scr[hh, grow, :] = alpha
                l_scr[hh, grow, :] = alpha * l_scr[hh, grow, :] + jnp.sum(p, axis=-1, keepdims=True)
                m_scr[hh, grow, :] = m_new
                p_scr[u, rows, :n] = p.astype(BF16)

        def values(u):
            hh, g = units[u]
            n = ncols(g)
            grow = slice(g * G, (g + 1) * G)
            vb = v_ref[0, pl.ds(kstart, n), :]
            acc_scr[hh, grow, :] = (a_scr[hh, grow, :] * acc_scr[hh, grow, :]
                                    + jnp.dot(p_scr[u, :, :n], vb, preferred_element_type=F32))

        scores(0)
        for u in range(1, len(units)):
            scores(u)
            strips(u - 1)
            values(u - 1)
        strips(len(units) - 1)
        values(len(units) - 1)

    def body(j, carry):
        kv_block(j, False)
        return carry

    lax.fori_loop(0, i, body, 0)
    kv_block(i, True)
    lane = _iota((tq, LANES), 1)
    o_ref[0] = jnp.where(lane < HEAD_DIM, acc_scr[0] / l_scr[0], acc_scr[1] / l_scr[1]).astype(o_ref.dtype)


def _fox_attention(qkv, f_cum, B, S):
    tq = min(FOX_TQ, S)
    nk = S // tq
    grp = min(FOX_GROUP, tq)
    n_units = 2 * (tq // grp)
    n_pairs = N_HEADS_ATT // 2
    qkv3 = qkv.reshape(B, S, 3 * ATT_WIDTH)
    f5 = f_cum.reshape(B, n_pairs, 2, nk, tq).transpose(0, 1, 3, 2, 4)
    out = pl.pallas_call(
        functools.partial(_fox_kernel, tq=tq),
        out_shape=jax.ShapeDtypeStruct((B, S, ATT_WIDTH), BF16),
        grid=(B, n_pairs, nk),
        in_specs=[pl.BlockSpec((1, tq, LANES), lambda b, p, i: (b, i, p)),
                  pl.BlockSpec((1, S, LANES), lambda b, p, i: (b, 0, n_pairs + p)),
                  pl.BlockSpec((1, S, LANES), lambda b, p, i: (b, 0, 2 * n_pairs + p)),
                  pl.BlockSpec((1, 1, 1, 2, tq), lambda b, p, i: (b, p, i, 0, 0)),
                  pl.BlockSpec((1, 1, nk, 2, tq), lambda b, p, i: (b, p, 0, 0, 0))],
        out_specs=pl.BlockSpec((1, tq, LANES), lambda b, p, i: (b, i, p)),
        scratch_shapes=[pltpu.VMEM((n_units, grp, tq), F32), pltpu.VMEM((n_units, grp, tq), BF16)]
                       + [pltpu.VMEM((2, tq, LANES), F32) for _ in range(5)],
        compiler_params=_cparams("parallel", "parallel", "arbitrary"),
        name="fox_attention",
    )(qkv3, qkv3, qkv3, f5, f5)
    return out.reshape(B * S, ATT_WIDTH)


def _dil_kernel(slope_ref, q_ref, k_ref, v_ref, o_ref, m0_scr, m1_scr, l_scr, acc_scr, *, S):
    QB = QUERY_BLOCK
    G = DIL_GROUP
    p_idx = pl.program_id(1)
    lane = _iota((QB, LANES), 1)
    head0 = lane < HEAD_DIM
    contract_last = (((1,), (1,)), ((), ()))
    qscale = HEAD_DIM ** -0.5 * LOG2E
    slopes = [slope_ref[2 * p_idx + hh] * LOG2E for hh in range(2)]
    delta_cur = _iota((QB, QB), 0) - _iota((QB, QB), 1)
    delta_two = QB + _iota((QB, 2 * QB), 0) - _iota((QB, 2 * QB), 1)

    def rows(start, dil):
        return pl.ds(start, QB, stride=dil) if dil > 1 else pl.ds(start, QB)

    def update(q0s, kp0s, dil, biases, first):
        loaded = []
        for g, q0 in enumerate(q0s):
            rq = rows(q0, dil)
            q = q_ref[0, rq, :] * qscale
            kc = k_ref[0, rq, :]
            vc = v_ref[0, rq, :]
            if kp0s is not None:
                rp = rows(kp0s[g], dil)
                kc = jnp.concatenate([k_ref[0, rp, :], kc], axis=0)
                vc = jnp.concatenate([v_ref[0, rp, :], vc], axis=0)
            old = None if first else (m0_scr[rq, :], m1_scr[rq, :], l_scr[rq, :], acc_scr[rq, :])
            loaded.append((rq, q, kc.astype(BF16), vc.astype(BF16), old))
        units = [(g, hh) for g in range(len(loaded)) for hh in range(2)]
        reps = loaded[0][2].shape[0] // LANES
        s_u = [lax.dot_general(jnp.where(head0 == (hh == 0), loaded[g][1], 0.0).astype(BF16), loaded[g][2],
                               contract_last, preferred_element_type=F32) - biases[hh] for g, hh in units]
        m_u = [jnp.broadcast_to(jnp.max(s, axis=-1, keepdims=True), (QB, LANES)) for s in s_u]
        if not first:
            m_u = [jnp.maximum(loaded[g][4][hh], m) for (g, hh), m in zip(units, m_u)]
            a_u = [jnp.exp2(loaded[g][4][hh] - m) for (g, hh), m in zip(units, m_u)]
        p_u = [jnp.exp2(s - jnp.concatenate([m] * reps, axis=1)) for s, m in zip(s_u, m_u)]
        ps_u = [jnp.sum(p, axis=-1, keepdims=True) for p in p_u]
        pv_u = [jnp.dot(p.astype(BF16), loaded[g][3], preferred_element_type=F32)
                for (g, hh), p in zip(units, p_u)]
        for g in range(len(loaded)):
            rq, old = loaded[g][0], loaded[g][4]
            l_new = jnp.where(head0, ps_u[2 * g], ps_u[2 * g + 1])
            acc_new = jnp.where(head0, pv_u[2 * g], pv_u[2 * g + 1])
            if not first:
                alpha = jnp.where(head0, a_u[2 * g], a_u[2 * g + 1])
                l_new = alpha * old[2] + l_new
                acc_new = alpha * old[3] + acc_new
            m0_scr[rq, :] = m_u[2 * g]
            m1_scr[rq, :] = m_u[2 * g + 1]
            l_scr[rq, :] = l_new
            acc_scr[rq, :] = acc_new

    for branch, (window, dil) in enumerate(sorted(DIL_PATTERNS, key=lambda wd: -wd[1])):
        span = window // dil
        assert span <= QB and (S // dil) % QB == 0
        nblk = S // dil // QB
        first = branch == 0

        def masked_bias(delta, hh, dil=dil, span=span):
            return jnp.where((delta >= 0) & (delta <= span), slopes[hh] * (delta * dil).astype(F32), -NEG_INF)

        bias_cur = [masked_bias(delta_cur, hh) for hh in range(2)]
        bias_two = [masked_bias(delta_two, hh) for hh in range(2)]

        ga = min(G, dil)

        def head_step(t, carry, dil=dil, ga=ga, bias_cur=bias_cur, first=first):
            update([t * ga + g for g in range(ga)], None, dil, bias_cur, first)
            return carry

        lax.fori_loop(0, dil // ga, head_step, 0)

        def starts(idx, dil=dil):
            q0 = (idx % dil) + (1 + idx // dil) * (QB * dil)
            return q0, q0 - QB * dil

        def tail_step(t, carry, base=0, count=G, dil=dil, bias_two=bias_two, first=first):
            pairs = [starts(base + t * count + g) for g in range(count)]
            update([a for a, _ in pairs], [b for _, b in pairs], dil, bias_two, first)
            return carry

        n_tail = dil * (nblk - 1)
        lax.fori_loop(0, n_tail // G, tail_step, 0)
        if n_tail % G:
            tail_step(0, 0, base=(n_tail // G) * G, count=n_tail % G)
    o_ref[0] = (acc_scr[...] / l_scr[...]).astype(o_ref.dtype)


def _dilated_attention(qkv, B, S):
    n_pairs = N_HEADS_ATT // 2
    slopes = jnp.asarray(2.0 ** (-8.0 * (np.arange(N_HEADS_ATT) + 1) / N_HEADS_ATT), dtype=F32)
    qkv3 = qkv.reshape(B, S, qkv.shape[1])
    out = pl.pallas_call(
        functools.partial(_dil_kernel, S=S),
        out_shape=jax.ShapeDtypeStruct((B, S, ATT_WIDTH), BF16),
        grid=(B, n_pairs),
        in_specs=[pl.BlockSpec(memory_space=pltpu.SMEM),
                  pl.BlockSpec((1, S, LANES), lambda b, p: (b, 0, p)),
                  pl.BlockSpec((1, S, LANES), lambda b, p: (b, 0, n_pairs + p)),
                  pl.BlockSpec((1, S, LANES), lambda b, p: (b, 0, 2 * n_pairs + p))],
        out_specs=pl.BlockSpec((1, S, LANES), lambda b, p: (b, 0, p)),
        scratch_shapes=[pltpu.VMEM((S, LANES), F32) for _ in range(4)],
        compiler_params=_cparams("parallel", "parallel"),
        name="dilated_attention",
    )(slopes, qkv3, qkv3, qkv3)
    return out.reshape(B * S, ATT_WIDTH)


def _attn_out_kernel(a0_ref, a1_ref, w_ref, x_ref, g_ref, b_ref, y_ref):
    k0 = a0_ref.shape[1]
    mix = jnp.dot(a0_ref[...], w_ref[:k0, :], preferred_element_type=F32)
    mix = mix + jnp.dot(a1_ref[...], w_ref[k0:, :], preferred_element_type=F32)
    y_ref[...] = _layer_norm_rows(DEEPNORM_ALPHA * x_ref[...] + mix, g_ref[...], b_ref[...])


def _attn_out_ln(a0, a1, w, x, g, b):
    M, D = x.shape
    tm = min(LN_TM, M)
    row = lambda i: (i, 0)
    const = lambda i: (0, 0)
    return pl.pallas_call(
        _attn_out_kernel,
        out_shape=jax.ShapeDtypeStruct((M, D), F32),
        grid=(M // tm,),
        in_specs=[pl.BlockSpec((tm, a0.shape[1]), row), pl.BlockSpec((tm, a1.shape[1]), row),
                  pl.BlockSpec(w.shape, const), pl.BlockSpec((tm, D), row),
                  pl.BlockSpec((1, D), const), pl.BlockSpec((1, D), const)],
        out_specs=pl.BlockSpec((tm, D), row),
        compiler_params=_cparams("parallel"),
        name="attn_out_ln",
    )(a0, a1, w, x, g.reshape(1, D), b.reshape(1, D))


def _gdn_out_kernel(a_ref, w_ref, x_ref, g_ref, b_ref, r_ref, yt_ref, route_ref):
    tm = x_ref.shape[0]
    mix = jnp.dot(a_ref[...], w_ref[...], preferred_element_type=F32)
    y = _layer_norm_rows(DEEPNORM_ALPHA * x_ref[...] + mix, g_ref[...], b_ref[...])
    for s in range(SUBLANES):
        yt_ref[pl.ds(s, tm, stride=SUBLANES), :] = y[:, s * LANES:(s + 1) * LANES]
    y_hi = y.astype(BF16)
    y_lo = (y - y_hi.astype(F32)).astype(BF16)
    both = jnp.dot(y_hi, r_ref[...], preferred_element_type=F32)
    logits = both[:, :LANES] + (both[:, LANES:] + jnp.dot(y_lo, r_ref[:, :LANES], preferred_element_type=F32))
    lane = _iota((tm, LANES), 1)
    logits = jnp.where(lane < N_EXPERTS, logits, -jnp.inf)
    m1 = jnp.max(logits, axis=-1, keepdims=True)
    i1 = jnp.min(jnp.where(logits == m1, lane, LANES), axis=-1, keepdims=True)
    rest = jnp.where(lane == i1, -jnp.inf, logits)
    m2 = jnp.max(rest, axis=-1, keepdims=True)
    i2 = jnp.min(jnp.where(rest == m2, lane, LANES), axis=-1, keepdims=True)
    e2 = jnp.exp(m2 - m1)
    w1 = 1.0 / (1.0 + e2)
    w2 = e2 / (1.0 + e2)
    route = jnp.where(lane == 0, i1.astype(F32),
                      jnp.where(lane == 1, i2.astype(F32),
                                jnp.where(lane == 2, w1, jnp.where(lane == 3, w2, 0.0))))
    route_ref[...] = route


def _gdn_out_ln_route(a, w, x, g, b, router):
    M, D = x.shape
    tm = min(LN_TM, M)
    row = lambda i: (i, 0)
    const = lambda i: (0, 0)
    r_pad = jnp.zeros((D, LANES), F32).at[:, :N_EXPERTS].set(router.astype(F32))
    r_hi = r_pad.astype(BF16)
    r_lo = (r_pad - r_hi.astype(F32)).astype(BF16)
    r_pad = jnp.concatenate([r_hi, r_lo], axis=1)
    return pl.pallas_call(
        _gdn_out_kernel,
        out_shape=(jax.ShapeDtypeStruct((M * SUBLANES, LANES), F32),
                   jax.ShapeDtypeStruct((M, LANES), F32)),
        grid=(M // tm,),
        in_specs=[pl.BlockSpec((tm, a.shape[1]), row), pl.BlockSpec(w.shape, const),
                  pl.BlockSpec((tm, D), row), pl.BlockSpec((1, D), const),
                  pl.BlockSpec((1, D), const), pl.BlockSpec((D, 2 * LANES), const)],
        out_specs=(pl.BlockSpec((tm * SUBLANES, LANES), row), pl.BlockSpec((tm, LANES), row)),
        compiler_params=_cparams("parallel"),
        name="gdn_out_ln_route",
    )(a, w, x, g.reshape(1, D), b.reshape(1, D), r_pad)


def _ffn_dense_kernel(wg_ref, wu_ref, wd_ref, x_ref, g_ref, b_ref, y_ref, ybf_ref, acc_scr, xbf_scr):
    j = pl.program_id(1)

    @pl.when(j == 0)
    def _():
        acc_scr[...] = jnp.zeros_like(acc_scr)
        xbf_scr[...] = x_ref[...].astype(BF16)

    x = xbf_scr[...]
    hg = jnp.dot(x, wg_ref[...], preferred_element_type=F32)
    hu = jnp.dot(x, wu_ref[...], preferred_element_type=F32)
    h = (_silu(hg) * hu).astype(BF16)
    acc_scr[...] += jnp.dot(h, wd_ref[...], preferred_element_type=F32)

    @pl.when(j == pl.num_programs(1) - 1)
    def _():
        y = _layer_norm_rows(DEEPNORM_ALPHA * x_ref[...] + acc_scr[...], g_ref[...], b_ref[...])
        y_ref[...] = y
        ybf_ref[...] = y.astype(BF16)


def _ffn_dense_ln(x, wg, wu, wd, g, b):
    M, D = x.shape
    FF = wg.shape[1]
    tm = min(FFN_TM, M)
    tf = FFN_TF
    assert FF % tf == 0
    row = lambda i, j: (i, 0)
    const = lambda i, j: (0, 0)
    return pl.pallas_call(
        _ffn_dense_kernel,
        out_shape=(jax.ShapeDtypeStruct((M, D), F32), jax.ShapeDtypeStruct((M, D), BF16)),
        grid=(M // tm, FF // tf),
        in_specs=[pl.BlockSpec((D, tf), lambda i, j: (0, j)),
                  pl.BlockSpec((D, tf), lambda i, j: (0, j)),
                  pl.BlockSpec((tf, D), lambda i, j: (j, 0)),
                  pl.BlockSpec((tm, D), row),
                  pl.BlockSpec((1, D), const), pl.BlockSpec((1, D), const)],
        out_specs=(pl.BlockSpec((tm, D), row), pl.BlockSpec((tm, D), row)),
        scratch_shapes=[pltpu.VMEM((tm, D), F32), pltpu.VMEM((tm, D), BF16)],
        compiler_params=_cparams("parallel", "arbitrary"),
        name="ffn_dense_ln",
    )(wg, wu, wd, x, g.reshape(1, D), b.reshape(1, D))


def _ffn_grouped_kernel(te_ref, na_ref, xs_hbm, wg_ref, wu_ref, wd_ref, ys_hbm,
                        xin_scr, yout_scr, xbf_scr, acc_scr, sem_in, sem_out):
    i = pl.program_id(0)
    j = pl.program_id(1)
    nj = pl.num_programs(1)
    tm = acc_scr.shape[0]
    rows = tm * SUBLANES
    na = na_ref[0]
    active = i < na
    slot = i % 2

    def x_copy(tile, sl):
        src = xs_hbm.at[pl.ds(pl.multiple_of(tile * rows, rows), rows), :]
        return pltpu.make_async_copy(src, xin_scr.at[sl], sem_in.at[sl])

    def y_copy(tile, sl):
        dst = ys_hbm.at[pl.ds(pl.multiple_of(tile * rows, rows), rows), :]
        return pltpu.make_async_copy(yout_scr.at[sl], dst, sem_out.at[sl])

    @pl.when(active & (j == 0))
    def _():
        @pl.when(i == 0)
        def _():
            x_copy(0, 0).start()

        x_copy(i, slot).wait()

        @pl.when(i + 1 < na)
        def _():
            x_copy(i + 1, 1 - slot).start()

        acc_scr[...] = jnp.zeros_like(acc_scr)
        for s in range(SUBLANES):
            xbf_scr[:, s * LANES:(s + 1) * LANES] = xin_scr[slot, pl.ds(s, tm, stride=SUBLANES), :].astype(BF16)

    def swiglu_rows(n_rows):
        x = xbf_scr[:n_rows, :]
        hg = jnp.dot(x, wg_ref[0].astype(BF16), preferred_element_type=F32)
        hu = jnp.dot(x, wu_ref[0].astype(BF16), preferred_element_type=F32)
        h = (_silu(hg) * hu).astype(BF16)
        acc_scr[:n_rows, :] += jnp.dot(h, wd_ref[0].astype(BF16), preferred_element_type=F32)

    n_valid = na_ref[1 + i]

    @pl.when(active & (n_valid > tm // 2))
    def _():
        swiglu_rows(tm)

    @pl.when(active & (n_valid <= tm // 2))
    def _():
        swiglu_rows(tm // 2)

    @pl.when(active & (j == nj - 1))
    def _():
        @pl.when(i >= 2)
        def _():
            y_copy(i - 2, slot).wait()

        for s in range(SUBLANES):
            yout_scr[slot, pl.ds(s, tm, stride=SUBLANES), :] = acc_scr[:, s * LANES:(s + 1) * LANES]
        y_copy(i, slot).start()

        @pl.when(i == na - 1)
        def _():
            @pl.when(i >= 1)
            def _():
                y_copy(i - 1, 1 - slot).wait()

            y_copy(i, slot).wait()

    @pl.when(jnp.logical_not(active) & (j == 0))
    def _():
        yout_scr[0] = jnp.zeros(yout_scr.shape[1:], F32)
        y_copy(i, 0).start()
        y_copy(i, 0).wait()


def _ffn_grouped(xt, tile_expert, n_active, wg, wu, wd, tm):
    R = xt.shape[0] // SUBLANES
    E, D, FF = wg.shape
    tf = FFN_TF
    nf = FF // tf
    n_tiles = R // tm

    def ff_idx(i, j, na):
        return jnp.where(i < na[0], j, nf - 1)

    return pl.pallas_call(
        _ffn_grouped_kernel,
        out_shape=jax.ShapeDtypeStruct((R * SUBLANES, LANES), F32),
        grid_spec=pltpu.PrefetchScalarGridSpec(
            num_scalar_prefetch=2,
            grid=(n_tiles, nf),
            in_specs=[pl.BlockSpec(memory_space=pl.ANY),
                      pl.BlockSpec((1, D, tf), lambda i, j, te, na: (te[i], 0, ff_idx(i, j, na))),
                      pl.BlockSpec((1, D, tf), lambda i, j, te, na: (te[i], 0, ff_idx(i, j, na))),
                      pl.BlockSpec((1, tf, D), lambda i, j, te, na: (te[i], ff_idx(i, j, na), 0))],
            out_specs=pl.BlockSpec(memory_space=pl.ANY),
            scratch_shapes=[pltpu.VMEM((2, tm * SUBLANES, LANES), F32), pltpu.VMEM((2, tm * SUBLANES, LANES), F32),
                            pltpu.VMEM((tm, D), BF16), pltpu.VMEM((tm, D), F32),
                            pltpu.SemaphoreType.DMA((2,)), pltpu.SemaphoreType.DMA((2,))]),
        compiler_params=_cparams("arbitrary", "arbitrary"),
        name="ffn_grouped",
    )(tile_expert, n_active, xt, wg, wu, wd)


def _row_tile(ref, r):
    return ref.at[pl.ds(pl.multiple_of(r, SUBLANES), SUBLANES), :]


def _dispatch_kernel(dest_ref, xt_ref, xs_in_hbm, xs_hbm, sem):
    del xs_in_hbm
    ch = dest_ref.shape[2] // 2

    def body(t, carry):
        src = _row_tile(xt_ref, t * SUBLANES)
        for k in range(2):
            pltpu.make_async_copy(src, _row_tile(xs_hbm, dest_ref[0, 0, 2 * t + k]), sem).start()
        return carry

    lax.fori_loop(0, ch, body, 0, unroll=8)
    for _ in range(2):
        pltpu.make_async_copy(xt_ref, xs_hbm.at[pl.ds(0, ch * SUBLANES), :], sem).wait()


def _dispatch_rows(xt, dest8, xs_init):
    N = xt.shape[0] // SUBLANES
    ch = min(DISPATCH_CHUNK, N)
    dest3 = dest8.reshape(N // ch, 1, 2 * ch)
    return pl.pallas_call(
        _dispatch_kernel,
        out_shape=jax.ShapeDtypeStruct(xs_init.shape, F32),
        grid=(N // ch,),
        in_specs=[pl.BlockSpec((1, 1, 2 * ch), lambda i: (i, 0, 0), memory_space=pltpu.SMEM),
                  pl.BlockSpec((ch * SUBLANES, LANES), lambda i: (i, 0)),
                  pl.BlockSpec(memory_space=pl.ANY)],
        out_specs=pl.BlockSpec(memory_space=pl.ANY),
        scratch_shapes=[pltpu.SemaphoreType.DMA(())],
        input_output_aliases={2: 0},
        compiler_params=_cparams("arbitrary"),
        name="moe_dispatch",
    )(dest3, xt, xs_init)


def _moe_ln_kernel(dest_ref, dnext_ref, xt_ref, ys_hbm, route_ref, g_ref, b_ref, o_ref, ya_scr, yb_scr, sem):
    tm = o_ref.shape[0]
    i = pl.program_id(0)
    slot = i % 2

    def start_gathers(idx_ref, sl):
        def body(t, carry):
            pltpu.make_async_copy(_row_tile(ys_hbm, idx_ref[0, 0, 2 * t]),
                                  _row_tile(ya_scr.at[sl], t * SUBLANES), sem.at[sl]).start()
            pltpu.make_async_copy(_row_tile(ys_hbm, idx_ref[0, 0, 2 * t + 1]),
                                  _row_tile(yb_scr.at[sl], t * SUBLANES), sem.at[sl]).start()
            return carry

        lax.fori_loop(0, tm, body, 0, unroll=8)

    @pl.when(i == 0)
    def _():
        start_gathers(dest_ref, 0)

    @pl.when(i + 1 < pl.num_programs(0))
    def _():
        start_gathers(dnext_ref, 1 - slot)

    for scr in (ya_scr, yb_scr):
        pltpu.make_async_copy(ys_hbm.at[pl.ds(0, tm * SUBLANES), :], scr.at[slot], sem.at[slot]).wait()

    w1 = jnp.broadcast_to(route_ref[:, 2:3], (tm, LANES))
    w2 = jnp.broadcast_to(route_ref[:, 3:4], (tm, LANES))
    parts = []
    for s in range(SUBLANES):
        rows = pl.ds(s, tm, stride=SUBLANES)
        parts.append(DEEPNORM_ALPHA * xt_ref[rows, :]
                     + (w1 * ya_scr[slot, rows, :] + w2 * yb_scr[slot, rows, :]))
    d_model = SUBLANES * LANES
    mu = jnp.sum(sum(parts), axis=-1, keepdims=True) / d_model
    var = jnp.sum(sum((z - mu) * (z - mu) for z in parts), axis=-1, keepdims=True) / d_model
    rstd = lax.rsqrt(var + LN_EPS)
    for s in range(SUBLANES):
        cols = slice(s * LANES, (s + 1) * LANES)
        o_ref[:, cols] = (parts[s] - mu) * rstd * g_ref[:, cols] + b_ref[:, cols]


def _moe_combine_ln(xt, ys, dest8, route, g, b):
    M = route.shape[0]
    D = SUBLANES * LANES
    tm = min(LN_TM, M)
    row = lambda i: (i, 0)
    const = lambda i: (0, 0)
    n_blocks = M // tm
    dest3 = dest8.reshape(n_blocks, 1, 2 * tm)
    return pl.pallas_call(
        _moe_ln_kernel,
        out_shape=jax.ShapeDtypeStruct((M, D), F32),
        grid=(n_blocks,),
        in_specs=[pl.BlockSpec((1, 1, 2 * tm), lambda i: (i, 0, 0), memory_space=pltpu.SMEM),
                  pl.BlockSpec((1, 1, 2 * tm), lambda i: (jnp.minimum(i + 1, n_blocks - 1), 0, 0),
                               memory_space=pltpu.SMEM),
                  pl.BlockSpec((tm * SUBLANES, LANES), row),
                  pl.BlockSpec(memory_space=pl.ANY),
                  pl.BlockSpec((tm, LANES), row),
                  pl.BlockSpec((1, D), const), pl.BlockSpec((1, D), const)],
        out_specs=pl.BlockSpec((tm, D), row),
        scratch_shapes=[pltpu.VMEM((2, tm * SUBLANES, LANES), F32), pltpu.VMEM((2, tm * SUBLANES, LANES), F32),
                        pltpu.SemaphoreType.DMA((2,))],
        compiler_params=_cparams("arbitrary"),
        name="moe_combine_ln",
    )(dest3, dest3, xt, ys, route, g.reshape(1, D), b.reshape(1, D))


def _gdn_kernel(alog_ref, dt_ref, pq_ref, pk_ref, pv_ref, hq_ref, hk_ref, hv_ref,
                cq_ref, ck_ref, cv_ref, br_ref, ar_ref, ng_ref, gate_ref, o_ref,
                state_scr, sq_scr, sk_scr, sv_scr, *, blk, hp):
    C = GDN_CHUNK
    Dh = GDN_HEAD_DIM
    nchunk = blk // C
    h0 = pl.program_id(1) * hp
    sb = pl.program_id(2)
    heads = range(hp)

    @pl.when(sb == 0)
    def _():
        state_scr[...] = jnp.zeros_like(state_scr)

    have_prev = (sb > 0).astype(F32)

    def conv_silu(cur_ref, halo_ref, w_ref, stage_scr):
        outs = []
        for c in range(stage_scr.shape[0]):
            lanes = slice(c * LANES, (c + 1) * LANES)
            stage_scr[c, pl.ds(0, SUBLANES, stride=2), :] = halo_ref[0, :, lanes] * have_prev
            stage_scr[c, pl.ds(2 * SUBLANES, blk, stride=2), :] = cur_ref[0, :, lanes]
            out = None
            for j in range(GDN_CONV):
                off = 2 * (SUBLANES - (GDN_CONV - 1) + j)
                term = w_ref[j:j + 1, lanes] * stage_scr[c, pl.ds(off, blk, stride=2), :]
                out = term if out is None else out + term
            outs.append(out)
        return _silu(jnp.concatenate(outs, axis=1))

    def split(t):
        return [t[:, hh * Dh:(hh + 1) * Dh] for hh in heads]

    def l2n(t):
        return t * lax.rsqrt(jnp.sum(t * t, axis=-1, keepdims=True) + RMS_EPS)

    q_h = [l2n(t) * (Dh ** -0.5) for t in split(conv_silu(pq_ref, hq_ref, cq_ref, sq_scr))]
    k_h = [l2n(t) for t in split(conv_silu(pk_ref, hk_ref, ck_ref, sk_scr))]
    v_h = split(conv_silu(pv_ref, hv_ref, cv_ref, sv_scr))

    lanes_row = _iota((hp, blk), 0)
    dt_rows = jnp.zeros((hp, blk), F32)
    alog_rows = jnp.zeros((hp, blk), F32)
    for hh in heads:
        dt_rows = jnp.where(lanes_row == hh, dt_ref[h0 + hh], dt_rows)
        alog_rows = jnp.where(lanes_row == hh, alog_ref[h0 + hh], alog_rows)
    beta_rows = jax.nn.sigmoid(br_ref[0, 0, 0])
    za = ar_ref[0, 0, 0] + dt_rows
    g_rows = -jnp.exp(alog_rows) * (jnp.maximum(za, 0.0) + jnp.log1p(jnp.exp(-jnp.abs(za))))
    ri = _iota((blk, blk), 0)
    ci = _iota((blk, blk), 1)
    same = (ri // C) == (ci // C)
    g8 = jnp.concatenate([g_rows, jnp.zeros((SUBLANES - hp, blk), F32)], axis=0) if hp < SUBLANES else g_rows
    gam_rows = jnp.dot(g8, (same & (ri <= ci)).astype(F32), precision=HIGHEST,
                       preferred_element_type=F32)
    gl_rows = jnp.dot(g8, same.astype(F32), precision=HIGHEST,
                      preferred_element_type=F32)
    beta = [_row_to_col(beta_rows[hh:hh + 1, :]) for hh in heads]
    gam = [_row_to_col(gam_rows[hh:hh + 1, :]) for hh in heads]
    gl = [_row_to_col(gl_rows[hh:hh + 1, :]) for hh in heads]
    eg = [jnp.exp(t) for t in gam]
    ekd = [jnp.exp(a - b) for a, b in zip(gl, gam)]

    incl = same & (ri >= ci)
    strict = same & (ri > ci)
    contract_last = (((1,), (1,)), ((), ()))
    decay = [jnp.where(incl, jnp.exp(jnp.where(incl, gam[hh] - gam_rows[hh:hh + 1, :], 0.0)), 0.0)
             for hh in heads]
    kb = [t.astype(BF16) for t in k_h]
    kk = [lax.dot_general(t, t, contract_last, preferred_element_type=F32) for t in kb]
    x_acc = [jnp.where(strict, -(beta[hh] * kk[hh] * decay[hh]), 0.0) for hh in heads]
    pw = x_acc
    for _ in range(int(np.log2(C)) - 1):
        pwb = [t.astype(BF16) for t in pw]
        pw = [jnp.dot(t, t, preferred_element_type=F32) for t in pwb]
        x_acc = [x + p + jnp.dot(p.astype(BF16), x.astype(BF16), preferred_element_type=F32)
                 for x, p in zip(x_acc, pw)]
    rhs = [jnp.concatenate([v_h[hh] * beta[hh], k_h[hh] * (beta[hh] * eg[hh])], axis=1) for hh in heads]
    sol = [r + jnp.dot(x.astype(BF16), r.astype(BF16), preferred_element_type=F32)
           for x, r in zip(x_acc, rhs)]
    u = [t[:, :Dh] for t in sol]
    w_b = [t[:, Dh:].astype(BF16) for t in sol]
    qk = [lax.dot_general(q_h[hh].astype(BF16), kb[hh], contract_last, preferred_element_type=F32) * decay[hh]
          for hh in heads]
    q_dec = [(q_h[hh] * eg[hh]).astype(BF16) for hh in heads]
    k_dec = [(k_h[hh] * ekd[hh]).astype(BF16) for hh in heads]

    state = [state_scr[hh] for hh in heads]
    v_new = [[] for _ in heads]
    o_inter = [[] for _ in heads]
    for c in range(nchunk):
        rows = slice(c * C, (c + 1) * C)
        sbf = [t.astype(BF16) for t in state]
        vn = [u[hh][rows] - jnp.dot(w_b[hh][rows], sbf[hh], preferred_element_type=F32) for hh in heads]
        for hh in heads:
            o_inter[hh].append(jnp.dot(q_dec[hh][rows], sbf[hh], preferred_element_type=F32))
            v_new[hh].append(vn[hh])
        state = [state[hh] * jnp.exp(gl[hh][c * C:c * C + 1, :])
                 + lax.dot_general(k_dec[hh][rows], vn[hh].astype(BF16), (((0,), (0,)), ((), ())),
                                   preferred_element_type=F32) for hh in heads]
    for hh in heads:
        state_scr[hh] = state[hh]
    o = [jnp.concatenate(o_inter[hh], axis=0)
         + jnp.dot(qk[hh].astype(BF16), jnp.concatenate(v_new[hh], axis=0).astype(BF16),
                   preferred_element_type=F32) for hh in heads]
    o = [t * lax.rsqrt(jnp.mean(t * t, axis=-1, keepdims=True) + RMS_EPS) * ng_ref[...] for t in o]
    o = o[0] if hp == 1 else jnp.concatenate(o, axis=1)
    o_ref[0] = (o * _silu(gate_ref[0])).astype(o_ref.dtype)


def _gated_deltanet(pre, ab_rows, gate, conv_w, a_log, dt_bias, norm_g, B, S):
    blk = min(GDN_BLK, S)
    H = N_HEADS_GDN
    hp = GDN_HEADS_PER_STEP
    hg = H // hp
    wide = hp * GDN_HEAD_DIM
    pre3 = pre.reshape(B, S, pre.shape[1])
    gate3 = gate.reshape(B, S, GDN_WIDTH)
    ab5 = ab_rows.reshape(B, 2, hg, hp, S)
    halo_blocks = blk // SUBLANES

    def cur(sec):
        return pl.BlockSpec((1, blk, wide), lambda b, h, s: (b, s, sec * hg + h))

    def halo(sec):
        return pl.BlockSpec((1, SUBLANES, wide),
                            lambda b, h, s: (b, jnp.maximum(s * halo_blocks - 1, 0), sec * hg + h))

    def cw(sec):
        return pl.BlockSpec((GDN_CONV, wide), lambda b, h, s: (0, sec * hg + h))

    smem = pl.BlockSpec(memory_space=pltpu.SMEM)
    out = pl.pallas_call(
        functools.partial(_gdn_kernel, blk=blk, hp=hp),
        out_shape=jax.ShapeDtypeStruct((B, S, GDN_WIDTH), BF16),
        grid=(B, hg, S // blk),
        in_specs=[smem, smem,
                  cur(0), cur(1), cur(2), halo(0), halo(1), halo(2),
                  cw(0), cw(1), cw(2),
                  pl.BlockSpec((1, 1, 1, hp, blk), lambda b, h, s: (b, 0, h, 0, s)),
                  pl.BlockSpec((1, 1, 1, hp, blk), lambda b, h, s: (b, 1, h, 0, s)),
                  pl.BlockSpec((1, LANES), lambda b, h, s: (0, 0)),
                  pl.BlockSpec((1, blk, wide), lambda b, h, s: (b, s, h))],
        out_specs=pl.BlockSpec((1, blk, wide), lambda b, h, s: (b, s, h)),
        scratch_shapes=[pltpu.VMEM((hp, GDN_HEAD_DIM, GDN_HEAD_DIM), F32)]
                       + [pltpu.VMEM((wide // LANES, 2 * (blk + SUBLANES), LANES), F32) for _ in range(3)],
        compiler_params=_cparams("parallel", "parallel", "arbitrary"),
        name="gated_deltanet",
    )(a_log.astype(F32), dt_bias.astype(F32), pre3, pre3, pre3, pre3, pre3, pre3,
      conv_w, conv_w, conv_w, ab5, ab5, norm_g.reshape(1, LANES).astype(F32), gate3)
    return out.reshape(B * S, GDN_WIDTH)


def _pad_cols(w, width):
    return jnp.zeros((w.shape[0], width), w.dtype).at[:, :w.shape[1]].set(w)


def _attention_layer(x, xbf, B, S, w_in, forget_bias, w_out, ln_g, ln_b,
                     w_gate, w_up, w_down, ln2_g, ln2_b):
    W = ATT_WIDTH
    H = N_HEADS_ATT
    w_fox = jnp.concatenate([w_in[:, :W] * (HEAD_DIM ** -0.5 * LOG2E), w_in[:, W:3 * W]], axis=1).astype(BF16)
    w_dil = jnp.concatenate([w_in[:, 3 * W + H:], _pad_cols(w_in[:, 3 * W:3 * W + H], LANES)], axis=1).astype(BF16)
    qkv_fox = _proj(xbf, w_fox, BF16)
    qkv_dil = _proj(xbf, w_dil, F32)
    f_rows = qkv_dil[:, 3 * W:3 * W + H].reshape(B, S, H).transpose(0, 2, 1)
    f_cum = _forget_cumsum(f_rows, forget_bias)
    o_fox = _fox_attention(qkv_fox, f_cum, B, S)
    o_dil = _dilated_attention(qkv_dil, B, S)
    x1 = _attn_out_ln(o_fox, o_dil, w_out.astype(BF16), x, ln_g, ln_b)
    return _ffn_dense_ln(x1, w_gate.astype(BF16), w_up.astype(BF16), w_down.astype(BF16),
                         ln2_g, ln2_b)


def _slot_indices(route, tm):
    N = route.shape[0]
    experts = route[:, 0:2].astype(jnp.int32).reshape(2 * N)
    onehot = (experts[:, None] == jnp.arange(N_EXPERTS, dtype=jnp.int32)[None, :]).astype(jnp.int32)
    csum = jnp.cumsum(onehot, axis=0)
    counts = csum[-1]
    padded = ((counts + tm - 1) // tm) * tm
    ends = jnp.cumsum(padded)
    starts = ends - padded
    dest = jnp.sum(onehot * (csum - 1 + starts[None, :]), axis=1).astype(jnp.int32)
    n_tiles = (2 * N) // tm + N_EXPERTS
    tile_start = jnp.arange(n_tiles, dtype=jnp.int32) * tm
    tile_expert = jnp.minimum(jnp.sum((tile_start[:, None] >= ends[None, :]).astype(jnp.int32), axis=1),
                              N_EXPERTS - 1).astype(jnp.int32)
    n_active = (ends[-1] // tm).astype(jnp.int32).reshape(1)
    sel = (tile_expert[:, None] == jnp.arange(N_EXPERTS, dtype=jnp.int32)[None, :]).astype(jnp.int32)
    tile_valid = jnp.clip(jnp.sum(sel * (starts + counts)[None, :], axis=1) - tile_start, 0, tm).astype(jnp.int32)
    return dest, tile_expert, jnp.concatenate([n_active, tile_valid]), n_tiles


def _deltanet_layer(x, xbf, B, S, w_in, conv_w, a_log, dt_bias, norm_g, w_out, ln_g, ln_b,
                    router, w_gate, w_up, w_down, ln2_g, ln2_b):
    N = B * S
    W = GDN_WIDTH
    H = N_HEADS_GDN
    w_qkv = jnp.concatenate([w_in[:, :3 * W], _pad_cols(w_in[:, 3 * W:3 * W + 2 * H], 2 * LANES)], axis=1).astype(BF16)
    w_gt = w_in[:, 3 * W + 2 * H:].astype(BF16)
    pre = _proj(xbf, w_qkv, F32)
    gate = _proj(xbf, w_gt, F32)
    ab_rows = pre[:, 3 * W:3 * W + 2 * H].reshape(B, S, 2 * H).transpose(0, 2, 1).reshape(B, 2 * H, 1, S)
    o = _gated_deltanet(pre, ab_rows, gate, conv_w.astype(F32), a_log, dt_bias, norm_g, B, S)
    xt, route = _gdn_out_ln_route(o, w_out.astype(BF16), x, ln_g, ln_b, router)

    tm = min(FFN_TM, N)
    dest, tile_expert, n_active, n_tiles = _slot_indices(route, tm)
    dest8 = dest * SUBLANES
    xs_init = jnp.zeros((n_tiles * tm * SUBLANES, LANES), F32)
    xs = _dispatch_rows(xt, dest8, xs_init)
    ys = _ffn_grouped(xs, tile_expert, n_active,
                      w_gate, w_up, w_down, tm)
    return _moe_combine_ln(xt, ys, dest8, route, ln2_g, ln2_b)


def kernel(x, attn_w_in, fox_forget_bias, attn_w_out, ln_attn_g, ln_attn_b, ffn_w_gate, ffn_w_up,
           ffn_w_down, ln_ffn_g, ln_ffn_b, gdn_w_in, gdn_conv_w, gdn_a_log, gdn_dt_bias, gdn_norm_g,
           gdn_w_out, ln_gdn_g, ln_gdn_b, moe_router, moe_w_gate, moe_w_up, moe_w_down, ln_moe_g,
           ln_moe_b):
    B, S, D = x.shape
    x2 = x.reshape(B * S, D)
    x2bf = x2.astype(BF16)
    x2, x2bf = _attention_layer(x2, x2bf, B, S, attn_w_in[0], fox_forget_bias[0], attn_w_out[0],
                                ln_attn_g[0], ln_attn_b[0], ffn_w_gate[0], ffn_w_up[0], ffn_w_down[0],
                                ln_ffn_g[0], ln_ffn_b[0])
    y = _deltanet_layer(x2, x2bf, B, S, gdn_w_in[0], gdn_conv_w[0], gdn_a_log[0], gdn_dt_bias[0],
                        gdn_norm_g[0], gdn_w_out[0], ln_gdn_g[0], ln_gdn_b[0], moe_router[0],
                        moe_w_gate[0], moe_w_up[0], moe_w_down[0], ln_moe_g[0], ln_moe_b[0])
    return y.reshape(B, S, D)
```

```python
import functools

import numpy as np
import jax
import jax.numpy as jnp
from jax import lax
from jax.experimental import pallas as pl
from jax.experimental.pallas import tpu as pltpu

F32 = jnp.float32
BF16 = jnp.bfloat16
HIGHEST = lax.Precision.HIGHEST

LANES = 128
SUBLANES = 8
VMEM_LIMIT = 52 * 1024 * 1024

HEAD_DIM = 64
N_HEADS_ATT = 8
ATT_WIDTH = N_HEADS_ATT * HEAD_DIM
QUERY_BLOCK = 128
DIL_PATTERNS = ((128, 1), (512, 4), (2048, 16))
GDN_HEAD_DIM = 128
N_HEADS_GDN = 8
GDN_WIDTH = N_HEADS_GDN * GDN_HEAD_DIM
GDN_CONV = 4
GDN_CHUNK = 64
N_EXPERTS = 8
DEPTH = 2
DEEPNORM_ALPHA = (2.0 * DEPTH) ** 0.25
LN_EPS = 1e-5
RMS_EPS = 1e-6
NEG_INF = -1e30
LOG2E = 1.4426950408889634

PROJ_TM = 1024
PROJ_TN = 1664
FOX_TQ = 1024
FOX_GROUP = 512
FOX_STRIP = 32
DIL_GROUP = 8
LN_TM = 512
FFN_TM = 1024
FFN_TF = 512
GDN_BLK = 128
GDN_HEADS_PER_STEP = 8
DISPATCH_CHUNK = 512


def _cparams(*sem):
    return pltpu.CompilerParams(dimension_semantics=sem, vmem_limit_bytes=VMEM_LIMIT)


def _iota(shape, dim):
    return lax.broadcasted_iota(jnp.int32, shape, dim)


def _silu(x):
    return x * jax.nn.sigmoid(x)


def _row_to_col(row):
    n = row.shape[1]
    eye = _iota((LANES, LANES), 0) == _iota((LANES, LANES), 1)
    cols = []
    for c in range(n // LANES):
        seg = row[:, c * LANES:(c + 1) * LANES]
        cols.append(jnp.sum(jnp.where(eye, seg, 0.0), axis=1, keepdims=True))
    return cols[0] if len(cols) == 1 else jnp.concatenate(cols, axis=0)


def _layer_norm_rows(z, g, b):
    mu = jnp.mean(z, axis=-1, keepdims=True)
    zc = z - mu
    var = jnp.mean(zc * zc, axis=-1, keepdims=True)
    return zc * lax.rsqrt(var + LN_EPS) * g + b


def _proj_kernel(x_ref, w_ref, o_ref):
    o_ref[...] = jnp.dot(x_ref[...], w_ref[...], preferred_element_type=F32).astype(o_ref.dtype)


def _proj(x, w, out_dtype):
    M, K = x.shape
    C = w.shape[1]
    tm = min(PROJ_TM, M)
    tn = min(PROJ_TN, C)
    assert M % tm == 0 and C % tn == 0
    return pl.pallas_call(
        _proj_kernel,
        out_shape=jax.ShapeDtypeStruct((M, C), out_dtype),
        grid=(M // tm, C // tn),
        in_specs=[pl.BlockSpec((tm, K), lambda i, j: (i, 0)),
                  pl.BlockSpec((K, tn), lambda i, j: (0, j))],
        out_specs=pl.BlockSpec((tm, tn), lambda i, j: (i, j)),
        compiler_params=_cparams("parallel", "parallel"),
        name="proj",
    )(x, w)


def _forget_cumsum_kernel(f_ref, b_ref, o_ref):
    S = f_ref.shape[2]
    z = f_ref[0] + b_ref[...]
    lf = (jnp.minimum(z, 0.0) - jnp.log1p(jnp.exp(-jnp.abs(z)))) * LOG2E
    upper = (_iota((LANES, LANES), 0) <= _iota((LANES, LANES), 1)).astype(F32)
    carry = jnp.zeros((z.shape[0], 1), F32)
    for c in range(S // LANES):
        seg = jnp.dot(lf[:, c * LANES:(c + 1) * LANES], upper, precision=HIGHEST,
                      preferred_element_type=F32) + carry
        o_ref[0, :, c * LANES:(c + 1) * LANES] = seg
        carry = seg[:, LANES - 1:LANES]


def _forget_cumsum(f_rows, bias):
    B, H, S = f_rows.shape
    return pl.pallas_call(
        _forget_cumsum_kernel,
        out_shape=jax.ShapeDtypeStruct((B, H, S), F32),
        grid=(B,),
        in_specs=[pl.BlockSpec((1, H, S), lambda b: (b, 0, 0)),
                  pl.BlockSpec((H, 1), lambda b: (0, 0))],
        out_specs=pl.BlockSpec((1, H, S), lambda b: (b, 0, 0)),
        compiler_params=_cparams("parallel"),
        name="forget_cumsum",
    )(f_rows, bias.reshape(H, 1).astype(F32))


def _fox_kernel(q_ref, k_ref, v_ref, fq_ref, fk_ref, o_ref,
                s_scr, p_scr, m_scr, l_scr, a_scr, acc_scr, fq_scr, *, tq):
    G = min(FOX_GROUP, tq)
    R = FOX_STRIP
    i = pl.program_id(2)
    contract_last = (((1,), (1,)), ((), ()))
    units = [(hh, g) for hh in range(2) for g in range(tq // G)]
    lane_g = _iota((G, LANES), 1)
    qh = []
    for hh, g in units:
        qg = q_ref[0, g * G:(g + 1) * G, :]
        qh.append(jnp.where((lane_g < HEAD_DIM) == (hh == 0), qg, jnp.zeros_like(qg)))
    for hh in range(2):
        fq_scr[hh] = jnp.broadcast_to(_row_to_col(fq_ref[0, 0, 0, hh:hh + 1, :]), (tq, LANES))
    m_scr[...] = jnp.full(m_scr.shape, NEG_INF, F32)
    l_scr[...] = jnp.zeros(l_scr.shape, F32)
    acc_scr[...] = jnp.zeros(acc_scr.shape, F32)

    def kv_block(j, masked):
        kstart = pl.multiple_of(j * tq, tq)

        def ncols(g):
            return (g + 1) * G if masked else tq

        def scores(u):
            hh, g = units[u]
            n = ncols(g)
            kb = k_ref[0, pl.ds(kstart, n), :]
            s_scr[u, :, :n] = (lax.dot_general(qh[u], kb, contract_last, preferred_element_type=F32)
                               - fk_ref[0, 0, j, hh:hh + 1, :n])

        def strips(u):
            hh, g = units[u]
            n_all = ncols(g)
            for r in range(G // R):
                rows = slice(r * R, (r + 1) * R)
                grow = slice(g * G + r * R, g * G + (r + 1) * R)
                n = n_all
                if masked:
                    n = min(n_all, -(-(g * G + (r + 1) * R) // LANES) * LANES)
                    if n < n_all:
                        p_scr[u, rows, n:n_all] = jnp.zeros((R, n_all - n), BF16)
                s = s_scr[u, rows, :n]
                if masked:
                    s = jnp.where(_iota((R, n), 1) <= _iota((R, n), 0) + (g * G + r * R), s, NEG_INF)
                fq = fq_scr[hh, grow, :]
                m_old = m_scr[hh, grow, :]
                m_new = jnp.maximum(m_old, jnp.max(s, axis=-1, keepdims=True) + fq)
                p = jnp.exp2(s - jnp.concatenate([m_new - fq] * (n // LANES), axis=1))
                alpha = jnp.exp2(m_old - m_new)
                a_scr[hh, grow, :] = alpha
                l_scr[hh, grow, :] = alpha * l_scr[hh, grow, :] + jnp.sum(p, axis=-1, keepdims=True)
                m_scr[hh, grow, :] = m_new
                p_scr[u, rows, :n] = p.astype(BF16)

        def values(u):
            hh, g = units[u]
            n = ncols(g)
            grow = slice(g * G, (g + 1) * G)
            vb = v_ref[0, pl.ds(kstart, n), :]
            acc_scr[hh, grow, :] = (a_scr[hh, grow, :] * acc_scr[hh, grow, :]
                                    + jnp.dot(p_scr[u, :, :n], vb, preferred_element_type=F32))

        scores(0)
        for u in range(1, len(units)):
            scores(u)
            strips(u - 1)
            values(u - 1)
        strips(len(units) - 1)
        values(len(units) - 1)

    def body(j, carry):
        kv_block(j, False)
        return carry

    lax.fori_loop(0, i, body, 0)
    kv_block(i, True)
    lane = _iota((tq, LANES), 1)
    o_ref[0] = jnp.where(lane < HEAD_DIM, acc_scr[0] / l_scr[0], acc_scr[1] / l_scr[1]).astype(o_ref.dtype)


def _fox_attention(qkv, f_cum, B, S):
    tq = min(FOX_TQ, S)
    nk = S // tq
    grp = min(FOX_GROUP, tq)
    n_units = 2 * (tq // grp)
    n_pairs = N_HEADS_ATT // 2
    qkv3 = qkv.reshape(B, S, 3 * ATT_WIDTH)
    f5 = f_cum.reshape(B, n_pairs, 2, nk, tq).transpose(0, 1, 3, 2, 4)
    out = pl.pallas_call(
        functools.partial(_fox_kernel, tq=tq),
        out_shape=jax.ShapeDtypeStruct((B, S, ATT_WIDTH), BF16),
        grid=(B, n_pairs, nk),
        in_specs=[pl.BlockSpec((1, tq, LANES), lambda b, p, i: (b, i, p)),
                  pl.BlockSpec((1, S, LANES), lambda b, p, i: (b, 0, n_pairs + p)),
                  pl.BlockSpec((1, S, LANES), lambda b, p, i: (b, 0, 2 * n_pairs + p)),
                  pl.BlockSpec((1, 1, 1, 2, tq), lambda b, p, i: (b, p, i, 0, 0)),
                  pl.BlockSpec((1, 1, nk, 2, tq), lambda b, p, i: (b, p, 0, 0, 0))],
        out_specs=pl.BlockSpec((1, tq, LANES), lambda b, p, i: (b, i, p)),
        scratch_shapes=[pltpu.VMEM((n_units, grp, tq), F32), pltpu.VMEM((n_units, grp, tq), BF16)]
                       + [pltpu.VMEM((2, tq, LANES), F32) for _ in range(5)],
        compiler_params=_cparams("parallel", "parallel", "arbitrary"),
        name="fox_attention",
    )(qkv3, qkv3, qkv3, f5, f5)
    return out.reshape(B * S, ATT_WIDTH)


def _dil_kernel(slope_ref, q_ref, k_ref, v_ref, o_ref, m0_scr, m1_scr, l_scr, acc_scr, *, S):
    QB = QUERY_BLOCK
    G = DIL_GROUP
    p_idx = pl.program_id(1)
    lane = _iota((QB, LANES), 1)
    head0 = lane < HEAD_DIM
    contract_last = (((1,), (1,)), ((), ()))
    qscale = HEAD_DIM ** -0.5 * LOG2E
    slopes = [slope_ref[2 * p_idx + hh] * LOG2E for hh in range(2)]
    delta_cur = _iota((QB, QB), 0) - _iota((QB, QB), 1)
    delta_two = QB + _iota((QB, 2 * QB), 0) - _iota((QB, 2 * QB), 1)

    def rows(start, dil):
        return pl.ds(start, QB, stride=dil) if dil > 1 else pl.ds(start, QB)

    def update(q0s, kp0s, dil, biases, first):
        loaded = []
        for g, q0 in enumerate(q0s):
            rq = rows(q0, dil)
            q = q_ref[0, rq, :] * qscale
            kc = k_ref[0, rq, :]
            vc = v_ref[0, rq, :]
            if kp0s is not None:
                rp = rows(kp0s[g], dil)
                kc = jnp.concatenate([k_ref[0, rp, :], kc], axis=0)
                vc = jnp.concatenate([v_ref[0, rp, :], vc], axis=0)
            old = None if first else (m0_scr[rq, :], m1_scr[rq, :], l_scr[rq, :], acc_scr[rq, :])
            loaded.append((rq, q, kc.astype(BF16), vc.astype(BF16), old))
        units = [(g, hh) for g in range(len(loaded)) for hh in range(2)]
        reps = loaded[0][2].shape[0] // LANES
        s_u = [lax.dot_general(jnp.where(head0 == (hh == 0), loaded[g][1], 0.0).astype(BF16), loaded[g][2],
                               contract_last, preferred_element_type=F32) - biases[hh] for g, hh in units]
        m_u = [jnp.broadcast_to(jnp.max(s, axis=-1, keepdims=True), (QB, LANES)) for s in s_u]
        if not first:
            m_u = [jnp.maximum(loaded[g][4][hh], m) for (g, hh), m in zip(units, m_u)]
            a_u = [jnp.exp2(loaded[g][4][hh] - m) for (g, hh), m in zip(units, m_u)]
        p_u = [jnp.exp2(s - jnp.concatenate([m] * reps, axis=1)) for s, m in zip(s_u, m_u)]
        ps_u = [jnp.sum(p, axis=-1, keepdims=True) for p in p_u]
        pv_u = [jnp.dot(p.astype(BF16), loaded[g][3], preferred_element_type=F32)
                for (g, hh), p in zip(units, p_u)]
        for g in range(len(loaded)):
            rq, old = loaded[g][0], loaded[g][4]
            l_new = jnp.where(head0, ps_u[2 * g], ps_u[2 * g + 1])
            acc_new = jnp.where(head0, pv_u[2 * g], pv_u[2 * g + 1])
            if not first:
                alpha = jnp.where(head0, a_u[2 * g], a_u[2 * g + 1])
                l_new = alpha * old[2] + l_new
                acc_new = alpha * old[3] + acc_new
            m0_scr[rq, :] = m_u[2 * g]
            m1_scr[rq, :] = m_u[2 * g + 1]
            l_scr[rq, :] = l_new
            acc_scr[rq, :] = acc_new

    for branch, (window, dil) in enumerate(sorted(DIL_PATTERNS, key=lambda wd: -wd[1])):
        span = window // dil
        assert span <= QB and (S // dil) % QB == 0
        nblk = S // dil // QB
        first = branch == 0

        def masked_bias(delta, hh, dil=dil, span=span):
            return jnp.where((delta >= 0) & (delta <= span), slopes[hh] * (delta * dil).astype(F32), -NEG_INF)

        bias_cur = [masked_bias(delta_cur, hh) for hh in range(2)]
        bias_two = [masked_bias(delta_two, hh) for hh in range(2)]

        ga = min(G, dil)

        def head_step(t, carry, dil=dil, ga=ga, bias_cur=bias_cur, first=first):
            update([t * ga + g for g in range(ga)], None, dil, bias_cur, first)
            return carry

        lax.fori_loop(0, dil // ga, head_step, 0)

        def starts(idx, dil=dil):
            q0 = (idx % dil) + (1 + idx // dil) * (QB * dil)
            return q0, q0 - QB * dil

        def tail_step(t, carry, base=0, count=G, dil=dil, bias_two=bias_two, first=first):
            pairs = [starts(base + t * count + g) for g in range(count)]
            update([a for a, _ in pairs], [b for _, b in pairs], dil, bias_two, first)
            return carry

        n_tail = dil * (nblk - 1)
        lax.fori_loop(0, n_tail // G, tail_step, 0)
        if n_tail % G:
            tail_step(0, 0, base=(n_tail // G) * G, count=n_tail % G)
    o_ref[0] = (acc_scr[...] / l_scr[...]).astype(o_ref.dtype)


def _dilated_attention(qkv, B, S):
    n_pairs = N_HEADS_ATT // 2
    slopes = jnp.asarray(2.0 ** (-8.0 * (np.arange(N_HEADS_ATT) + 1) / N_HEADS_ATT), dtype=F32)
    qkv3 = qkv.reshape(B, S, qkv.shape[1])
    out = pl.pallas_call(
        functools.partial(_dil_kernel, S=S),
        out_shape=jax.ShapeDtypeStruct((B, S, ATT_WIDTH), BF16),
        grid=(B, n_pairs),
        in_specs=[pl.BlockSpec(memory_space=pltpu.SMEM),
                  pl.BlockSpec((1, S, LANES), lambda b, p: (b, 0, p)),
                  pl.BlockSpec((1, S, LANES), lambda b, p: (b, 0, n_pairs + p)),
                  pl.BlockSpec((1, S, LANES), lambda b, p: (b, 0, 2 * n_pairs + p))],
        out_specs=pl.BlockSpec((1, S, LANES), lambda b, p: (b, 0, p)),
        scratch_shapes=[pltpu.VMEM((S, LANES), F32) for _ in range(4)],
        compiler_params=_cparams("parallel", "parallel"),
        name="dilated_attention",
    )(slopes, qkv3, qkv3, qkv3)
    return out.reshape(B * S, ATT_WIDTH)


def _attn_out_kernel(a0_ref, a1_ref, w_ref, x_ref, g_ref, b_ref, y_ref):
    k0 = a0_ref.shape[1]
    mix = jnp.dot(a0_ref[...], w_ref[:k0, :], preferred_element_type=F32)
    mix = mix + jnp.dot(a1_ref[...], w_ref[k0:, :], preferred_element_type=F32)
    y_ref[...] = _layer_norm_rows(DEEPNORM_ALPHA * x_ref[...] + mix, g_ref[...], b_ref[...])


def _attn_out_ln(a0, a1, w, x, g, b):
    M, D = x.shape
    tm = min(LN_TM, M)
    row = lambda i: (i, 0)
    const = lambda i: (0, 0)
    return pl.pallas_call(
        _attn_out_kernel,
        out_shape=jax.ShapeDtypeStruct((M, D), F32),
        grid=(M // tm,),
        in_specs=[pl.BlockSpec((tm, a0.shape[1]), row), pl.BlockSpec((tm, a1.shape[1]), row),
                  pl.BlockSpec(w.shape, const), pl.BlockSpec((tm, D), row),
                  pl.BlockSpec((1, D), const), pl.BlockSpec((1, D), const)],
        out_specs=pl.BlockSpec((tm, D), row),
        compiler_params=_cparams("parallel"),
        name="attn_out_ln",
    )(a0, a1, w, x, g.reshape(1, D), b.reshape(1, D))


def _gdn_out_kernel(a_ref, w_ref, x_ref, g_ref, b_ref, r_ref, yt_ref, route_ref):
    tm = x_ref.shape[0]
    mix = jnp.dot(a_ref[...], w_ref[...], preferred_element_type=F32)
    y = _layer_norm_rows(DEEPNORM_ALPHA * x_ref[...] + mix, g_ref[...], b_ref[...])
    for s in range(SUBLANES):
        yt_ref[pl.ds(s, tm, stride=SUBLANES), :] = y[:, s * LANES:(s + 1) * LANES]
    y_hi = y.astype(BF16)
    y_lo = (y - y_hi.astype(F32)).astype(BF16)
    both = jnp.dot(y_hi, r_ref[...], preferred_element_type=F32)
    logits = both[:, :LANES] + (both[:, LANES:] + jnp.dot(y_lo, r_ref[:, :LANES], preferred_element_type=F32))
    lane = _iota((tm, LANES), 1)
    logits = jnp.where(lane < N_EXPERTS, logits, -jnp.inf)
    m1 = jnp.max(logits, axis=-1, keepdims=True)
    i1 = jnp.min(jnp.where(logits == m1, lane, LANES), axis=-1, keepdims=True)
    rest = jnp.where(lane == i1, -jnp.inf, logits)
    m2 = jnp.max(rest, axis=-1, keepdims=True)
    i2 = jnp.min(jnp.where(rest == m2, lane, LANES), axis=-1, keepdims=True)
    e2 = jnp.exp(m2 - m1)
    w1 = 1.0 / (1.0 + e2)
    w2 = e2 / (1.0 + e2)
    route = jnp.where(lane == 0, i1.astype(F32),
                      jnp.where(lane == 1, i2.astype(F32),
                                jnp.where(lane == 2, w1, jnp.where(lane == 3, w2, 0.0))))
    route_ref[...] = route


def _gdn_out_ln_route(a, w, x, g, b, router):
    M, D = x.shape
    tm = min(LN_TM, M)
    row = lambda i: (i, 0)
    const = lambda i: (0, 0)
    r_pad = jnp.zeros((D, LANES), F32).at[:, :N_EXPERTS].set(router.astype(F32))
    r_hi = r_pad.astype(BF16)
    r_lo = (r_pad - r_hi.astype(F32)).astype(BF16)
    r_pad = jnp.concatenate([r_hi, r_lo], axis=1)
    return pl.pallas_call(
        _gdn_out_kernel,
        out_shape=(jax.ShapeDtypeStruct((M * SUBLANES, LANES), F32),
                   jax.ShapeDtypeStruct((M, LANES), F32)),
        grid=(M // tm,),
        in_specs=[pl.BlockSpec((tm, a.shape[1]), row), pl.BlockSpec(w.shape, const),
                  pl.BlockSpec((tm, D), row), pl.BlockSpec((1, D), const),
                  pl.BlockSpec((1, D), const), pl.BlockSpec((D, 2 * LANES), const)],
        out_specs=(pl.BlockSpec((tm * SUBLANES, LANES), row), pl.BlockSpec((tm, LANES), row)),
        compiler_params=_cparams("parallel"),
        name="gdn_out_ln_route",
    )(a, w, x, g.reshape(1, D), b.reshape(1, D), r_pad)


def _ffn_dense_kernel(wg_ref, wu_ref, wd_ref, x_ref, g_ref, b_ref, y_ref, ybf_ref, acc_scr, xbf_scr):
    j = pl.program_id(1)

    @pl.when(j == 0)
    def _():
        acc_scr[...] = jnp.zeros_like(acc_scr)
        xbf_scr[...] = x_ref[...].astype(BF16)

    x = xbf_scr[...]
    hg = jnp.dot(x, wg_ref[...], preferred_element_type=F32)
    hu = jnp.dot(x, wu_ref[...], preferred_element_type=F32)
    h = (_silu(hg) * hu).astype(BF16)
    acc_scr[...] += jnp.dot(h, wd_ref[...], preferred_element_type=F32)

    @pl.when(j == pl.num_programs(1) - 1)
    def _():
        y = _layer_norm_rows(DEEPNORM_ALPHA * x_ref[...] + acc_scr[...], g_ref[...], b_ref[...])
        y_ref[...] = y
        ybf_ref[...] = y.astype(BF16)


def _ffn_dense_ln(x, wg, wu, wd, g, b):
    M, D = x.shape
    FF = wg.shape[1]
    tm = min(FFN_TM, M)
    tf = FFN_TF
    assert FF % tf == 0
    row = lambda i, j: (i, 0)
    const = lambda i, j: (0, 0)
    return pl.pallas_call(
        _ffn_dense_kernel,
        out_shape=(jax.ShapeDtypeStruct((M, D), F32), jax.ShapeDtypeStruct((M, D), BF16)),
        grid=(M // tm, FF // tf),
        in_specs=[pl.BlockSpec((D, tf), lambda i, j: (0, j)),
                  pl.BlockSpec((D, tf), lambda i, j: (0, j)),
                  pl.BlockSpec((tf, D), lambda i, j: (j, 0)),
                  pl.BlockSpec((tm, D), row),
                  pl.BlockSpec((1, D), const), pl.BlockSpec((1, D), const)],
        out_specs=(pl.BlockSpec((tm, D), row), pl.BlockSpec((tm, D), row)),
        scratch_shapes=[pltpu.VMEM((tm, D), F32), pltpu.VMEM((tm, D), BF16)],
        compiler_params=_cparams("parallel", "arbitrary"),
        name="ffn_dense_ln",
    )(wg, wu, wd, x, g.reshape(1, D), b.reshape(1, D))


def _ffn_grouped_kernel(te_ref, na_ref, xs_hbm, wg_ref, wu_ref, wd_ref, ys_hbm,
                        xin_scr, yout_scr, xbf_scr, acc_scr, sem_in, sem_out):
    i = pl.program_id(0)
    j = pl.program_id(1)
    nj = pl.num_programs(1)
    tm = acc_scr.shape[0]
    rows = tm * SUBLANES
    na = na_ref[0]
    active = i < na
    slot = i % 2

    def x_copy(tile, sl):
        src = xs_hbm.at[pl.ds(pl.multiple_of(tile * rows, rows), rows), :]
        return pltpu.make_async_copy(src, xin_scr.at[sl], sem_in.at[sl])

    def y_copy(tile, sl):
        dst = ys_hbm.at[pl.ds(pl.multiple_of(tile * rows, rows), rows), :]
        return pltpu.make_async_copy(yout_scr.at[sl], dst, sem_out.at[sl])

    @pl.when(active & (j == 0))
    def _():
        @pl.when(i == 0)
        def _():
            x_copy(0, 0).start()

        x_copy(i, slot).wait()

        @pl.when(i + 1 < na)
        def _():
            x_copy(i + 1, 1 - slot).start()

        acc_scr[...] = jnp.zeros_like(acc_scr)
        for s in range(SUBLANES):
            xbf_scr[:, s * LANES:(s + 1) * LANES] = xin_scr[slot, pl.ds(s, tm, stride=SUBLANES), :].astype(BF16)

    @pl.when(active)
    def _():
        x = xbf_scr[...]
        hg = jnp.dot(x, wg_ref[0].astype(BF16), preferred_element_type=F32)
        hu = jnp.dot(x, wu_ref[0].astype(BF16), preferred_element_type=F32)
        h = (_silu(hg) * hu).astype(BF16)
        acc_scr[...] += jnp.dot(h, wd_ref[0].astype(BF16), preferred_element_type=F32)

    @pl.when(active & (j == nj - 1))
    def _():
        @pl.when(i >= 2)
        def _():
            y_copy(i - 2, slot).wait()

        for s in range(SUBLANES):
            yout_scr[slot, pl.ds(s, tm, stride=SUBLANES), :] = acc_scr[:, s * LANES:(s + 1) * LANES]
        y_copy(i, slot).start()

        @pl.when(i == na - 1)
        def _():
            @pl.when(i >= 1)
            def _():
                y_copy(i - 1, 1 - slot).wait()

            y_copy(i, slot).wait()

    @pl.when(jnp.logical_not(active) & (j == 0))
    def _():
        yout_scr[0] = jnp.zeros(yout_scr.shape[1:], F32)
        y_copy(i, 0).start()
        y_copy(i, 0).wait()


def _ffn_grouped(xt, tile_expert, n_active, wg, wu, wd, tm):
    R = xt.shape[0] // SUBLANES
    E, D, FF = wg.shape
    tf = FFN_TF
    nf = FF // tf
    n_tiles = R // tm

    def ff_idx(i, j, na):
        return jnp.where(i < na[0], j, nf - 1)

    return pl.pallas_call(
        _ffn_grouped_kernel,
        out_shape=jax.ShapeDtypeStruct((R * SUBLANES, LANES), F32),
        grid_spec=pltpu.PrefetchScalarGridSpec(
            num_scalar_prefetch=2,
            grid=(n_tiles, nf),
            in_specs=[pl.BlockSpec(memory_space=pl.ANY),
                      pl.BlockSpec((1, D, tf), lambda i, j, te, na: (te[i], 0, ff_idx(i, j, na))),
                      pl.BlockSpec((1, D, tf), lambda i, j, te, na: (te[i], 0, ff_idx(i, j, na))),
                      pl.BlockSpec((1, tf, D), lambda i, j, te, na: (te[i], ff_idx(i, j, na), 0))],
            out_specs=pl.BlockSpec(memory_space=pl.ANY),
            scratch_shapes=[pltpu.VMEM((2, tm * SUBLANES, LANES), F32), pltpu.VMEM((2, tm * SUBLANES, LANES), F32),
                            pltpu.VMEM((tm, D), BF16), pltpu.VMEM((tm, D), F32),
                            pltpu.SemaphoreType.DMA((2,)), pltpu.SemaphoreType.DMA((2,))]),
        compiler_params=_cparams("arbitrary", "arbitrary"),
        name="ffn_grouped",
    )(tile_expert, n_active, xt, wg, wu, wd)


def _row_tile(ref, r):
    return ref.at[pl.ds(pl.multiple_of(r, SUBLANES), SUBLANES), :]


def _dispatch_kernel(dest_ref, xt_ref, xs_in_hbm, xs_hbm, sem):
    del xs_in_hbm
    ch = dest_ref.shape[2] // 2

    def body(t, carry):
        src = _row_tile(xt_ref, t * SUBLANES)
        for k in range(2):
            pltpu.make_async_copy(src, _row_tile(xs_hbm, dest_ref[0, 0, 2 * t + k]), sem).start(priority=k)
        return carry

    lax.fori_loop(0, ch, body, 0, unroll=8)
    for _ in range(2):
        pltpu.make_async_copy(xt_ref, xs_hbm.at[pl.ds(0, ch * SUBLANES), :], sem).wait()


def _dispatch_rows(xt, dest8, xs_init):
    N = xt.shape[0] // SUBLANES
    ch = min(DISPATCH_CHUNK, N)
    dest3 = dest8.reshape(N // ch, 1, 2 * ch)
    return pl.pallas_call(
        _dispatch_kernel,
        out_shape=jax.ShapeDtypeStruct(xs_init.shape, F32),
        grid=(N // ch,),
        in_specs=[pl.BlockSpec((1, 1, 2 * ch), lambda i: (i, 0, 0), memory_space=pltpu.SMEM),
                  pl.BlockSpec((ch * SUBLANES, LANES), lambda i: (i, 0)),
                  pl.BlockSpec(memory_space=pl.ANY)],
        out_specs=pl.BlockSpec(memory_space=pl.ANY),
        scratch_shapes=[pltpu.SemaphoreType.DMA(())],
        input_output_aliases={2: 0},
        compiler_params=_cparams("arbitrary"),
        name="moe_dispatch",
    )(dest3, xt, xs_init)


def _moe_ln_kernel(dest_ref, dnext_ref, xt_ref, ys_hbm, route_ref, g_ref, b_ref, o_ref, ya_scr, yb_scr, sem):
    tm = o_ref.shape[0]
    i = pl.program_id(0)
    slot = i % 2

    def start_gathers(idx_ref, sl):
        def body(t, carry):
            pltpu.make_async_copy(_row_tile(ys_hbm, idx_ref[0, 0, 2 * t]),
                                  _row_tile(ya_scr.at[sl], t * SUBLANES), sem.at[sl]).start(priority=0)
            pltpu.make_async_copy(_row_tile(ys_hbm, idx_ref[0, 0, 2 * t + 1]),
                                  _row_tile(yb_scr.at[sl], t * SUBLANES), sem.at[sl]).start(priority=1)
            return carry

        lax.fori_loop(0, tm, body, 0, unroll=8)

    @pl.when(i == 0)
    def _():
        start_gathers(dest_ref, 0)

    @pl.when(i + 1 < pl.num_programs(0))
    def _():
        start_gathers(dnext_ref, 1 - slot)

    for scr in (ya_scr, yb_scr):
        pltpu.make_async_copy(ys_hbm.at[pl.ds(0, tm * SUBLANES), :], scr.at[slot], sem.at[slot]).wait()

    w1 = jnp.broadcast_to(route_ref[:, 2:3], (tm, LANES))
    w2 = jnp.broadcast_to(route_ref[:, 3:4], (tm, LANES))
    parts = []
    for s in range(SUBLANES):
        rows = pl.ds(s, tm, stride=SUBLANES)
        parts.append(DEEPNORM_ALPHA * xt_ref[rows, :]
                     + (w1 * ya_scr[slot, rows, :] + w2 * yb_scr[slot, rows, :]))
    d_model = SUBLANES * LANES
    mu = jnp.sum(sum(parts), axis=-1, keepdims=True) / d_model
    var = jnp.sum(sum((z - mu) * (z - mu) for z in parts), axis=-1, keepdims=True) / d_model
    rstd = lax.rsqrt(var + LN_EPS)
    for s in range(SUBLANES):
        cols = slice(s * LANES, (s + 1) * LANES)
        o_ref[:, cols] = (parts[s] - mu) * rstd * g_ref[:, cols] + b_ref[:, cols]


def _moe_combine_ln(xt, ys, dest8, route, g, b):
    M = route.shape[0]
    D = SUBLANES * LANES
    tm = min(LN_TM, M)
    row = lambda i: (i, 0)
    const = lambda i: (0, 0)
    n_blocks = M // tm
    dest3 = dest8.reshape(n_blocks, 1, 2 * tm)
    return pl.pallas_call(
        _moe_ln_kernel,
        out_shape=jax.ShapeDtypeStruct((M, D), F32),
        grid=(n_blocks,),
        in_specs=[pl.BlockSpec((1, 1, 2 * tm), lambda i: (i, 0, 0), memory_space=pltpu.SMEM),
                  pl.BlockSpec((1, 1, 2 * tm), lambda i: (jnp.minimum(i + 1, n_blocks - 1), 0, 0),
                               memory_space=pltpu.SMEM),
                  pl.BlockSpec((tm * SUBLANES, LANES), row),
                  pl.BlockSpec(memory_space=pl.ANY),
                  pl.BlockSpec((tm, LANES), row),
                  pl.BlockSpec((1, D), const), pl.BlockSpec((1, D), const)],
        out_specs=pl.BlockSpec((tm, D), row),
        scratch_shapes=[pltpu.VMEM((2, tm * SUBLANES, LANES), F32), pltpu.VMEM((2, tm * SUBLANES, LANES), F32),
                        pltpu.SemaphoreType.DMA((2,))],
        compiler_params=_cparams("arbitrary"),
        name="moe_combine_ln",
    )(dest3, dest3, xt, ys, route, g.reshape(1, D), b.reshape(1, D))


def _gdn_kernel(alog_ref, dt_ref, pq_ref, pk_ref, pv_ref, hq_ref, hk_ref, hv_ref,
                cq_ref, ck_ref, cv_ref, br_ref, ar_ref, ng_ref, gate_ref, o_ref,
                state_scr, sq_scr, sk_scr, sv_scr, *, blk, hp):
    C = GDN_CHUNK
    Dh = GDN_HEAD_DIM
    nchunk = blk // C
    h0 = pl.program_id(1) * hp
    sb = pl.program_id(2)
    heads = range(hp)

    @pl.when(sb == 0)
    def _():
        state_scr[...] = jnp.zeros_like(state_scr)

    have_prev = (sb > 0).astype(F32)

    def conv_silu(cur_ref, halo_ref, w_ref, stage_scr):
        outs = []
        for c in range(stage_scr.shape[0]):
            lanes = slice(c * LANES, (c + 1) * LANES)
            stage_scr[c, pl.ds(0, SUBLANES, stride=2), :] = halo_ref[0, :, lanes] * have_prev
            stage_scr[c, pl.ds(2 * SUBLANES, blk, stride=2), :] = cur_ref[0, :, lanes]
            out = None
            for j in range(GDN_CONV):
                off = 2 * (SUBLANES - (GDN_CONV - 1) + j)
                term = w_ref[j:j + 1, lanes] * stage_scr[c, pl.ds(off, blk, stride=2), :]
                out = term if out is None else out + term
            outs.append(out)
        return _silu(jnp.concatenate(outs, axis=1))

    def split(t):
        return [t[:, hh * Dh:(hh + 1) * Dh] for hh in heads]

    def l2n(t):
        return t * lax.rsqrt(jnp.sum(t * t, axis=-1, keepdims=True) + RMS_EPS)

    q_h = [l2n(t) * (Dh ** -0.5) for t in split(conv_silu(pq_ref, hq_ref, cq_ref, sq_scr))]
    k_h = [l2n(t) for t in split(conv_silu(pk_ref, hk_ref, ck_ref, sk_scr))]
    v_h = split(conv_silu(pv_ref, hv_ref, cv_ref, sv_scr))

    lanes_row = _iota((hp, blk), 0)
    dt_rows = jnp.zeros((hp, blk), F32)
    alog_rows = jnp.zeros((hp, blk), F32)
    for hh in heads:
        dt_rows = jnp.where(lanes_row == hh, dt_ref[h0 + hh], dt_rows)
        alog_rows = jnp.where(lanes_row == hh, alog_ref[h0 + hh], alog_rows)
    beta_rows = jax.nn.sigmoid(br_ref[0, 0, 0])
    za = ar_ref[0, 0, 0] + dt_rows
    g_rows = -jnp.exp(alog_rows) * (jnp.maximum(za, 0.0) + jnp.log1p(jnp.exp(-jnp.abs(za))))
    ri = _iota((blk, blk), 0)
    ci = _iota((blk, blk), 1)
    same = (ri // C) == (ci // C)
    g8 = jnp.concatenate([g_rows, jnp.zeros((SUBLANES - hp, blk), F32)], axis=0) if hp < SUBLANES else g_rows
    gam_rows = jnp.dot(g8, (same & (ri <= ci)).astype(F32), precision=HIGHEST,
                       preferred_element_type=F32)
    gl_rows = jnp.dot(g8, same.astype(F32), precision=HIGHEST,
                      preferred_element_type=F32)
    beta = [_row_to_col(beta_rows[hh:hh + 1, :]) for hh in heads]
    gam = [_row_to_col(gam_rows[hh:hh + 1, :]) for hh in heads]
    gl = [_row_to_col(gl_rows[hh:hh + 1, :]) for hh in heads]
    eg = [jnp.exp(t) for t in gam]
    ekd = [jnp.exp(a - b) for a, b in zip(gl, gam)]

    incl = same & (ri >= ci)
    strict = same & (ri > ci)
    contract_last = (((1,), (1,)), ((), ()))
    decay = [jnp.where(incl, jnp.exp(jnp.where(incl, gam[hh] - gam_rows[hh:hh + 1, :], 0.0)), 0.0)
             for hh in heads]
    kb = [t.astype(BF16) for t in k_h]
    kk = [lax.dot_general(t, t, contract_last, preferred_element_type=F32) for t in kb]
    x_acc = [jnp.where(strict, -(beta[hh] * kk[hh] * decay[hh]), 0.0) for hh in heads]
    pw = x_acc
    for _ in range(int(np.log2(C)) - 1):
        pwb = [t.astype(BF16) for t in pw]
        pw = [jnp.dot(t, t, preferred_element_type=F32) for t in pwb]
        x_acc = [x + p + jnp.dot(p.astype(BF16), x.astype(BF16), preferred_element_type=F32)
                 for x, p in zip(x_acc, pw)]
    rhs = [jnp.concatenate([v_h[hh] * beta[hh], k_h[hh] * (beta[hh] * eg[hh])], axis=1) for hh in heads]
    sol = [r + jnp.dot(x.astype(BF16), r.astype(BF16), preferred_element_type=F32)
           for x, r in zip(x_acc, rhs)]
    u = [t[:, :Dh] for t in sol]
    w_b = [t[:, Dh:].astype(BF16) for t in sol]
    qk = [lax.dot_general(q_h[hh].astype(BF16), kb[hh], contract_last, preferred_element_type=F32) * decay[hh]
          for hh in heads]
    q_dec = [(q_h[hh] * eg[hh]).astype(BF16) for hh in heads]
    k_dec = [(k_h[hh] * ekd[hh]).astype(BF16) for hh in heads]

    state = [state_scr[hh] for hh in heads]
    v_new = [[] for _ in heads]
    o_inter = [[] for _ in heads]
    for c in range(nchunk):
        rows = slice(c * C, (c + 1) * C)
        sbf = [t.astype(BF16) for t in state]
        vn = [u[hh][rows] - jnp.dot(w_b[hh][rows], sbf[hh], preferred_element_type=F32) for hh in heads]
        for hh in heads:
            o_inter[hh].append(jnp.dot(q_dec[hh][rows], sbf[hh], preferred_element_type=F32))
            v_new[hh].append(vn[hh])
        state = [state[hh] * jnp.exp(gl[hh][c * C:c * C + 1, :])
                 + lax.dot_general(k_dec[hh][rows], vn[hh].astype(BF16), (((0,), (0,)), ((), ())),
                                   preferred_element_type=F32) for hh in heads]
    for hh in heads:
        state_scr[hh] = state[hh]
    o = [jnp.concatenate(o_inter[hh], axis=0)
         + jnp.dot(qk[hh].astype(BF16), jnp.concatenate(v_new[hh], axis=0).astype(BF16),
                   preferred_element_type=F32) for hh in heads]
    o = [t * lax.rsqrt(jnp.mean(t * t, axis=-1, keepdims=True) + RMS_EPS) * ng_ref[...] for t in o]
    o = o[0] if hp == 1 else jnp.concatenate(o, axis=1)
    o_ref[0] = (o * _silu(gate_ref[0])).astype(o_ref.dtype)


def _gated_deltanet(pre, ab_rows, gate, conv_w, a_log, dt_bias, norm_g, B, S):
    blk = min(GDN_BLK, S)
    H = N_HEADS_GDN
    hp = GDN_HEADS_PER_STEP
    hg = H // hp
    wide = hp * GDN_HEAD_DIM
    pre3 = pre.reshape(B, S, pre.shape[1])
    gate3 = gate.reshape(B, S, GDN_WIDTH)
    ab5 = ab_rows.reshape(B, 2, hg, hp, S)
    halo_blocks = blk // SUBLANES

    def cur(sec):
        return pl.BlockSpec((1, blk, wide), lambda b, h, s: (b, s, sec * hg + h))

    def halo(sec):
        return pl.BlockSpec((1, SUBLANES, wide),
                            lambda b, h, s: (b, jnp.maximum(s * halo_blocks - 1, 0), sec * hg + h))

    def cw(sec):
        return pl.BlockSpec((GDN_CONV, wide), lambda b, h, s: (0, sec * hg + h))

    smem = pl.BlockSpec(memory_space=pltpu.SMEM)
    out = pl.pallas_call(
        functools.partial(_gdn_kernel, blk=blk, hp=hp),
        out_shape=jax.ShapeDtypeStruct((B, S, GDN_WIDTH), BF16),
        grid=(B, hg, S // blk),
        in_specs=[smem, smem,
                  cur(0), cur(1), cur(2), halo(0), halo(1), halo(2),
                  cw(0), cw(1), cw(2),
                  pl.BlockSpec((1, 1, 1, hp, blk), lambda b, h, s: (b, 0, h, 0, s)),
                  pl.BlockSpec((1, 1, 1, hp, blk), lambda b, h, s: (b, 1, h, 0, s)),
                  pl.BlockSpec((1, LANES), lambda b, h, s: (0, 0)),
                  pl.BlockSpec((1, blk, wide), lambda b, h, s: (b, s, h))],
        out_specs=pl.BlockSpec((1, blk, wide), lambda b, h, s: (b, s, h)),
        scratch_shapes=[pltpu.VMEM((hp, GDN_HEAD_DIM, GDN_HEAD_DIM), F32)]
                       + [pltpu.VMEM((wide // LANES, 2 * (blk + SUBLANES), LANES), F32) for _ in range(3)],
        compiler_params=_cparams("parallel", "parallel", "arbitrary"),
        name="gated_deltanet",
    )(a_log.astype(F32), dt_bias.astype(F32), pre3, pre3, pre3, pre3, pre3, pre3,
      conv_w, conv_w, conv_w, ab5, ab5, norm_g.reshape(1, LANES).astype(F32), gate3)
    return out.reshape(B * S, GDN_WIDTH)


def _pad_cols(w, width):
    return jnp.zeros((w.shape[0], width), w.dtype).at[:, :w.shape[1]].set(w)


def _attention_layer(x, xbf, B, S, w_in, forget_bias, w_out, ln_g, ln_b,
                     w_gate, w_up, w_down, ln2_g, ln2_b):
    W = ATT_WIDTH
    H = N_HEADS_ATT
    w_fox = jnp.concatenate([w_in[:, :W] * (HEAD_DIM ** -0.5 * LOG2E), w_in[:, W:3 * W]], axis=1).astype(BF16)
    w_dil = jnp.concatenate([w_in[:, 3 * W + H:], _pad_cols(w_in[:, 3 * W:3 * W + H], LANES)], axis=1).astype(BF16)
    qkv_fox = _proj(xbf, w_fox, BF16)
    qkv_dil = _proj(xbf, w_dil, F32)
    f_rows = qkv_dil[:, 3 * W:3 * W + H].reshape(B, S, H).transpose(0, 2, 1)
    f_cum = _forget_cumsum(f_rows, forget_bias)
    o_fox = _fox_attention(qkv_fox, f_cum, B, S)
    o_dil = _dilated_attention(qkv_dil, B, S)
    x1 = _attn_out_ln(o_fox, o_dil, w_out.astype(BF16), x, ln_g, ln_b)
    return _ffn_dense_ln(x1, w_gate.astype(BF16), w_up.astype(BF16), w_down.astype(BF16),
                         ln2_g, ln2_b)


def _slot_indices(route, tm):
    N = route.shape[0]
    experts = route[:, 0:2].astype(jnp.int32).reshape(2 * N)
    onehot = (experts[:, None] == jnp.arange(N_EXPERTS, dtype=jnp.int32)[None, :]).astype(jnp.int32)
    csum = jnp.cumsum(onehot, axis=0)
    counts = csum[-1]
    padded = ((counts + tm - 1) // tm) * tm
    ends = jnp.cumsum(padded)
    starts = ends - padded
    dest = jnp.sum(onehot * (csum - 1 + starts[None, :]), axis=1).astype(jnp.int32)
    n_tiles = (2 * N) // tm + N_EXPERTS
    tile_start = jnp.arange(n_tiles, dtype=jnp.int32) * tm
    tile_expert = jnp.minimum(jnp.sum((tile_start[:, None] >= ends[None, :]).astype(jnp.int32), axis=1),
                              N_EXPERTS - 1).astype(jnp.int32)
    n_active = (ends[-1] // tm).astype(jnp.int32).reshape(1)
    return dest, tile_expert, n_active, n_tiles


def _deltanet_layer(x, xbf, B, S, w_in, conv_w, a_log, dt_bias, norm_g, w_out, ln_g, ln_b,
                    router, w_gate, w_up, w_down, ln2_g, ln2_b):
    N = B * S
    W = GDN_WIDTH
    H = N_HEADS_GDN
    w_qkv = jnp.concatenate([w_in[:, :3 * W], _pad_cols(w_in[:, 3 * W:3 * W + 2 * H], 2 * LANES)], axis=1).astype(BF16)
    w_gt = w_in[:, 3 * W + 2 * H:].astype(BF16)
    pre = _proj(xbf, w_qkv, F32)
    gate = _proj(xbf, w_gt, F32)
    ab_rows = pre[:, 3 * W:3 * W + 2 * H].reshape(B, S, 2 * H).transpose(0, 2, 1).reshape(B, 2 * H, 1, S)
    o = _gated_deltanet(pre, ab_rows, gate, conv_w.astype(F32), a_log, dt_bias, norm_g, B, S)
    xt, route = _gdn_out_ln_route(o, w_out.astype(BF16), x, ln_g, ln_b, router)

    tm = min(FFN_TM, N)
    dest, tile_expert, n_active, n_tiles = _slot_indices(route, tm)
    dest8 = dest * SUBLANES
    xs_init = jnp.zeros((n_tiles * tm * SUBLANES, LANES), F32)
    xs = _dispatch_rows(xt, dest8, xs_init)
    ys = _ffn_grouped(xs, tile_expert, n_active,
                      w_gate, w_up, w_down, tm)
    return _moe_combine_ln(xt, ys, dest8, route, ln2_g, ln2_b)


def kernel(x, attn_w_in, fox_forget_bias, attn_w_out, ln_attn_g, ln_attn_b, ffn_w_gate, ffn_w_up,
           ffn_w_down, ln_ffn_g, ln_ffn_b, gdn_w_in, gdn_conv_w, gdn_a_log, gdn_dt_bias, gdn_norm_g,
           gdn_w_out, ln_gdn_g, ln_gdn_b, moe_router, moe_w_gate, moe_w_up, moe_w_down, ln_moe_g,
           ln_moe_b):
    B, S, D = x.shape
    x2 = x.reshape(B * S, D)
    x2bf = x2.astype(BF16)
    x2, x2bf = _attention_layer(x2, x2bf, B, S, attn_w_in[0], fox_forget_bias[0], attn_w_out[0],
                                ln_attn_g[0], ln_attn_b[0], ffn_w_gate[0], ffn_w_up[0], ffn_w_down[0],
                                ln_ffn_g[0], ln_ffn_b[0])
    y = _deltanet_layer(x2, x2bf, B, S, gdn_w_in[0], gdn_conv_w[0], gdn_a_log[0], gdn_dt_bias[0],
                        gdn_norm_g[0], gdn_w_out[0], ln_gdn_g[0], ln_gdn_b[0], moe_router[0],
                        moe_w_gate[0], moe_w_up[0], moe_w_down[0], ln_moe_g[0], ln_moe_b[0])
    return y.reshape(B, S, D)
```
